```python
import math
import jax, jax.numpy as jnp
from jax import lax
import numpy as np

D_MODEL = 2048
BATCH = 4
SEQ = 2048
DEPTH = 2

N_MIXERS = 2
ATT_HEADS = 16
ATT_KV_HEADS = 4
ATT_HEAD_DIM = D_MODEL // ATT_HEADS
IDX_HEADS = 16
IDX_HEAD_DIM = 64
TOPK_MAX = 256
Q_BLOCK = 128
ML_HEADS = 8
ML_V_DIM = D_MODEL // ML_HEADS
ML_QK_DIM = ML_V_DIM // 2
ML_CHUNK = 64
GATE_SOFTCAP = 15.0
D_FF = 4 * D_MODEL
ROPE_THETA = 500000.0
ROT_FRAC = 4
EPS = 1e-6

N_ATT_LAYERS = (DEPTH + 1) // 2
N_ML_LAYERS = DEPTH // 2

ATT_SIZES = [ATT_HEADS * ATT_HEAD_DIM, ATT_KV_HEADS * ATT_HEAD_DIM, ATT_KV_HEADS * ATT_HEAD_DIM,
             IDX_HEADS * IDX_HEAD_DIM, IDX_HEAD_DIM, IDX_HEADS]
ATT_IN = sum(ATT_SIZES)
ML_SIZES = [ML_HEADS * ML_QK_DIM, ML_HEADS * ML_QK_DIM, ML_HEADS * ML_V_DIM, D_MODEL, ML_HEADS, ML_HEADS]
ML_IN = sum(ML_SIZES)

kernel_name = "hybrid_dsa_mlstm_trunk"


def _split_points(sizes):
    return np.cumsum(sizes)[:-1].tolist()


def rms_norm(x, g):
    xf = x.astype(jnp.float32)
    y = xf * lax.rsqrt(jnp.mean(xf * xf, axis=-1, keepdims=True) + EPS)
    return (y * g.astype(jnp.float32)).astype(x.dtype)


def rope_partial(x, pos):
    rot = x.shape[-1] // ROT_FRAC
    half = rot // 2
    inv_freq = ROPE_THETA ** (-2.0 * jnp.arange(half, dtype=jnp.float32) / rot)
    ang = pos.astype(jnp.float32)[:, None] * inv_freq[None, :]
    cos = jnp.cos(ang)[:, None, :]
    sin = jnp.sin(ang)[:, None, :]
    xf = x.astype(jnp.float32)
    x1 = xf[..., :half]
    x2 = xf[..., half:rot]
    out = jnp.concatenate([x1 * cos - x2 * sin, x1 * sin + x2 * cos, xf[..., rot:]], axis=-1)
    return out.astype(x.dtype)


def dsa_attention(h, w_in, q_gain, k_gain, w_out):
    B, T, _ = h.shape
    H, G, Dh = ATT_HEADS, ATT_KV_HEADS, ATT_HEAD_DIM
    R = H // G
    Hi, Di = IDX_HEADS, IDX_HEAD_DIM
    proj = h @ w_in
    q, k, v, qi, ki, wi = jnp.split(proj, _split_points(ATT_SIZES), axis=-1)
    pos = jnp.arange(T)
    q = rope_partial(rms_norm(q.reshape(B, T, H, Dh), q_gain), pos)
    k = rope_partial(rms_norm(k.reshape(B, T, G, Dh), k_gain), pos)
    v = v.reshape(B, T, G, Dh)
    qi = rope_partial(qi.reshape(B, T, Hi, Di), pos)
    ki = rope_partial(ki.reshape(B, T, 1, Di), pos)[:, :, 0].astype(jnp.float32)
    wi = wi.astype(jnp.float32) * (Hi ** -0.5 * Di ** -0.5)
    topk = min(TOPK_MAX, T // 4)
    nb = T // Q_BLOCK

    def to_blocks(a):
        return jnp.moveaxis(a.reshape((B, nb, Q_BLOCK) + a.shape[2:]), 1, 0)

    key_pos = jnp.arange(T)

    def block_fn(args):
        q_b, qi_b, wi_b, t0 = args
        tpos = t0 + jnp.arange(Q_BLOCK)
        dots = jnp.einsum('bthd,bsd->bths', qi_b.astype(jnp.float32), ki)
        score = jnp.einsum('bths,bth->bts', jax.nn.relu(dots), wi_b)
        causal = key_pos[None, :] <= tpos[:, None]
        score = jnp.where(causal[None], score, -jnp.inf)
        _, idx = lax.top_k(score, topk)
        valid = idx <= tpos[None, :, None]
        k_sel = jax.vmap(lambda kb, ib: kb[ib])(k, idx).astype(jnp.float32)
        v_sel = jax.vmap(lambda vb, ib: vb[ib])(v, idx).astype(jnp.float32)
        qg = q_b.reshape(B, Q_BLOCK, G, R, Dh).astype(jnp.float32)
        logits = jnp.einsum('btgrd,btkgd->btgrk', qg, k_sel) * (Dh ** -0.5)
        logits = jnp.where(valid[:, :, None, None, :], logits, -jnp.inf)
        p = jax.nn.softmax(logits, axis=-1)
        o = jnp.einsum('btgrk,btkgd->btgrd', p, v_sel)
        return o.reshape(B, Q_BLOCK, H * Dh).astype(h.dtype)

    starts = jnp.arange(nb, dtype=jnp.int32) * Q_BLOCK
    out = lax.map(block_fn, (to_blocks(q), to_blocks(qi), to_blocks(wi), starts))
    out = jnp.moveaxis(out, 0, 1).reshape(B, T, H * Dh)
    return out @ w_out


def mlstm_chunkwise(q, k, v, ig, logf):
    B, H, T, dk = q.shape
    dv = v.shape[-1]
    L = ML_CHUNK
    NC = T // L
    q = q.reshape(B, H, NC, L, dk)
    k = k.reshape(B, H, NC, L, dk)
    v = v.reshape(B, H, NC, L, dv)
    ig = ig.reshape(B, H, NC, L)
    b = jnp.cumsum(logf.reshape(B, H, NC, L), axis=-1)
    a = b[..., -1]
    g = a[..., None] - b + ig
    m_loc = jnp.max(g, axis=-1)
    wgt = jnp.exp(g - m_loc[..., None])
    C_loc = jnp.einsum('bhcl,bhcld,bhcle->bhcde', wgt, k, v)
    n_loc = jnp.einsum('bhcl,bhcld->bhcd', wgt, k)

    def step(carry, xs):
        C, n, m = carry
        a_c, m_c, C_c, n_c = xs
        m_new = jnp.maximum(a_c + m, m_c)
        s_old = jnp.exp(a_c + m - m_new)
        s_new = jnp.exp(m_c - m_new)
        C2 = s_old[..., None, None] * C + s_new[..., None, None] * C_c
        n2 = s_old[..., None] * n + s_new[..., None] * n_c
        return (C2, n2, m_new), (C, n, m)

    init = (jnp.zeros((B, H, dk, dv), jnp.float32), jnp.zeros((B, H, dk), jnp.float32),
            jnp.zeros((B, H), jnp.float32))
    xs = (jnp.moveaxis(a, 2, 0), jnp.moveaxis(m_loc, 2, 0), jnp.moveaxis(C_loc, 2, 0), jnp.moveaxis(n_loc, 2, 0))
    _, (C_st, n_st, m_st) = lax.scan(step, init, xs)
    C_st = jnp.moveaxis(C_st, 0, 2)
    n_st = jnp.moveaxis(n_st, 0, 2)
    m_st = jnp.moveaxis(m_st, 0, 2)

    causal = jnp.tril(jnp.ones((L, L), dtype=bool))
    Dm = b[..., :, None] - b[..., None, :] + ig[..., None, :]
    Dm = jnp.where(causal, Dm, -jnp.inf)
    inter = b + m_st[..., None]
    m_t = jnp.maximum(inter, jnp.max(Dm, axis=-1))
    S = jnp.einsum('bhcld,bhcsd->bhcls', q, k) * jnp.exp(Dm - m_t[..., None])
    s_inter = jnp.exp(inter - m_t)
    num = jnp.einsum('bhcls,bhcse->bhcle', S, v) + s_inter[..., None] * jnp.einsum('bhcld,bhcde->bhcle', q, C_st)
    den = jnp.sum(S, axis=-1) + s_inter * jnp.einsum('bhcld,bhcd->bhcl', q, n_st)
    h_t = num / jnp.maximum(jnp.abs(den), jnp.exp(-m_t))[..., None]
    return h_t.reshape(B, H, T, dv)


def mlstm_mixer(h, w_in, b_gate, h_gain, w_out):
    B, T, _ = h.shape
    Hm, dk, dv = ML_HEADS, ML_QK_DIM, ML_V_DIM
    proj = h @ w_in
    q, k, v, o, ig, fg = jnp.split(proj, _split_points(ML_SIZES), axis=-1)
    q = q.reshape(B, T, Hm, dk).transpose(0, 2, 1, 3).astype(jnp.float32)
    k = k.reshape(B, T, Hm, dk).transpose(0, 2, 1, 3).astype(jnp.float32) * (dk ** -0.5)
    v = v.reshape(B, T, Hm, dv).transpose(0, 2, 1, 3).astype(jnp.float32)
    gates = jnp.concatenate([ig, fg], axis=-1).astype(jnp.float32) + b_gate.astype(jnp.float32)
    gates = GATE_SOFTCAP * jnp.tanh(gates / GATE_SOFTCAP)
    ig_pre = gates[..., :Hm].transpose(0, 2, 1)
    logf = jax.nn.log_sigmoid(gates[..., Hm:]).transpose(0, 2, 1)
    h_t = mlstm_chunkwise(q, k, v, ig_pre, logf)
    h_t = rms_norm(h_t, h_gain)
    h_t = h_t.transpose(0, 2, 1, 3).reshape(B, T, Hm * dv).astype(h.dtype)
    return (jax.nn.sigmoid(o) * h_t) @ w_out


def setup_inputs(seed: int = 0) -> dict:
    key = jax.random.key(seed)
    ks = jax.random.split(key, 16)

    def nrm(k, shape, scale):
        return jax.random.normal(k, shape, jnp.float32) * scale

    Hm = ML_HEADS
    b_in = nrm(ks[9], (N_ML_LAYERS, Hm), 0.1)
    b_f = 3.0 + nrm(ks[10], (N_ML_LAYERS, Hm), 0.1)
    return {
        "x": nrm(ks[0], (BATCH, SEQ, D_MODEL), 1.0),
        "norm_mix": 1.0 + nrm(ks[1], (DEPTH, D_MODEL), 0.1),
        "norm_ffn": 1.0 + nrm(ks[2], (DEPTH, D_MODEL), 0.1),
        "att_w_in": nrm(ks[3], (N_ATT_LAYERS, D_MODEL, ATT_IN), D_MODEL ** -0.5),
        "att_q_gain": 1.0 + nrm(ks[4], (N_ATT_LAYERS, ATT_HEAD_DIM), 0.1),
        "att_k_gain": 1.0 + nrm(ks[5], (N_ATT_LAYERS, ATT_HEAD_DIM), 0.1),
        "att_w_out": nrm(ks[6], (N_ATT_LAYERS, ATT_HEADS * ATT_HEAD_DIM, D_MODEL), (ATT_HEADS * ATT_HEAD_DIM) ** -0.5),
        "ml_w_in": nrm(ks[7], (N_ML_LAYERS, D_MODEL, ML_IN), D_MODEL ** -0.5),
        "ml_b_gate": jnp.concatenate([b_in, b_f], axis=-1),
        "ml_h_gain": 1.0 + nrm(ks[11], (N_ML_LAYERS, ML_V_DIM), 0.1),
        "ml_w_out": nrm(ks[12], (N_ML_LAYERS, ML_HEADS * ML_V_DIM, D_MODEL), (ML_HEADS * ML_V_DIM) ** -0.5),
        "ffn_w_up": nrm(ks[13], (DEPTH, D_MODEL, D_FF), D_MODEL ** -0.5),
        "ffn_w_down": nrm(ks[14], (DEPTH, D_FF, D_MODEL), D_FF ** -0.5),
    }


def reference(x, norm_mix, norm_ffn, att_w_in, att_q_gain, att_k_gain, att_w_out,
              ml_w_in, ml_b_gate, ml_h_gain, ml_w_out, ffn_w_up, ffn_w_down):
    h = x
    for i in range(DEPTH):
        j = i // N_MIXERS
        hn = rms_norm(h, norm_mix[i])
        if i % N_MIXERS == 0:
            y = dsa_attention(hn, att_w_in[j], att_q_gain[j], att_k_gain[j], att_w_out[j])
        else:
            y = mlstm_mixer(hn, ml_w_in[j], ml_b_gate[j], ml_h_gain[j], ml_w_out[j])
        h = h + y
        hn = rms_norm(h, norm_ffn[i])
        h = h + jnp.square(jax.nn.relu(hn @ ffn_w_up[i])) @ ffn_w_down[i]
    return h
```

```python
import functools

import jax
import jax.numpy as jnp
from jax import lax
from jax.experimental import pallas as pl
from jax.experimental.pallas import tpu as pltpu

D_MODEL = 2048
ATT_HEADS = 16
ATT_KV_HEADS = 4
ATT_HEAD_DIM = 128
IDX_HEADS = 16
IDX_HEAD_DIM = 64
TOPK_MAX = 256
ML_HEADS = 8
ML_V_DIM = 256
ML_QK_DIM = 128
GATE_SOFTCAP = 15.0
D_FF = 4 * D_MODEL
ROPE_THETA = 500000.0
ROT_FRAC = 4
EPS = 1e-6

LANES = 128
VMEM_LIMIT = 56 * 1024 * 1024
INT_MIN = -(2 ** 31)
NEG_BIG = -1e30

ATT_Q_BLOCK = 128
ML_CHUNK = 128
TIE_CHUNK = 256


def _params(*sem):
    return pltpu.CompilerParams(dimension_semantics=sem, vmem_limit_bytes=VMEM_LIMIT)


def _rmsnorm_kernel(x_ref, g_ref, o_ref):
    x = x_ref[...]
    ms = jnp.mean(x * x, axis=-1, keepdims=True)
    o_ref[...] = (x * lax.rsqrt(ms + EPS) * g_ref[...]).astype(o_ref.dtype)


def _rmsnorm(x, g, tm=512):
    n, d = x.shape
    return pl.pallas_call(
        _rmsnorm_kernel,
        grid=(n // tm,),
        in_specs=[pl.BlockSpec((tm, d), lambda i: (i, 0)),
                  pl.BlockSpec((1, d), lambda i: (0, 0))],
        out_specs=pl.BlockSpec((tm, d), lambda i: (i, 0)),
        out_shape=jax.ShapeDtypeStruct((n, d), jnp.bfloat16),
        compiler_params=_params("parallel"),
        name="rmsnorm",
    )(x, g.reshape(1, d))


def _rope_tables(seq, head_dim, heads_per_vreg):
    rot = head_dim // ROT_FRAC
    half = rot // 2
    inv_freq = ROPE_THETA ** (-2.0 * jnp.arange(half, dtype=jnp.float32) / rot)
    ang = jnp.arange(seq).astype(jnp.float32)[:, None] * inv_freq[None, :]
    cos, sin = jnp.cos(ang), jnp.sin(ang)
    ones = jnp.ones((seq, head_dim - rot), jnp.float32)
    zeros_h = jnp.zeros((seq, half), jnp.float32)
    zeros_r = jnp.zeros((seq, head_dim - rot), jnp.float32)
    c = jnp.concatenate([cos, cos, ones], axis=1)
    s1 = jnp.concatenate([-sin, zeros_h, zeros_r], axis=1)
    s2 = jnp.concatenate([zeros_h, sin, zeros_r], axis=1)
    rep = lambda t: jnp.tile(t, (1, heads_per_vreg))
    return rep(c), rep(s1), rep(s2), half


def _rope(y, c, s1, s2, half):
    return y * c + pltpu.roll(y, LANES - half, 1) * s1 + pltpu.roll(y, half, 1) * s2


def _proj_plain_kernel(x_ref, w_ref, o_ref):
    acc = jnp.dot(x_ref[...], w_ref[...], preferred_element_type=jnp.float32)
    o_ref[...] = acc.astype(o_ref.dtype)


def _proj_scale_kernel(x_ref, w_ref, s_ref, o_ref):
    acc = jnp.dot(x_ref[...], w_ref[...], preferred_element_type=jnp.float32)
    o_ref[...] = (acc * s_ref[...]).astype(o_ref.dtype)


def _proj_headnorm_rope_kernel(x_ref, w_ref, g_ref, c_ref, s1_ref, s2_ref, o_ref, *, half):
    acc = jnp.dot(x_ref[...], w_ref[...], preferred_element_type=jnp.float32)
    gain = g_ref[0]
    c, s1, s2 = c_ref[...], s1_ref[...], s2_ref[...]
    for grp in range(acc.shape[1] // LANES):
        xg = acc[:, grp * LANES:(grp + 1) * LANES]
        ms = jnp.mean(xg * xg, axis=-1, keepdims=True)
        y = xg * lax.rsqrt(ms + EPS) * gain
        o_ref[:, grp * LANES:(grp + 1) * LANES] = _rope(y, c, s1, s2, half).astype(o_ref.dtype)


def _proj_rope_kernel(x_ref, w_ref, c_ref, s1_ref, s2_ref, o_ref, *, half):
    acc = jnp.dot(x_ref[...], w_ref[...], preferred_element_type=jnp.float32)
    c, s1, s2 = c_ref[0], s1_ref[0], s2_ref[0]
    for grp in range(acc.shape[1] // LANES):
        xg = acc[:, grp * LANES:(grp + 1) * LANES]
        o_ref[:, grp * LANES:(grp + 1) * LANES] = _rope(xg, c, s1, s2, half).astype(o_ref.dtype)


def _proj(kernel, x, w, extra, extra_specs, out_dtype, tm, tn, name):
    n, d = x.shape
    nout = w.shape[1]
    return pl.pallas_call(
        kernel,
        grid=(n // tm, nout // tn),
        in_specs=[pl.BlockSpec((tm, d), lambda i, j: (i, 0)),
                  pl.BlockSpec((d, tn), lambda i, j: (0, j))] + extra_specs,
        out_specs=pl.BlockSpec((tm, tn), lambda i, j: (i, j)),
        out_shape=jax.ShapeDtypeStruct((n, nout), out_dtype),
        compiler_params=_params("parallel", "arbitrary"),
        name=name,
    )(x, w, *extra)


def _gates_kernel(w_ref, x_ref, b_ref, o_ref):
    g = lax.dot_general(w_ref[...], x_ref[...], (((1,), (1,)), ((), ())),
                        preferred_element_type=jnp.float32)
    g = g + b_ref[...]
    g = GATE_SOFTCAP * jnp.tanh(g / GATE_SOFTCAP)
    logf = jnp.minimum(g, 0.0) - jnp.log1p(jnp.exp(-jnp.abs(g)))
    is_forget = lax.broadcasted_iota(jnp.int32, g.shape, 0) >= ML_HEADS
    o_ref[...] = jnp.where(is_forget, logf, g)


def _gates(hn, wg_t, bias, tm=1024):
    n, d = hn.shape
    tm = min(tm, n)
    r = wg_t.shape[0]
    return pl.pallas_call(
        _gates_kernel,
        grid=(n // tm,),
        in_specs=[pl.BlockSpec((r, d), lambda i: (0, 0)),
                  pl.BlockSpec((tm, d), lambda i: (i, 0)),
                  pl.BlockSpec((r, 1), lambda i: (0, 0))],
        out_specs=pl.BlockSpec((r, tm), lambda i: (0, i)),
        out_shape=jax.ShapeDtypeStruct((r, n), jnp.float32),
        compiler_params=_params("parallel"),
        name="ml_gates",
    )(wg_t, hn, bias.reshape(r, 1))


def _row_count(mask):
    return jnp.sum(jnp.where(mask, 1.0, 0.0), axis=1, keepdims=True)


def _attention_kernel(q_ref, k_ref, v_ref, qi_ref, wq_ref, kk_ref, o_ref, *, topk):
    tq = q_ref.shape[0]
    t = k_ref.shape[0]
    q0 = pl.program_id(1) * tq

    ki = kk_ref[:, :IDX_HEAD_DIM].astype(jnp.bfloat16)
    score = jnp.zeros((tq, t), jnp.float32)
    for h in range(IDX_HEADS):
        qh = qi_ref[:, h * IDX_HEAD_DIM:(h + 1) * IDX_HEAD_DIM].astype(jnp.bfloat16)
        d = lax.dot_general(qh, ki, (((1,), (1,)), ((), ())), preferred_element_type=jnp.float32)
        w = wq_ref[:, IDX_HEAD_DIM + h:IDX_HEAD_DIM + h + 1]
        score = score + jnp.maximum(d, 0.0) * w

    tpos = q0 + lax.broadcasted_iota(jnp.int32, (tq, t), 0)
    spos = lax.broadcasted_iota(jnp.int32, (tq, t), 1)
    causal = spos <= tpos

    bits = pltpu.bitcast(score + 0.0, jnp.int32)
    key = jnp.where(bits < 0, bits ^ jnp.int32(0x7FFFFFFF), bits)
    key = jnp.where(causal, key, jnp.int32(INT_MIN))

    kf = jnp.float32(topk)

    def search(it, tau):
        cand = tau + lax.shift_left(jnp.int32(1), jnp.int32(31) - it)
        cnt = _row_count(key >= cand)
        return jnp.where(cnt >= kf, cand, tau)

    tau = lax.fori_loop(0, 32, search, jnp.full((tq, 1), INT_MIN, jnp.int32))

    ge = jnp.logical_and(key >= tau, causal)
    n_ge = _row_count(ge)

    def with_ties():
        gt = key > tau
        eq = jnp.logical_and(key == tau, causal)
        need = kf - _row_count(gt)
        r = lax.broadcasted_iota(jnp.int32, (TIE_CHUNK, TIE_CHUNK), 0)
        c = lax.broadcasted_iota(jnp.int32, (TIE_CHUNK, TIE_CHUNK), 1)
        before = jnp.where(r < c, 1.0, 0.0).astype(jnp.bfloat16)
        carry = jnp.zeros((tq, 1), jnp.float32)
        parts = []
        for ch in range(t // TIE_CHUNK):
            sl = slice(ch * TIE_CHUNK, (ch + 1) * TIE_CHUNK)
            e = jnp.where(eq[:, sl], 1.0, 0.0)
            prefix = jnp.dot(e.astype(jnp.bfloat16), before, preferred_element_type=jnp.float32) + carry
            carry = carry + jnp.sum(e, axis=1, keepdims=True)
            keep = jnp.logical_or(gt[:, sl], jnp.logical_and(eq[:, sl], prefix < need))
            parts.append(jnp.where(keep, 0.0, NEG_BIG))
        return jnp.concatenate(parts, axis=1)

    def no_ties():
        return jnp.where(ge, 0.0, NEG_BIG)

    bias = lax.cond(jnp.max(n_ge) > kf, with_ties, no_ties)

    scale = ATT_HEAD_DIM ** -0.5
    rep = ATT_HEADS // ATT_KV_HEADS
    for g in range(ATT_KV_HEADS):
        kg = k_ref[:, g * ATT_HEAD_DIM:(g + 1) * ATT_HEAD_DIM]
        vg = v_ref[:, g * ATT_HEAD_DIM:(g + 1) * ATT_HEAD_DIM]
        for r in range(rep):
            h = g * rep + r
            qh = q_ref[:, h * ATT_HEAD_DIM:(h + 1) * ATT_HEAD_DIM]
            s = lax.dot_general(qh, kg, (((1,), (1,)), ((), ())), preferred_element_type=jnp.float32)
            s = s * scale + bias
            m = jnp.max(s, axis=1, keepdims=True)
            p = jnp.exp(s - m)
            l = jnp.sum(p, axis=1, keepdims=True)
            o = jnp.dot(p.astype(jnp.bfloat16), vg, preferred_element_type=jnp.float32) / l
            o_ref[:, h * ATT_HEAD_DIM:(h + 1) * ATT_HEAD_DIM] = o.astype(o_ref.dtype)


def _attention(qk, v, idx, batch, seq):
    n = qk.shape[0]
    tq = ATT_Q_BLOCK
    nq = seq // tq
    topk = min(TOPK_MAX, seq // 4)
    q_w, kv_w = ATT_HEADS * ATT_HEAD_DIM, ATT_KV_HEADS * ATT_HEAD_DIM
    qi_w = IDX_HEADS * IDX_HEAD_DIM
    kw_blk = qi_w // LANES
    return pl.pallas_call(
        functools.partial(_attention_kernel, topk=topk),
        grid=(batch, nq),
        in_specs=[pl.BlockSpec((tq, q_w), lambda b, i: (b * nq + i, 0)),
                  pl.BlockSpec((seq, kv_w), lambda b, i: (b, q_w // kv_w)),
                  pl.BlockSpec((seq, kv_w), lambda b, i: (b, 0)),
                  pl.BlockSpec((tq, qi_w), lambda b, i: (b * nq + i, 0)),
                  pl.BlockSpec((tq, LANES), lambda b, i: (b * nq + i, kw_blk)),
                  pl.BlockSpec((seq, LANES), lambda b, i: (b, kw_blk))],
        out_specs=pl.BlockSpec((tq, q_w), lambda b, i: (b * nq + i, 0)),
        out_shape=jax.ShapeDtypeStruct((n, q_w), jnp.bfloat16),
        compiler_params=_params("parallel", "arbitrary"),
        name="dsa_attention",
    )(qk, qk, v, idx, idx, idx)


def _mlstm_kernel(q_ref, k_ref, v_ref, og_ref, ig_ref, lf_ref, hg_ref, o_ref):
    L = ML_CHUNK
    nc = q_ref.shape[0] // L
    dk, dv = q_ref.shape[1], v_ref.shape[1]
    row = lax.broadcasted_iota(jnp.int32, (L, L), 0)
    col = lax.broadcasted_iota(jnp.int32, (L, L), 1)
    tri = col <= row
    eye = col == row
    gain = hg_ref[...]

    def to_col(x_row):
        return jnp.sum(jnp.where(eye, jnp.broadcast_to(x_row, (L, L)), 0.0), axis=1, keepdims=True)

    def chunk(c, carry):
        C, n_row, m = carry
        off = pl.multiple_of(c * L, L)
        qc = q_ref[pl.ds(off, L), :]
        kc = k_ref[pl.ds(off, L), :]
        vc = v_ref[pl.ds(off, L), :]
        ig_r = ig_ref[0, c]
        lf_r = lf_ref[0, c]

        lf_b = jnp.broadcast_to(lf_r, (L, L))
        b_col = jnp.sum(jnp.where(tri, lf_b, 0.0), axis=1, keepdims=True)
        lf_col = to_col(lf_r)
        b_row = jnp.sum(jnp.where(row <= col, jnp.broadcast_to(lf_col, (L, L)), 0.0), axis=0, keepdims=True)
        a = jnp.sum(lf_r, axis=1, keepdims=True)

        g_row = a - b_row + ig_r
        m_loc = jnp.max(g_row, axis=1, keepdims=True)
        w_row = jnp.exp(g_row - m_loc)
        w_col = to_col(w_row)
        c_loc = lax.dot_general(kc, (w_col * vc.astype(jnp.float32)).astype(jnp.bfloat16),
                                (((0,), (0,)), ((), ())), preferred_element_type=jnp.float32)
        n_loc = jnp.dot(jnp.broadcast_to(w_row, (8, L)).astype(jnp.bfloat16), kc,
                        preferred_element_type=jnp.float32)[0:1]

        dm = jnp.where(tri, b_col - b_row + ig_r, -jnp.inf)
        inter = b_col + m
        m_t = jnp.maximum(inter, jnp.max(dm, axis=1, keepdims=True))
        s = lax.dot_general(qc, kc, (((1,), (1,)), ((), ())), preferred_element_type=jnp.float32)
        s = s * jnp.exp(dm - m_t)
        s_inter = jnp.exp(inter - m_t)
        qf = qc.astype(jnp.float32)
        num = (jnp.dot(s.astype(jnp.bfloat16), vc, preferred_element_type=jnp.float32)
               + s_inter * jnp.dot(qc, C.astype(jnp.bfloat16), preferred_element_type=jnp.float32))
        den = jnp.sum(s, axis=1, keepdims=True) + s_inter * jnp.sum(qf * n_row, axis=1, keepdims=True)
        h = num / jnp.maximum(jnp.abs(den), jnp.exp(-m_t))

        ms = jnp.mean(h * h, axis=-1, keepdims=True)
        hn = h * lax.rsqrt(ms + EPS) * gain
        og = og_ref[pl.ds(off, L), :]
        o_ref[pl.ds(off, L), :] = (jax.nn.sigmoid(og) * hn).astype(o_ref.dtype)

        m_new = jnp.maximum(a + m, m_loc)
        s_old = jnp.exp(a + m - m_new)
        s_new = jnp.exp(m_loc - m_new)
        return (s_old * C + s_new * c_loc, s_old * n_row + s_new * n_loc, m_new)

    init = (jnp.zeros((dk, dv), jnp.float32), jnp.zeros((1, dk), jnp.float32), jnp.zeros((1, 1), jnp.float32))
    lax.fori_loop(0, nc, chunk, init)


def _mlstm(qk, v, og, gates_t, h_gain, batch, seq):
    n = qk.shape[0]
    L = ML_CHUNK
    nc = seq // L
    g4 = gates_t.reshape(2 * ML_HEADS, batch * nc, 1, L)
    return pl.pallas_call(
        _mlstm_kernel,
        grid=(batch, ML_HEADS),
        in_specs=[pl.BlockSpec((seq, ML_QK_DIM), lambda b, h: (b, h)),
                  pl.BlockSpec((seq, ML_QK_DIM), lambda b, h: (b, ML_HEADS + h)),
                  pl.BlockSpec((seq, ML_V_DIM), lambda b, h: (b, h)),
                  pl.BlockSpec((seq, ML_V_DIM), lambda b, h: (b, h)),
                  pl.BlockSpec((1, nc, 1, L), lambda b, h: (h, b, 0, 0)),
                  pl.BlockSpec((1, nc, 1, L), lambda b, h: (ML_HEADS + h, b, 0, 0)),
                  pl.BlockSpec((1, ML_V_DIM), lambda b, h: (0, 0))],
        out_specs=pl.BlockSpec((seq, ML_V_DIM), lambda b, h: (b, h)),
        out_shape=jax.ShapeDtypeStruct((n, ML_HEADS * ML_V_DIM), jnp.bfloat16),
        compiler_params=_params("parallel", "parallel"),
        name="mlstm",
    )(qk, qk, v, og, g4, g4, h_gain.reshape(1, ML_V_DIM))


def _outproj_kernel(a_ref, w_ref, x_ref, o_ref):
    o_ref[...] = x_ref[...] + jnp.dot(a_ref[...], w_ref[...], preferred_element_type=jnp.float32)


def _outproj(a, w, x, tm=512):
    n, d = x.shape
    k = a.shape[1]
    return pl.pallas_call(
        _outproj_kernel,
        grid=(n // tm,),
        in_specs=[pl.BlockSpec((tm, k), lambda i: (i, 0)),
                  pl.BlockSpec((k, d), lambda i: (0, 0)),
                  pl.BlockSpec((tm, d), lambda i: (i, 0))],
        out_specs=pl.BlockSpec((tm, d), lambda i: (i, 0)),
        out_shape=jax.ShapeDtypeStruct((n, d), jnp.float32),
        compiler_params=_params("parallel"),
        name="outproj",
    )(a, w, x)


def _ffn_kernel(x_ref, g_ref, wu_ref, wd_ref, o_ref, hn_ref):
    @pl.when(pl.program_id(1) == 0)
    def _():
        x = x_ref[...]
        ms = jnp.mean(x * x, axis=-1, keepdims=True)
        hn_ref[...] = (x * lax.rsqrt(ms + EPS) * g_ref[...]).astype(hn_ref.dtype)
        o_ref[...] = x

    u = jnp.dot(hn_ref[...], wu_ref[...], preferred_element_type=jnp.float32)
    act = jnp.square(jnp.maximum(u, 0.0)).astype(jnp.bfloat16)
    o_ref[...] += jnp.dot(act, wd_ref[...], preferred_element_type=jnp.float32)


def _ffn(x, g, w_up, w_down, tm=512, tf=512):
    n, d = x.shape
    f = w_up.shape[1]
    return pl.pallas_call(
        _ffn_kernel,
        grid=(n // tm, f // tf),
        in_specs=[pl.BlockSpec((tm, d), lambda i, j: (i, 0)),
                  pl.BlockSpec((1, d), lambda i, j: (0, 0)),
                  pl.BlockSpec((d, tf), lambda i, j: (0, j)),
                  pl.BlockSpec((tf, d), lambda i, j: (j, 0))],
        out_specs=pl.BlockSpec((tm, d), lambda i, j: (i, 0)),
        out_shape=jax.ShapeDtypeStruct((n, d), jnp.float32),
        scratch_shapes=[pltpu.VMEM((tm, d), jnp.bfloat16)],
        compiler_params=_params("parallel", "arbitrary"),
        name="ffn",
    )(x, g.reshape(1, d), w_up, w_down)


def _attention_layer(h, batch, seq, g_mix, w_in, q_gain, k_gain, w_out):
    bf = jnp.bfloat16
    nq, nk = ATT_HEADS * ATT_HEAD_DIM, ATT_KV_HEADS * ATT_HEAD_DIM
    ni = IDX_HEADS * IDX_HEAD_DIM
    tm, tn = min(1024, seq), 512
    hn = _rmsnorm(h, g_mix)

    c, s1, s2, half = _rope_tables(seq, ATT_HEAD_DIM, 1)
    gains = jnp.stack([q_gain, k_gain]).reshape(2, 1, ATT_HEAD_DIM)
    n_qtiles = nq // tn
    tbl = pl.BlockSpec((tm, LANES), lambda i, j: (i % (seq // tm), 0))
    qk = _proj(functools.partial(_proj_headnorm_rope_kernel, half=half), hn, w_in[:, :nq + nk].astype(bf),
               (gains, c, s1, s2),
               [pl.BlockSpec((1, 1, ATT_HEAD_DIM), lambda i, j: (jnp.where(j < n_qtiles, 0, 1), 0, 0)), tbl, tbl, tbl],
               bf, tm, tn, "att_qk_proj")
    v = _proj(_proj_plain_kernel, hn, w_in[:, nq + nk:nq + 2 * nk].astype(bf), (), [], bf, tm, tn, "att_v_proj")

    ci, s1i, s2i, half_i = _rope_tables(seq, IDX_HEAD_DIM, LANES // IDX_HEAD_DIM)
    lane = jnp.arange(LANES)
    is_k = (lane < IDX_HEAD_DIM)[None, :]
    is_w = jnp.logical_and(lane >= IDX_HEAD_DIM, lane < IDX_HEAD_DIM + IDX_HEADS)[None, :]
    w_scale = IDX_HEADS ** -0.5 * IDX_HEAD_DIM ** -0.5
    tabs = [jnp.stack([ci, jnp.where(is_k, ci, jnp.where(is_w, w_scale, 0.0))]),
            jnp.stack([s1i, jnp.where(is_k, s1i, 0.0)]),
            jnp.stack([s2i, jnp.where(is_k, s2i, 0.0)])]
    w_idx = w_in[:, nq + 2 * nk:]
    pad = (-w_idx.shape[1]) % LANES
    w_idx = jnp.pad(w_idx, ((0, 0), (0, pad))).astype(bf)
    tn_i = LANES
    last = w_idx.shape[1] // tn_i - 1
    tbl_i = pl.BlockSpec((1, tm, LANES), lambda i, j: (jnp.where(j == last, 1, 0), i % (seq // tm), 0))
    idx = _proj(functools.partial(_proj_rope_kernel, half=half_i), hn, w_idx, tabs, [tbl_i, tbl_i, tbl_i],
                jnp.float32, tm, tn_i, "att_idx_proj")

    o = _attention(qk, v, idx, batch, seq)
    return _outproj(o, w_out.astype(bf), h)


def _mlstm_layer(h, batch, seq, g_mix, w_in, b_gate, h_gain, w_out):
    bf = jnp.bfloat16
    nqk, nv = ML_HEADS * ML_QK_DIM, ML_HEADS * ML_V_DIM
    tm, tn = min(1024, seq), 512
    hn = _rmsnorm(h, g_mix)
    scale = jnp.concatenate([jnp.ones((1, nqk), jnp.float32),
                             jnp.full((1, nqk), ML_QK_DIM ** -0.5, jnp.float32)], axis=1)
    qk = _proj(_proj_scale_kernel, hn, w_in[:, :2 * nqk].astype(bf), (scale,),
               [pl.BlockSpec((1, tn), lambda i, j: (0, j))], bf, tm, tn, "ml_qk_proj")
    v = _proj(_proj_plain_kernel, hn, w_in[:, 2 * nqk:2 * nqk + nv].astype(bf), (), [], bf, tm, tn, "ml_v_proj")
    og = _proj(_proj_plain_kernel, hn, w_in[:, 2 * nqk + nv:2 * nqk + nv + D_MODEL].astype(bf), (), [],
               jnp.float32, tm, tn, "ml_o_proj")
    gates_t = _gates(hn, w_in[:, 2 * nqk + nv + D_MODEL:].T.astype(bf), b_gate)
    y = _mlstm(qk, v, og, gates_t, h_gain, batch, seq)
    return _outproj(y, w_out.astype(bf), h)


def kernel(x, norm_mix, norm_ffn, att_w_in, att_q_gain, att_k_gain, att_w_out, ml_w_in, ml_b_gate, ml_h_gain,
           ml_w_out, ffn_w_up, ffn_w_down):
    batch, seq, d = x.shape
    bf = jnp.bfloat16
    h = x.reshape(batch * seq, d)
    for i in range(norm_mix.shape[0]):
        j = i // 2
        if i % 2 == 0:
            h = _attention_layer(h, batch, seq, norm_mix[i], att_w_in[j], att_q_gain[j], att_k_gain[j], att_w_out[j])
        else:
            h = _mlstm_layer(h, batch, seq, norm_mix[i], ml_w_in[j], ml_b_gate[j], ml_h_gain[j], ml_w_out[j])
        h = _ffn(h, norm_ffn[i], ffn_w_up[i].astype(bf), ffn_w_down[i].astype(bf))
    return h.reshape(batch, seq, d)
```

```python
import functools

import jax
import jax.numpy as jnp
from jax import lax
from jax.experimental import pallas as pl
from jax.experimental.pallas import tpu as pltpu

D_MODEL = 2048
ATT_HEADS = 16
ATT_KV_HEADS = 4
ATT_HEAD_DIM = 128
IDX_HEADS = 16
IDX_HEAD_DIM = 64
TOPK_MAX = 256
ML_HEADS = 8
ML_V_DIM = 256
ML_QK_DIM = 128
GATE_SOFTCAP = 15.0
D_FF = 4 * D_MODEL
ROPE_THETA = 500000.0
ROT_FRAC = 4
EPS = 1e-6

LANES = 128
VMEM_LIMIT = 56 * 1024 * 1024
INT_MIN = -(2 ** 31)
NEG_BIG = -1e30

LOG2_E = 1.4426950408889634

ATT_Q_BLOCK = 256
ATT_KEY_CHUNK = 256
SEARCH_UNROLL = 4
ML_CHUNK = 128


def _params(*sem):
    return pltpu.CompilerParams(dimension_semantics=sem, vmem_limit_bytes=VMEM_LIMIT)


def _rmsnorm_kernel(x_ref, g_ref, o_ref):
    x = x_ref[...]
    ms = jnp.mean(x * x, axis=-1, keepdims=True)
    o_ref[...] = (x * lax.rsqrt(ms + EPS) * g_ref[...]).astype(o_ref.dtype)


def _rmsnorm(x, g, tm=512):
    n, d = x.shape
    return pl.pallas_call(
        _rmsnorm_kernel,
        grid=(n // tm,),
        in_specs=[pl.BlockSpec((tm, d), lambda i: (i, 0)),
                  pl.BlockSpec((1, d), lambda i: (0, 0))],
        out_specs=pl.BlockSpec((tm, d), lambda i: (i, 0)),
        out_shape=jax.ShapeDtypeStruct((n, d), jnp.bfloat16),
        compiler_params=_params("parallel"),
        name="rmsnorm",
    )(x, g.reshape(1, d))


def _rope_tables(seq, head_dim, heads_per_vreg):
    rot = head_dim // ROT_FRAC
    half = rot // 2
    inv_freq = ROPE_THETA ** (-2.0 * jnp.arange(half, dtype=jnp.float32) / rot)
    ang = jnp.arange(seq).astype(jnp.float32)[:, None] * inv_freq[None, :]
    cos, sin = jnp.cos(ang), jnp.sin(ang)
    ones = jnp.ones((seq, head_dim - rot), jnp.float32)
    zeros_h = jnp.zeros((seq, half), jnp.float32)
    zeros_r = jnp.zeros((seq, head_dim - rot), jnp.float32)
    c = jnp.concatenate([cos, cos, ones], axis=1)
    s1 = jnp.concatenate([-sin, zeros_h, zeros_r], axis=1)
    s2 = jnp.concatenate([zeros_h, sin, zeros_r], axis=1)
    rep = lambda t: jnp.tile(t, (1, heads_per_vreg))
    return rep(c), rep(s1), rep(s2), half


def _rope(y, c, s1, s2, half):
    return y * c + pltpu.roll(y, LANES - half, 1) * s1 + pltpu.roll(y, half, 1) * s2


def _proj_plain_kernel(x_ref, w_ref, o_ref):
    acc = jnp.dot(x_ref[...], w_ref[...], preferred_element_type=jnp.float32)
    o_ref[...] = acc.astype(o_ref.dtype)


def _proj_scale_kernel(x_ref, w_ref, s_ref, o_ref):
    acc = jnp.dot(x_ref[...], w_ref[...], preferred_element_type=jnp.float32)
    o_ref[...] = (acc * s_ref[...]).astype(o_ref.dtype)


def _proj_headnorm_rope_kernel(x_ref, w_ref, g_ref, c_ref, s1_ref, s2_ref, o_ref, *, half):
    acc = jnp.dot(x_ref[...], w_ref[...], preferred_element_type=jnp.float32)
    gain = g_ref[0]
    c, s1, s2 = c_ref[...], s1_ref[...], s2_ref[...]
    for grp in range(acc.shape[1] // LANES):
        xg = acc[:, grp * LANES:(grp + 1) * LANES]
        ms = jnp.mean(xg * xg, axis=-1, keepdims=True)
        y = xg * lax.rsqrt(ms + EPS) * gain
        o_ref[:, grp * LANES:(grp + 1) * LANES] = _rope(y, c, s1, s2, half).astype(o_ref.dtype)


def _proj_rope_kernel(x_ref, w_ref, c_ref, s1_ref, s2_ref, o_ref, *, half):
    acc = jnp.dot(x_ref[...], w_ref[...], preferred_element_type=jnp.float32)
    c, s1, s2 = c_ref[0], s1_ref[0], s2_ref[0]
    for grp in range(acc.shape[1] // LANES):
        xg = acc[:, grp * LANES:(grp + 1) * LANES]
        o_ref[:, grp * LANES:(grp + 1) * LANES] = _rope(xg, c, s1, s2, half).astype(o_ref.dtype)


def _proj(kernel, x, w, extra, extra_specs, out_dtype, tm, tn, name):
    n, d = x.shape
    nout = w.shape[1]
    return pl.pallas_call(
        kernel,
        grid=(n // tm, nout // tn),
        in_specs=[pl.BlockSpec((tm, d), lambda i, j: (i, 0)),
                  pl.BlockSpec((d, tn), lambda i, j: (0, j))] + extra_specs,
        out_specs=pl.BlockSpec((tm, tn), lambda i, j: (i, j)),
        out_shape=jax.ShapeDtypeStruct((n, nout), out_dtype),
        compiler_params=_params("parallel", "arbitrary"),
        name=name,
    )(x, w, *extra)


def _gates_kernel(w_ref, x_ref, b_ref, o_ref):
    g = lax.dot_general(w_ref[...], x_ref[...], (((1,), (1,)), ((), ())),
                        preferred_element_type=jnp.float32)
    g = g + b_ref[...]
    g = GATE_SOFTCAP * jnp.tanh(g / GATE_SOFTCAP)
    logf = jnp.minimum(g, 0.0) - jnp.log1p(jnp.exp(-jnp.abs(g)))
    is_forget = lax.broadcasted_iota(jnp.int32, g.shape, 0) >= ML_HEADS
    o_ref[...] = jnp.where(is_forget, logf, g)


def _gates(hn, wg_t, bias, tm=1024):
    n, d = hn.shape
    tm = min(tm, n)
    r = wg_t.shape[0]
    return pl.pallas_call(
        _gates_kernel,
        grid=(n // tm,),
        in_specs=[pl.BlockSpec((r, d), lambda i: (0, 0)),
                  pl.BlockSpec((tm, d), lambda i: (i, 0)),
                  pl.BlockSpec((r, 1), lambda i: (0, 0))],
        out_specs=pl.BlockSpec((r, tm), lambda i: (0, i)),
        out_shape=jax.ShapeDtypeStruct((r, n), jnp.float32),
        compiler_params=_params("parallel"),
        name="ml_gates",
    )(wg_t, hn, bias.reshape(r, 1))


def _key_to_f32(key):
    return pltpu.bitcast(jnp.where(key < 0, key ^ jnp.int32(0x7FFFFFFF), key), jnp.float32)


def _select_topk(score_ref, bias_ref, *, topk, n_chunks):
    _, tq, ck = score_ref.shape
    kf = jnp.float32(topk)

    def count(pred, thr):
        part = jnp.zeros((tq, LANES), jnp.float32)
        for c in range(n_chunks):
            for j in range(ck // LANES):
                part = part + jnp.where(pred(score_ref[c, :, j * LANES:(j + 1) * LANES], thr), 1.0, 0.0)
        return jnp.sum(part, axis=1, keepdims=True)

    key_ninf = INT_MIN + 0x7FFFFF
    key_pinf = 0x7F800000

    def search(it, tau):
        cand = tau + lax.shift_left(jnp.int32(1), jnp.int32(31) - it)
        in_range = jnp.logical_and(cand > tau, cand <= key_pinf)
        cnt = count(jnp.greater_equal, _key_to_f32(cand))
        return jnp.where(jnp.logical_and(in_range, cnt >= kf), cand, tau)

    tau = lax.fori_loop(0, 32, search, jnp.full((tq, 1), key_ninf, jnp.int32), unroll=SEARCH_UNROLL)
    thr = _key_to_f32(tau)
    finite = lambda s, th: jnp.logical_and(s >= th, s > -jnp.inf)
    n_ge = count(finite, thr)

    @pl.when(jnp.max(n_ge) <= kf)
    def _():
        for c in range(n_chunks):
            bias_ref[c] = jnp.where(finite(score_ref[c], thr), 0.0, NEG_BIG)

    @pl.when(jnp.max(n_ge) > kf)
    def _():
        need = kf - count(jnp.greater, thr)
        r = lax.broadcasted_iota(jnp.int32, (ck, ck), 0)
        col = lax.broadcasted_iota(jnp.int32, (ck, ck), 1)
        before = jnp.where(r < col, 1.0, 0.0).astype(jnp.bfloat16)
        seen = jnp.zeros((tq, 1), jnp.float32)
        for c in range(n_chunks):
            s = score_ref[c]
            eq = jnp.logical_and(s == thr, s > -jnp.inf)
            e = jnp.where(eq, 1.0, 0.0)
            prefix = jnp.dot(e.astype(jnp.bfloat16), before, preferred_element_type=jnp.float32) + seen
            seen = seen + jnp.sum(e, axis=1, keepdims=True)
            keep = jnp.logical_or(s > thr, jnp.logical_and(eq, prefix < need))
            bias_ref[c] = jnp.where(keep, 0.0, NEG_BIG)


def _for_each_chunk(n, body):
    def pair(p, carry):
        body(2 * p)
        body(2 * p + 1)
        return carry

    lax.fori_loop(0, n // 2, pair, 0)
    pl.when(n % 2 == 1)(lambda: body(n - 1))


def _attention_kernel(q_ref, k_ref, v_ref, qi_ref, wq_ref, kk_ref, o_ref,
                      qs_ref, wb_ref, score_ref, bias_ref, q4_ref, s_ref, m_ref, l_ref, acc_ref, *, topk):
    tq = q_ref.shape[0]
    ck = ATT_KEY_CHUNK
    half = tq // 2
    nt = (((1,), (1,)), ((), ()))
    i = pl.program_id(1)
    q0 = i * tq
    n_chunks = (q0 + tq) // ck

    for r in range(2):
        rows = slice(r * half, (r + 1) * half)
        for h in range(IDX_HEADS):
            qs_ref[r, h * half:(h + 1) * half, :] = (
                qi_ref[rows, h * IDX_HEAD_DIM:(h + 1) * IDX_HEAD_DIM].astype(jnp.bfloat16))
            wb_ref[r * IDX_HEADS + h] = jnp.broadcast_to(
                wq_ref[rows, IDX_HEAD_DIM + h:IDX_HEAD_DIM + h + 1], (half, LANES))

    def score_chunk(c):
        off = pl.multiple_of(c * ck, ck)
        ki = kk_ref[pl.ds(off, ck), :].astype(jnp.bfloat16)[:, :IDX_HEAD_DIM]
        spos = off + lax.broadcasted_iota(jnp.int32, (half, LANES), 1)
        for r in range(2):
            d = lax.dot_general(qs_ref[r], ki, nt, preferred_element_type=jnp.float32)
            tpos = q0 + r * half + lax.broadcasted_iota(jnp.int32, (half, LANES), 0)
            for j in range(ck // LANES):
                acc = jnp.zeros((half, LANES), jnp.float32)
                for h in range(IDX_HEADS):
                    dh = d[h * half:(h + 1) * half, j * LANES:(j + 1) * LANES]
                    acc = acc + jnp.maximum(dh, 0.0) * wb_ref[r * IDX_HEADS + h]
                score_ref[c, r * half:(r + 1) * half, j * LANES:(j + 1) * LANES] = (
                    jnp.where(spos + j * LANES <= tpos, acc, -jnp.inf))

    _for_each_chunk(n_chunks, score_chunk)

    for n in range(1, score_ref.shape[0] + 1):
        pl.when(n_chunks == n)(functools.partial(_select_topk, score_ref, bias_ref, topk=topk, n_chunks=n))

    exp2_scale = ATT_HEAD_DIM ** -0.5 * LOG2_E
    rep = ATT_HEADS // ATT_KV_HEADS
    for g in range(ATT_KV_HEADS):
        cols = slice(g * ATT_HEAD_DIM, (g + 1) * ATT_HEAD_DIM)
        for r in range(rep):
            h = g * rep + r
            q4_ref[r * tq:(r + 1) * tq, :] = q_ref[:, h * ATT_HEAD_DIM:(h + 1) * ATT_HEAD_DIM]
        m_ref[...] = jnp.full(m_ref.shape, NEG_BIG, jnp.float32)

        def logits_chunk(c):
            off = pl.multiple_of(c * ck, ck)
            s = lax.dot_general(q4_ref[...], k_ref[pl.ds(off, ck), cols], nt, preferred_element_type=jnp.float32)
            s = (s.reshape(rep, tq, ck) + bias_ref[c][None]).reshape(rep * tq, ck)
            s_ref[c] = s
            m = m_ref[...]
            for j in range(ck // LANES):
                m = jnp.maximum(m, s[:, j * LANES:(j + 1) * LANES])
            m_ref[...] = m

        _for_each_chunk(n_chunks, logits_chunk)
        m_ref[...] = jnp.broadcast_to(jnp.max(m_ref[...], axis=1, keepdims=True), m_ref.shape)
        l_ref[...] = jnp.zeros(l_ref.shape, jnp.float32)
        acc_ref[...] = jnp.zeros(acc_ref.shape, jnp.float32)

        def pv_chunk(c):
            off = pl.multiple_of(c * ck, ck)
            s = s_ref[c]
            m = m_ref[...]
            p = jnp.exp2((s - jnp.concatenate([m] * (ck // LANES), axis=1)) * exp2_scale)
            l = l_ref[...]
            for j in range(ck // LANES):
                l = l + p[:, j * LANES:(j + 1) * LANES]
            l_ref[...] = l
            acc_ref[...] += jnp.dot(p.astype(jnp.bfloat16), v_ref[pl.ds(off, ck), cols],
                                    preferred_element_type=jnp.float32)

        _for_each_chunk(n_chunks, pv_chunk)
        o4 = acc_ref[...] / jnp.sum(l_ref[...], axis=1, keepdims=True)
        for r in range(rep):
            h = g * rep + r
            o_ref[:, h * ATT_HEAD_DIM:(h + 1) * ATT_HEAD_DIM] = o4[r * tq:(r + 1) * tq].astype(o_ref.dtype)


def _attention(qk, v, idx, batch, seq):
    n = qk.shape[0]
    tq, ck = ATT_Q_BLOCK, ATT_KEY_CHUNK
    nq = seq // tq
    rep = ATT_HEADS // ATT_KV_HEADS
    topk = min(TOPK_MAX, seq // 4)
    q_w, kv_w = ATT_HEADS * ATT_HEAD_DIM, ATT_KV_HEADS * ATT_HEAD_DIM
    qi_w = IDX_HEADS * IDX_HEAD_DIM
    kw_blk = qi_w // LANES
    return pl.pallas_call(
        functools.partial(_attention_kernel, topk=topk),
        grid=(batch, nq),
        in_specs=[pl.BlockSpec((tq, q_w), lambda b, i: (b * nq + i, 0)),
                  pl.BlockSpec((seq, kv_w), lambda b, i: (b, q_w // kv_w)),
                  pl.BlockSpec((seq, kv_w), lambda b, i: (b, 0)),
                  pl.BlockSpec((tq, qi_w), lambda b, i: (b * nq + i, 0)),
                  pl.BlockSpec((tq, LANES), lambda b, i: (b * nq + i, kw_blk)),
                  pl.BlockSpec((seq, LANES), lambda b, i: (b, kw_blk))],
        out_specs=pl.BlockSpec((tq, q_w), lambda b, i: (b * nq + i, 0)),
        out_shape=jax.ShapeDtypeStruct((n, q_w), jnp.bfloat16),
        scratch_shapes=[pltpu.VMEM((2, IDX_HEADS * tq // 2, IDX_HEAD_DIM), jnp.bfloat16),
                        pltpu.VMEM((2 * IDX_HEADS, tq // 2, LANES), jnp.float32),
                        pltpu.VMEM((seq // ck, tq, ck), jnp.float32),
                        pltpu.VMEM((seq // ck, tq, ck), jnp.float32),
                        pltpu.VMEM((rep * tq, ATT_HEAD_DIM), jnp.bfloat16),
                        pltpu.VMEM((seq // ck, rep * tq, ck), jnp.float32),
                        pltpu.VMEM((rep * tq, LANES), jnp.float32),
                        pltpu.VMEM((rep * tq, LANES), jnp.float32),
                        pltpu.VMEM((rep * tq, ATT_HEAD_DIM), jnp.float32)],
        compiler_params=_params("parallel", "arbitrary"),
        name="dsa_attention",
    )(qk, qk, v, idx, idx, idx)


def _mlstm_kernel(q_ref, k_ref, v_ref, og_ref, ig_ref, lf_ref, hg_ref, o_ref):
    L = ML_CHUNK
    nc = q_ref.shape[0] // L
    dk, dv = q_ref.shape[1], v_ref.shape[1]
    row = lax.broadcasted_iota(jnp.int32, (L, L), 0)
    col = lax.broadcasted_iota(jnp.int32, (L, L), 1)
    tri = col <= row
    eye = col == row
    gain = hg_ref[...]

    def to_col(x_row):
        return jnp.sum(jnp.where(eye, jnp.broadcast_to(x_row, (L, L)), 0.0), axis=1, keepdims=True)

    def chunk(c, carry):
        C, n_row, m = carry
        off = pl.multiple_of(c * L, L)
        qc = q_ref[pl.ds(off, L), :]
        kc = k_ref[pl.ds(off, L), :]
        vc = v_ref[pl.ds(off, L), :]
        ig_r = ig_ref[0, c]
        lf_r = lf_ref[0, c]

        lf_b = jnp.broadcast_to(lf_r, (L, L))
        b_col = jnp.sum(jnp.where(tri, lf_b, 0.0), axis=1, keepdims=True)
        lf_col = to_col(lf_r)
        b_row = jnp.sum(jnp.where(row <= col, jnp.broadcast_to(lf_col, (L, L)), 0.0), axis=0, keepdims=True)
        a = jnp.sum(lf_r, axis=1, keepdims=True)

        g_row = a - b_row + ig_r
        m_loc = jnp.max(g_row, axis=1, keepdims=True)
        w_row = jnp.exp(g_row - m_loc)
        w_col = to_col(w_row)
        c_loc = lax.dot_general(kc, (w_col * vc.astype(jnp.float32)).astype(jnp.bfloat16),
                                (((0,), (0,)), ((), ())), preferred_element_type=jnp.float32)
        n_loc = jnp.dot(jnp.broadcast_to(w_row, (8, L)).astype(jnp.bfloat16), kc,
                        preferred_element_type=jnp.float32)[0:1]

        dm = jnp.where(tri, b_col - b_row + ig_r, -jnp.inf)
        inter = b_col + m
        m_t = jnp.maximum(inter, jnp.max(dm, axis=1, keepdims=True))
        s = lax.dot_general(qc, kc, (((1,), (1,)), ((), ())), preferred_element_type=jnp.float32)
        s = s * jnp.exp(dm - m_t)
        s_inter = jnp.exp(inter - m_t)
        qf = qc.astype(jnp.float32)
        num = (jnp.dot(s.astype(jnp.bfloat16), vc, preferred_element_type=jnp.float32)
               + s_inter * jnp.dot(qc, C.astype(jnp.bfloat16), preferred_element_type=jnp.float32))
        den = jnp.sum(s, axis=1, keepdims=True) + s_inter * jnp.sum(qf * n_row, axis=1, keepdims=True)
        h = num / jnp.maximum(jnp.abs(den), jnp.exp(-m_t))

        ms = jnp.mean(h * h, axis=-1, keepdims=True)
        hn = h * lax.rsqrt(ms + EPS) * gain
        og = og_ref[pl.ds(off, L), :]
        o_ref[pl.ds(off, L), :] = (jax.nn.sigmoid(og) * hn).astype(o_ref.dtype)

        m_new = jnp.maximum(a + m, m_loc)
        s_old = jnp.exp(a + m - m_new)
        s_new = jnp.exp(m_loc - m_new)
        return (s_old * C + s_new * c_loc, s_old * n_row + s_new * n_loc, m_new)

    init = (jnp.zeros((dk, dv), jnp.float32), jnp.zeros((1, dk), jnp.float32), jnp.zeros((1, 1), jnp.float32))
    lax.fori_loop(0, nc, chunk, init)


def _mlstm(qk, v, og, gates_t, h_gain, batch, seq):
    n = qk.shape[0]
    L = ML_CHUNK
    nc = seq // L
    g4 = gates_t.reshape(2 * ML_HEADS, batch * nc, 1, L)
    return pl.pallas_call(
        _mlstm_kernel,
        grid=(batch, ML_HEADS),
        in_specs=[pl.BlockSpec((seq, ML_QK_DIM), lambda b, h: (b, h)),
                  pl.BlockSpec((seq, ML_QK_DIM), lambda b, h: (b, ML_HEADS + h)),
                  pl.BlockSpec((seq, ML_V_DIM), lambda b, h: (b, h)),
                  pl.BlockSpec((seq, ML_V_DIM), lambda b, h: (b, h)),
                  pl.BlockSpec((1, nc, 1, L), lambda b, h: (h, b, 0, 0)),
                  pl.BlockSpec((1, nc, 1, L), lambda b, h: (ML_HEADS + h, b, 0, 0)),
                  pl.BlockSpec((1, ML_V_DIM), lambda b, h: (0, 0))],
        out_specs=pl.BlockSpec((seq, ML_V_DIM), lambda b, h: (b, h)),
        out_shape=jax.ShapeDtypeStruct((n, ML_HEADS * ML_V_DIM), jnp.bfloat16),
        compiler_params=_params("parallel", "parallel"),
        name="mlstm",
    )(qk, qk, v, og, g4, g4, h_gain.reshape(1, ML_V_DIM))


def _outproj_kernel(a_ref, w_ref, x_ref, o_ref):
    o_ref[...] = x_ref[...] + jnp.dot(a_ref[...], w_ref[...], preferred_element_type=jnp.float32)


def _outproj(a, w, x, tm=512):
    n, d = x.shape
    k = a.shape[1]
    return pl.pallas_call(
        _outproj_kernel,
        grid=(n // tm,),
        in_specs=[pl.BlockSpec((tm, k), lambda i: (i, 0)),
                  pl.BlockSpec((k, d), lambda i: (0, 0)),
                  pl.BlockSpec((tm, d), lambda i: (i, 0))],
        out_specs=pl.BlockSpec((tm, d), lambda i: (i, 0)),
        out_shape=jax.ShapeDtypeStruct((n, d), jnp.float32),
        compiler_params=_params("parallel"),
        name="outproj",
    )(a, w, x)


def _ffn_kernel(x_ref, g_ref, wu_ref, wd_ref, o_ref, hn_ref):
    @pl.when(pl.program_id(1) == 0)
    def _():
        x = x_ref[...]
        ms = jnp.mean(x * x, axis=-1, keepdims=True)
        hn_ref[...] = (x * lax.rsqrt(ms + EPS) * g_ref[...]).astype(hn_ref.dtype)
        o_ref[...] = x

    u = jnp.dot(hn_ref[...], wu_ref[...], preferred_element_type=jnp.float32)
    act = jnp.square(jnp.maximum(u, 0.0)).astype(jnp.bfloat16)
    o_ref[...] += jnp.dot(act, wd_ref[...], preferred_element_type=jnp.float32)


def _ffn(x, g, w_up, w_down, tm=512, tf=512):
    n, d = x.shape
    f = w_up.shape[1]
    return pl.pallas_call(
        _ffn_kernel,
        grid=(n // tm, f // tf),
        in_specs=[pl.BlockSpec((tm, d), lambda i, j: (i, 0)),
                  pl.BlockSpec((1, d), lambda i, j: (0, 0)),
                  pl.BlockSpec((d, tf), lambda i, j: (0, j)),
                  pl.BlockSpec((tf, d), lambda i, j: (j, 0))],
        out_specs=pl.BlockSpec((tm, d), lambda i, j: (i, 0)),
        out_shape=jax.ShapeDtypeStruct((n, d), jnp.float32),
        scratch_shapes=[pltpu.VMEM((tm, d), jnp.bfloat16)],
        compiler_params=_params("parallel", "arbitrary"),
        name="ffn",
    )(x, g.reshape(1, d), w_up, w_down)


def _attention_layer(h, batch, seq, g_mix, w_in, q_gain, k_gain, w_out):
    bf = jnp.bfloat16
    nq, nk = ATT_HEADS * ATT_HEAD_DIM, ATT_KV_HEADS * ATT_HEAD_DIM
    ni = IDX_HEADS * IDX_HEAD_DIM
    tm, tn = min(1024, seq), 512
    hn = _rmsnorm(h, g_mix)

    c, s1, s2, half = _rope_tables(seq, ATT_HEAD_DIM, 1)
    gains = jnp.stack([q_gain, k_gain]).reshape(2, 1, ATT_HEAD_DIM)
    n_qtiles = nq // tn
    tbl = pl.BlockSpec((tm, LANES), lambda i, j: (i % (seq // tm), 0))
    qk = _proj(functools.partial(_proj_headnorm_rope_kernel, half=half), hn, w_in[:, :nq + nk].astype(bf),
               (gains, c, s1, s2),
               [pl.BlockSpec((1, 1, ATT_HEAD_DIM), lambda i, j: (jnp.where(j < n_qtiles, 0, 1), 0, 0)), tbl, tbl, tbl],
               bf, tm, tn, "att_qk_proj")
    v = _proj(_proj_plain_kernel, hn, w_in[:, nq + nk:nq + 2 * nk].astype(bf), (), [], bf, tm, tn, "att_v_proj")

    ci, s1i, s2i, half_i = _rope_tables(seq, IDX_HEAD_DIM, LANES // IDX_HEAD_DIM)
    lane = jnp.arange(LANES)
    is_k = (lane < IDX_HEAD_DIM)[None, :]
    is_w = jnp.logical_and(lane >= IDX_HEAD_DIM, lane < IDX_HEAD_DIM + IDX_HEADS)[None, :]
    w_scale = IDX_HEADS ** -0.5 * IDX_HEAD_DIM ** -0.5
    tabs = [jnp.stack([ci, jnp.where(is_k, ci, jnp.where(is_w, w_scale, 0.0))]),
            jnp.stack([s1i, jnp.where(is_k, s1i, 0.0)]),
            jnp.stack([s2i, jnp.where(is_k, s2i, 0.0)])]
    w_idx = w_in[:, nq + 2 * nk:]
    pad = (-w_idx.shape[1]) % LANES
    w_idx = jnp.pad(w_idx, ((0, 0), (0, pad))).astype(bf)
    tn_i = LANES
    last = w_idx.shape[1] // tn_i - 1
    tbl_i = pl.BlockSpec((1, tm, LANES), lambda i, j: (jnp.where(j == last, 1, 0), i % (seq // tm), 0))
    idx = _proj(functools.partial(_proj_rope_kernel, half=half_i), hn, w_idx, tabs, [tbl_i, tbl_i, tbl_i],
                jnp.float32, tm, tn_i, "att_idx_proj")

    o = _attention(qk, v, idx, batch, seq)
    return _outproj(o, w_out.astype(bf), h)


def _mlstm_layer(h, batch, seq, g_mix, w_in, b_gate, h_gain, w_out):
    bf = jnp.bfloat16
    nqk, nv = ML_HEADS * ML_QK_DIM, ML_HEADS * ML_V_DIM
    tm, tn = min(1024, seq), 512
    hn = _rmsnorm(h, g_mix)
    scale = jnp.concatenate([jnp.ones((1, nqk), jnp.float32),
                             jnp.full((1, nqk), ML_QK_DIM ** -0.5, jnp.float32)], axis=1)
    qk = _proj(_proj_scale_kernel, hn, w_in[:, :2 * nqk].astype(bf), (scale,),
               [pl.BlockSpec((1, tn), lambda i, j: (0, j))], bf, tm, tn, "ml_qk_proj")
    v = _proj(_proj_plain_kernel, hn, w_in[:, 2 * nqk:2 * nqk + nv].astype(bf), (), [], bf, tm, tn, "ml_v_proj")
    og = _proj(_proj_plain_kernel, hn, w_in[:, 2 * nqk + nv:2 * nqk + nv + D_MODEL].astype(bf), (), [],
               jnp.float32, tm, tn, "ml_o_proj")
    gates_t = _gates(hn, w_in[:, 2 * nqk + nv + D_MODEL:].T.astype(bf), b_gate)
    y = _mlstm(qk, v, og, gates_t, h_gain, batch, seq)
    return _outproj(y, w_out.astype(bf), h)


def kernel(x, norm_mix, norm_ffn, att_w_in, att_q_gain, att_k_gain, att_w_out, ml_w_in, ml_b_gate, ml_h_gain,
           ml_w_out, ffn_w_up, ffn_w_down):
    batch, seq, d = x.shape
    bf = jnp.bfloat16
    h = x.reshape(batch * seq, d)
    for i in range(norm_mix.shape[0]):
        j = i // 2
        if i % 2 == 0:
            h = _attention_layer(h, batch, seq, norm_mix[i], att_w_in[j], att_q_gain[j], att_k_gain[j], att_w_out[j])
        else:
            h = _mlstm_layer(h, batch, seq, norm_mix[i], ml_w_in[j], ml_b_gate[j], ml_h_gain[j], ml_w_out[j])
        h = _ffn(h, norm_ffn[i], ffn_w_up[i].astype(bf), ffn_w_down[i].astype(bf))
    return h.reshape(batch, seq, d)
```

```python
import functools

import jax
import jax.numpy as jnp
from jax import lax
from jax.experimental import pallas as pl
from jax.experimental.pallas import tpu as pltpu

D_MODEL = 2048
ATT_HEADS = 16
ATT_KV_HEADS = 4
ATT_HEAD_DIM = 128
IDX_HEADS = 16
IDX_HEAD_DIM = 64
TOPK_MAX = 256
ML_HEADS = 8
ML_V_DIM = 256
ML_QK_DIM = 128
GATE_SOFTCAP = 15.0
D_FF = 4 * D_MODEL
ROPE_THETA = 500000.0
ROT_FRAC = 4
EPS = 1e-6

LANES = 128
VMEM_LIMIT = 56 * 1024 * 1024
INT_MIN = -(2 ** 31)
NEG_BIG = -1e30

LOG2_E = 1.4426950408889634

ATT_Q_BLOCK = 256
ATT_KEY_CHUNK = 256
SEARCH_UNROLL = 4
ML_CHUNK = 128
PROJ_SUB = 512
FFN_SUB = 256


def _params(*sem):
    return pltpu.CompilerParams(dimension_semantics=sem, vmem_limit_bytes=VMEM_LIMIT)


def _store_rmsnorm(x_ref, g_ref, hn_ref):
    x = x_ref[...]
    ms = jnp.mean(x * x, axis=-1, keepdims=True)
    hn_ref[...] = (x * lax.rsqrt(ms + EPS) * g_ref[...]).astype(hn_ref.dtype)

def _rope_tables(seq, head_dim, heads_per_vreg):
    rot = head_dim // ROT_FRAC
    half = rot // 2
    inv_freq = ROPE_THETA ** (-2.0 * jnp.arange(half, dtype=jnp.float32) / rot)
    ang = jnp.arange(seq).astype(jnp.float32)[:, None] * inv_freq[None, :]
    cos, sin = jnp.cos(ang), jnp.sin(ang)
    ones = jnp.ones((seq, head_dim - rot), jnp.float32)
    zeros_h = jnp.zeros((seq, half), jnp.float32)
    zeros_r = jnp.zeros((seq, head_dim - rot), jnp.float32)
    c = jnp.concatenate([cos, cos, ones], axis=1)
    s1 = jnp.concatenate([-sin, zeros_h, zeros_r], axis=1)
    s2 = jnp.concatenate([zeros_h, sin, zeros_r], axis=1)
    rep = lambda t: jnp.tile(t, (1, heads_per_vreg))
    return rep(c), rep(s1), rep(s2), half


def _rope(y, c, s1, s2, half):
    return y * c + pltpu.roll(y, LANES - half, 1) * s1 + pltpu.roll(y, half, 1) * s2


def _att_proj_kernel(x_ref, g_ref, w_ref, gain_ref, tab_ref, tabi_ref, tabt_ref,
                     qk_ref, v_ref, qi_ref, kw_ref, hn_ref, *, half, half_i):
    _store_rmsnorm(x_ref, g_ref, hn_ref)
    n_qk, n_v, n_qi = qk_ref.shape[1], v_ref.shape[1], qi_ref.shape[1]
    n_q = ATT_HEADS * ATT_HEAD_DIM

    def cols(start, width):
        return jnp.dot(hn_ref[...], w_ref[:, start:start + width], preferred_element_type=jnp.float32)

    for s in range(n_qk // PROJ_SUB):
        acc = cols(s * PROJ_SUB, PROJ_SUB)
        gain = gain_ref[0 if s * PROJ_SUB < n_q else 1]
        for grp in range(PROJ_SUB // LANES):
            xg = acc[:, grp * LANES:(grp + 1) * LANES]
            ms = jnp.mean(xg * xg, axis=-1, keepdims=True)
            y = _rope(xg * lax.rsqrt(ms + EPS) * gain, tab_ref[0], tab_ref[1], tab_ref[2], half)
            qk_ref[:, s * PROJ_SUB + grp * LANES:s * PROJ_SUB + (grp + 1) * LANES] = y.astype(qk_ref.dtype)

    v_ref[...] = cols(n_qk, n_v).astype(v_ref.dtype)

    for s in range(n_qi // PROJ_SUB):
        acc = cols(n_qk + n_v + s * PROJ_SUB, PROJ_SUB)
        for grp in range(PROJ_SUB // LANES):
            y = _rope(acc[:, grp * LANES:(grp + 1) * LANES], tabi_ref[0], tabi_ref[1], tabi_ref[2], half_i)
            qi_ref[:, s * PROJ_SUB + grp * LANES:s * PROJ_SUB + (grp + 1) * LANES] = y.astype(qi_ref.dtype)

    kw_ref[...] = _rope(cols(n_qk + n_v + n_qi, LANES), tabt_ref[0], tabt_ref[1], tabt_ref[2], half_i)


def _ml_proj_kernel(x_ref, g_ref, w_ref, wg_ref, b_ref, qk_ref, v_ref, gates_ref, hn_ref):
    _store_rmsnorm(x_ref, g_ref, hn_ref)
    n_qk = qk_ref.shape[1]
    for s in range((n_qk + v_ref.shape[1]) // PROJ_SUB):
        acc = jnp.dot(hn_ref[...], w_ref[:, s * PROJ_SUB:(s + 1) * PROJ_SUB], preferred_element_type=jnp.float32)
        start = s * PROJ_SUB
        if start < n_qk // 2:
            qk_ref[:, start:start + PROJ_SUB] = acc.astype(qk_ref.dtype)
        elif start < n_qk:
            qk_ref[:, start:start + PROJ_SUB] = (acc * ML_QK_DIM ** -0.5).astype(qk_ref.dtype)
        else:
            v_ref[:, start - n_qk:start - n_qk + PROJ_SUB] = acc.astype(v_ref.dtype)

    g = lax.dot_general(wg_ref[...], hn_ref[...], (((1,), (1,)), ((), ())),
                        preferred_element_type=jnp.float32)
    g = g + b_ref[...]
    g = GATE_SOFTCAP * jnp.tanh(g / GATE_SOFTCAP)
    logf = jnp.minimum(g, 0.0) - jnp.log1p(jnp.exp(-jnp.abs(g)))
    is_forget = lax.broadcasted_iota(jnp.int32, g.shape, 0) >= ML_HEADS
    gates_ref[...] = jnp.where(is_forget, logf, g)


def _norm_proj_kernel(x_ref, g_ref, w_ref, o_ref, hn_ref):
    _store_rmsnorm(x_ref, g_ref, hn_ref)
    for s in range(o_ref.shape[1] // PROJ_SUB):
        cols = slice(s * PROJ_SUB, (s + 1) * PROJ_SUB)
        o_ref[:, cols] = jnp.dot(hn_ref[...], w_ref[:, cols], preferred_element_type=jnp.float32).astype(o_ref.dtype)


def _key_to_f32(key):
    return pltpu.bitcast(jnp.where(key < 0, key ^ jnp.int32(0x7FFFFFFF), key), jnp.float32)


def _select_topk(score_ref, bias_ref, *, topk, n_chunks):
    _, tq, ck = score_ref.shape
    kf = jnp.float32(topk)

    def count(pred, thr):
        part = jnp.zeros((tq, LANES), jnp.float32)
        for c in range(n_chunks):
            for j in range(ck // LANES):
                part = part + jnp.where(pred(score_ref[c, :, j * LANES:(j + 1) * LANES], thr), 1.0, 0.0)
        return jnp.sum(part, axis=1, keepdims=True)

    key_ninf = INT_MIN + 0x7FFFFF
    key_pinf = 0x7F800000

    def search(it, tau):
        cand = tau + lax.shift_left(jnp.int32(1), jnp.int32(31) - it)
        in_range = jnp.logical_and(cand > tau, cand <= key_pinf)
        cnt = count(jnp.greater_equal, _key_to_f32(cand))
        return jnp.where(jnp.logical_and(in_range, cnt >= kf), cand, tau)

    tau = lax.fori_loop(0, 32, search, jnp.full((tq, 1), key_ninf, jnp.int32), unroll=SEARCH_UNROLL)
    thr = _key_to_f32(tau)
    finite = lambda s, th: jnp.logical_and(s >= th, s > -jnp.inf)
    n_ge = count(finite, thr)

    @pl.when(jnp.max(n_ge) <= kf)
    def _():
        for c in range(n_chunks):
            bias_ref[c] = jnp.where(finite(score_ref[c], thr), 0.0, NEG_BIG)

    @pl.when(jnp.max(n_ge) > kf)
    def _():
        need = kf - count(jnp.greater, thr)
        r = lax.broadcasted_iota(jnp.int32, (ck, ck), 0)
        col = lax.broadcasted_iota(jnp.int32, (ck, ck), 1)
        before = jnp.where(r < col, 1.0, 0.0).astype(jnp.bfloat16)
        seen = jnp.zeros((tq, 1), jnp.float32)
        for c in range(n_chunks):
            s = score_ref[c]
            eq = jnp.logical_and(s == thr, s > -jnp.inf)
            e = jnp.where(eq, 1.0, 0.0)
            prefix = jnp.dot(e.astype(jnp.bfloat16), before, preferred_element_type=jnp.float32) + seen
            seen = seen + jnp.sum(e, axis=1, keepdims=True)
            keep = jnp.logical_or(s > thr, jnp.logical_and(eq, prefix < need))
            bias_ref[c] = jnp.where(keep, 0.0, NEG_BIG)


def _for_each_chunk(n, body):
    def pair(p, carry):
        body(2 * p)
        body(2 * p + 1)
        return carry

    lax.fori_loop(0, n // 2, pair, 0)
    pl.when(n % 2 == 1)(lambda: body(n - 1))


def _attention_kernel(q_ref, k_ref, v_ref, qi_ref, wq_ref, kk_ref, o_ref,
                      qs_ref, wb_ref, score_ref, bias_ref, q4_ref, s_ref, m_ref, l_ref, acc_ref, *, topk):
    tq = q_ref.shape[0]
    ck = ATT_KEY_CHUNK
    half = tq // 2
    nt = (((1,), (1,)), ((), ()))
    i = pl.program_id(1)
    q0 = i * tq
    n_chunks = (q0 + tq) // ck

    for r in range(2):
        rows = slice(r * half, (r + 1) * half)
        for h in range(IDX_HEADS):
            qs_ref[r, h * half:(h + 1) * half, :] = (
                qi_ref[rows, h * IDX_HEAD_DIM:(h + 1) * IDX_HEAD_DIM].astype(jnp.bfloat16))
            wb_ref[r * IDX_HEADS + h] = jnp.broadcast_to(
                wq_ref[rows, IDX_HEAD_DIM + h:IDX_HEAD_DIM + h + 1], (half, LANES))

    def score_chunk(c):
        off = pl.multiple_of(c * ck, ck)
        ki = kk_ref[pl.ds(off, ck), :].astype(jnp.bfloat16)[:, :IDX_HEAD_DIM]
        spos = off + lax.broadcasted_iota(jnp.int32, (half, LANES), 1)
        for r in range(2):
            d = lax.dot_general(qs_ref[r], ki, nt, preferred_element_type=jnp.float32)
            tpos = q0 + r * half + lax.broadcasted_iota(jnp.int32, (half, LANES), 0)
            for j in range(ck // LANES):
                acc = jnp.zeros((half, LANES), jnp.float32)
                for h in range(IDX_HEADS):
                    dh = d[h * half:(h + 1) * half, j * LANES:(j + 1) * LANES]
                    acc = acc + jnp.maximum(dh, 0.0) * wb_ref[r * IDX_HEADS + h]
                score_ref[c, r * half:(r + 1) * half, j * LANES:(j + 1) * LANES] = (
                    jnp.where(spos + j * LANES <= tpos, acc, -jnp.inf))

    _for_each_chunk(n_chunks, score_chunk)

    for n in range(1, score_ref.shape[0] + 1):
        pl.when(n_chunks == n)(functools.partial(_select_topk, score_ref, bias_ref, topk=topk, n_chunks=n))

    exp2_scale = ATT_HEAD_DIM ** -0.5 * LOG2_E
    rep = ATT_HEADS // ATT_KV_HEADS
    for g in range(ATT_KV_HEADS):
        cols = slice(g * ATT_HEAD_DIM, (g + 1) * ATT_HEAD_DIM)
        for r in range(rep):
            h = g * rep + r
            q4_ref[r * tq:(r + 1) * tq, :] = q_ref[:, h * ATT_HEAD_DIM:(h + 1) * ATT_HEAD_DIM]
        m_ref[...] = jnp.full(m_ref.shape, NEG_BIG, jnp.float32)

        def logits_chunk(c):
            off = pl.multiple_of(c * ck, ck)
            s = lax.dot_general(q4_ref[...], k_ref[pl.ds(off, ck), cols], nt, preferred_element_type=jnp.float32)
            s = (s.reshape(rep, tq, ck) + bias_ref[c][None]).reshape(rep * tq, ck)
            s_ref[c] = s
            m = m_ref[...]
            for j in range(ck // LANES):
                m = jnp.maximum(m, s[:, j * LANES:(j + 1) * LANES])
            m_ref[...] = m

        _for_each_chunk(n_chunks, logits_chunk)
        m_ref[...] = jnp.broadcast_to(jnp.max(m_ref[...], axis=1, keepdims=True), m_ref.shape)
        l_ref[...] = jnp.zeros(l_ref.shape, jnp.float32)
        acc_ref[...] = jnp.zeros(acc_ref.shape, jnp.float32)

        def pv_chunk(c):
            off = pl.multiple_of(c * ck, ck)
            s = s_ref[c]
            m = m_ref[...]
            p = jnp.exp2((s - jnp.concatenate([m] * (ck // LANES), axis=1)) * exp2_scale)
            l = l_ref[...]
            for j in range(ck // LANES):
                l = l + p[:, j * LANES:(j + 1) * LANES]
            l_ref[...] = l
            acc_ref[...] += jnp.dot(p.astype(jnp.bfloat16), v_ref[pl.ds(off, ck), cols],
                                    preferred_element_type=jnp.float32)

        _for_each_chunk(n_chunks, pv_chunk)
        o4 = acc_ref[...] / jnp.sum(l_ref[...], axis=1, keepdims=True)
        for r in range(rep):
            h = g * rep + r
            o_ref[:, h * ATT_HEAD_DIM:(h + 1) * ATT_HEAD_DIM] = o4[r * tq:(r + 1) * tq].astype(o_ref.dtype)


def _attention(qk, v, qi, kw, batch, seq):
    n = qk.shape[0]
    tq, ck = ATT_Q_BLOCK, ATT_KEY_CHUNK
    nq = seq // tq
    rep = ATT_HEADS // ATT_KV_HEADS
    topk = min(TOPK_MAX, seq // 4)
    q_w, kv_w = ATT_HEADS * ATT_HEAD_DIM, ATT_KV_HEADS * ATT_HEAD_DIM
    qi_w = IDX_HEADS * IDX_HEAD_DIM
    return pl.pallas_call(
        functools.partial(_attention_kernel, topk=topk),
        grid=(batch, nq),
        in_specs=[pl.BlockSpec((tq, q_w), lambda b, i: (b * nq + i, 0)),
                  pl.BlockSpec((seq, kv_w), lambda b, i: (b, q_w // kv_w)),
                  pl.BlockSpec((seq, kv_w), lambda b, i: (b, 0)),
                  pl.BlockSpec((tq, qi_w), lambda b, i: (b * nq + i, 0)),
                  pl.BlockSpec((tq, LANES), lambda b, i: (b * nq + i, 0)),
                  pl.BlockSpec((seq, LANES), lambda b, i: (b, 0))],
        out_specs=pl.BlockSpec((tq, q_w), lambda b, i: (b * nq + i, 0)),
        out_shape=jax.ShapeDtypeStruct((n, q_w), jnp.bfloat16),
        scratch_shapes=[pltpu.VMEM((2, IDX_HEADS * tq // 2, IDX_HEAD_DIM), jnp.bfloat16),
                        pltpu.VMEM((2 * IDX_HEADS, tq // 2, LANES), jnp.float32),
                        pltpu.VMEM((seq // ck, tq, ck), jnp.float32),
                        pltpu.VMEM((seq // ck, tq, ck), jnp.float32),
                        pltpu.VMEM((rep * tq, ATT_HEAD_DIM), jnp.bfloat16),
                        pltpu.VMEM((seq // ck, rep * tq, ck), jnp.float32),
                        pltpu.VMEM((rep * tq, LANES), jnp.float32),
                        pltpu.VMEM((rep * tq, LANES), jnp.float32),
                        pltpu.VMEM((rep * tq, ATT_HEAD_DIM), jnp.float32)],
        compiler_params=_params("parallel", "arbitrary"),
        name="dsa_attention",
    )(qk, qk, v, qi, kw, kw)


def _mlstm_kernel(q_ref, k_ref, v_ref, og_ref, ig_ref, lf_ref, hg_ref, o_ref):
    L = ML_CHUNK
    nc = q_ref.shape[0] // L
    dk, dv = q_ref.shape[1], v_ref.shape[1]
    row = lax.broadcasted_iota(jnp.int32, (L, L), 0)
    col = lax.broadcasted_iota(jnp.int32, (L, L), 1)
    tri = col <= row
    eye = col == row
    gain = hg_ref[...]

    def to_col(x_row):
        return jnp.sum(jnp.where(eye, jnp.broadcast_to(x_row, (L, L)), 0.0), axis=1, keepdims=True)

    def chunk(c, carry):
        C, n_row, m = carry
        off = pl.multiple_of(c * L, L)
        qc = q_ref[pl.ds(off, L), :]
        kc = k_ref[pl.ds(off, L), :]
        vc = v_ref[pl.ds(off, L), :]
        ig_r = ig_ref[0, c]
        lf_r = lf_ref[0, c]

        lf_b = jnp.broadcast_to(lf_r, (L, L))
        b_col = jnp.sum(jnp.where(tri, lf_b, 0.0), axis=1, keepdims=True)
        lf_col = to_col(lf_r)
        b_row = jnp.sum(jnp.where(row <= col, jnp.broadcast_to(lf_col, (L, L)), 0.0), axis=0, keepdims=True)
        a = jnp.sum(lf_r, axis=1, keepdims=True)

        g_row = a - b_row + ig_r
        m_loc = jnp.max(g_row, axis=1, keepdims=True)
        w_row = jnp.exp(g_row - m_loc)
        w_col = to_col(w_row)
        c_loc = lax.dot_general(kc, (w_col * vc.astype(jnp.float32)).astype(jnp.bfloat16),
                                (((0,), (0,)), ((), ())), preferred_element_type=jnp.float32)
        n_loc = jnp.dot(jnp.broadcast_to(w_row, (8, L)).astype(jnp.bfloat16), kc,
                        preferred_element_type=jnp.float32)[0:1]

        dm = jnp.where(tri, b_col - b_row + ig_r, -jnp.inf)
        inter = b_col + m
        m_t = jnp.maximum(inter, jnp.max(dm, axis=1, keepdims=True))
        s = lax.dot_general(qc, kc, (((1,), (1,)), ((), ())), preferred_element_type=jnp.float32)
        s = s * jnp.exp(dm - m_t)
        s_inter = jnp.exp(inter - m_t)
        qf = qc.astype(jnp.float32)
        num = (jnp.dot(s.astype(jnp.bfloat16), vc, preferred_element_type=jnp.float32)
               + s_inter * jnp.dot(qc, C.astype(jnp.bfloat16), preferred_element_type=jnp.float32))
        den = jnp.sum(s, axis=1, keepdims=True) + s_inter * jnp.sum(qf * n_row, axis=1, keepdims=True)
        h = num / jnp.maximum(jnp.abs(den), jnp.exp(-m_t))

        ms = jnp.mean(h * h, axis=-1, keepdims=True)
        hn = h * lax.rsqrt(ms + EPS) * gain
        og = og_ref[pl.ds(off, L), :]
        o_ref[pl.ds(off, L), :] = (jax.nn.sigmoid(og) * hn).astype(o_ref.dtype)

        m_new = jnp.maximum(a + m, m_loc)
        s_old = jnp.exp(a + m - m_new)
        s_new = jnp.exp(m_loc - m_new)
        return (s_old * C + s_new * c_loc, s_old * n_row + s_new * n_loc, m_new)

    init = (jnp.zeros((dk, dv), jnp.float32), jnp.zeros((1, dk), jnp.float32), jnp.zeros((1, 1), jnp.float32))
    lax.fori_loop(0, nc, chunk, init)


def _mlstm(qk, v, og, gates_t, h_gain, batch, seq):
    n = qk.shape[0]
    L = ML_CHUNK
    nc = seq // L
    g4 = gates_t.reshape(2 * ML_HEADS, batch * nc, 1, L)
    return pl.pallas_call(
        _mlstm_kernel,
        grid=(batch, ML_HEADS),
        in_specs=[pl.BlockSpec((seq, ML_QK_DIM), lambda b, h: (b, h)),
                  pl.BlockSpec((seq, ML_QK_DIM), lambda b, h: (b, ML_HEADS + h)),
                  pl.BlockSpec((seq, ML_V_DIM), lambda b, h: (b, h)),
                  pl.BlockSpec((seq, ML_V_DIM), lambda b, h: (b, h)),
                  pl.BlockSpec((1, nc, 1, L), lambda b, h: (h, b, 0, 0)),
                  pl.BlockSpec((1, nc, 1, L), lambda b, h: (ML_HEADS + h, b, 0, 0)),
                  pl.BlockSpec((1, ML_V_DIM), lambda b, h: (0, 0))],
        out_specs=pl.BlockSpec((seq, ML_V_DIM), lambda b, h: (b, h)),
        out_shape=jax.ShapeDtypeStruct((n, ML_HEADS * ML_V_DIM), jnp.bfloat16),
        compiler_params=_params("parallel", "parallel"),
        name="mlstm",
    )(qk, qk, v, og, g4, g4, h_gain.reshape(1, ML_V_DIM))


def _outproj_kernel(a_ref, w_ref, x_ref, o_ref):
    o_ref[...] = x_ref[...] + jnp.dot(a_ref[...], w_ref[...], preferred_element_type=jnp.float32)


def _outproj(a, w, x, tm=512):
    n, d = x.shape
    k = a.shape[1]
    return pl.pallas_call(
        _outproj_kernel,
        grid=(n // tm,),
        in_specs=[pl.BlockSpec((tm, k), lambda i: (i, 0)),
                  pl.BlockSpec((k, d), lambda i: (0, 0)),
                  pl.BlockSpec((tm, d), lambda i: (i, 0))],
        out_specs=pl.BlockSpec((tm, d), lambda i: (i, 0)),
        out_shape=jax.ShapeDtypeStruct((n, d), jnp.float32),
        compiler_params=_params("parallel"),
        name="outproj",
    )(a, w, x)


def _ffn_kernel(x_ref, g_ref, wu_ref, wd_ref, o_ref, hn_ref):
    @pl.when(pl.program_id(1) == 0)
    def _():
        x = x_ref[...]
        ms = jnp.mean(x * x, axis=-1, keepdims=True)
        hn_ref[...] = (x * lax.rsqrt(ms + EPS) * g_ref[...]).astype(hn_ref.dtype)
        o_ref[...] = x

    acts = []
    for s in range(wu_ref.shape[1] // FFN_SUB):
        u = jnp.dot(hn_ref[...], wu_ref[:, s * FFN_SUB:(s + 1) * FFN_SUB], preferred_element_type=jnp.float32)
        acts.append(jnp.square(jnp.maximum(u, 0.0)).astype(jnp.bfloat16))
    o_ref[...] += jnp.dot(jnp.concatenate(acts, axis=1), wd_ref[...], preferred_element_type=jnp.float32)


def _ffn(x, g, w_up, w_down, tm=1024, tf=512):
    n, d = x.shape
    tm = min(tm, n)
    f = w_up.shape[1]
    return pl.pallas_call(
        _ffn_kernel,
        grid=(n // tm, f // tf),
        in_specs=[pl.BlockSpec((tm, d), lambda i, j: (i, 0)),
                  pl.BlockSpec((1, d), lambda i, j: (0, 0)),
                  pl.BlockSpec((d, tf), lambda i, j: (0, j)),
                  pl.BlockSpec((tf, d), lambda i, j: (j, 0))],
        out_specs=pl.BlockSpec((tm, d), lambda i, j: (i, 0)),
        out_shape=jax.ShapeDtypeStruct((n, d), jnp.float32),
        scratch_shapes=[pltpu.VMEM((tm, d), jnp.bfloat16)],
        compiler_params=_params("parallel", "arbitrary"),
        name="ffn",
    )(x, g.reshape(1, d), w_up, w_down)


def _attention_layer(h, batch, seq, g_mix, w_in, q_gain, k_gain, w_out):
    bf = jnp.bfloat16
    n, d = h.shape
    n_qk = (ATT_HEADS + ATT_KV_HEADS) * ATT_HEAD_DIM
    n_v = ATT_KV_HEADS * ATT_HEAD_DIM
    n_qi = IDX_HEADS * IDX_HEAD_DIM
    tm = min(512, seq)

    c, s1, s2, half = _rope_tables(seq, ATT_HEAD_DIM, 1)
    ci, s1i, s2i, half_i = _rope_tables(seq, IDX_HEAD_DIM, LANES // IDX_HEAD_DIM)
    lane = jnp.arange(LANES)
    is_k = (lane < IDX_HEAD_DIM)[None, :]
    is_w = jnp.logical_and(lane >= IDX_HEAD_DIM, lane < IDX_HEAD_DIM + IDX_HEADS)[None, :]
    w_scale = IDX_HEADS ** -0.5 * IDX_HEAD_DIM ** -0.5
    tab = jnp.stack([c, s1, s2])
    tab_i = jnp.stack([ci, s1i, s2i])
    tab_t = jnp.stack([jnp.where(is_k, ci, jnp.where(is_w, w_scale, 0.0)),
                       jnp.where(is_k, s1i, 0.0), jnp.where(is_k, s2i, 0.0)])
    gains = jnp.stack([q_gain, k_gain]).reshape(2, 1, ATT_HEAD_DIM)
    w = jnp.pad(w_in, ((0, 0), (0, (-w_in.shape[1]) % LANES))).astype(bf)

    row = lambda width: pl.BlockSpec((tm, width), lambda i: (i, 0))
    whole = lambda a: pl.BlockSpec(a.shape, lambda i: (0,) * a.ndim, pipeline_mode=pl.Buffered(1))
    tbl = pl.BlockSpec((3, tm, LANES), lambda i: (0, i % (seq // tm), 0))
    qk, v, qi, kw = pl.pallas_call(
        functools.partial(_att_proj_kernel, half=half, half_i=half_i),
        grid=(n // tm,),
        in_specs=[row(d), whole(g_mix.reshape(1, d)), whole(w), whole(gains), tbl, tbl, tbl],
        out_specs=[row(n_qk), row(n_v), row(n_qi), row(LANES)],
        out_shape=[jax.ShapeDtypeStruct((n, n_qk), bf), jax.ShapeDtypeStruct((n, n_v), bf),
                   jax.ShapeDtypeStruct((n, n_qi), bf), jax.ShapeDtypeStruct((n, LANES), jnp.float32)],
        scratch_shapes=[pltpu.VMEM((tm, d), bf)],
        compiler_params=_params("parallel"),
        name="att_proj",
    )(h, g_mix.reshape(1, d), w, gains, tab, tab_i, tab_t)

    o = _attention(qk, v, qi, kw, batch, seq)
    return _outproj(o, w_out.astype(bf), h)


def _mlstm_layer(h, batch, seq, g_mix, w_in, b_gate, h_gain, w_out):
    bf = jnp.bfloat16
    n, d = h.shape
    n_qk, n_v = 2 * ML_HEADS * ML_QK_DIM, ML_HEADS * ML_V_DIM
    n_g = 2 * ML_HEADS
    tm = min(512, seq)
    g2 = g_mix.reshape(1, d)
    w_qkv = w_in[:, :n_qk + n_v].astype(bf)
    w_o = w_in[:, n_qk + n_v:n_qk + n_v + d].astype(bf)
    wg_t = w_in[:, n_qk + n_v + d:].T.astype(bf)

    row = lambda width: pl.BlockSpec((tm, width), lambda i: (i, 0))
    whole = lambda a: pl.BlockSpec(a.shape, lambda i: (0,) * a.ndim, pipeline_mode=pl.Buffered(1))
    qk, v, gates_t = pl.pallas_call(
        _ml_proj_kernel,
        grid=(n // tm,),
        in_specs=[row(d), whole(g2), whole(w_qkv), whole(wg_t), pl.BlockSpec((n_g, 1), lambda i: (0, 0))],
        out_specs=[row(n_qk), row(n_v), pl.BlockSpec((n_g, tm), lambda i: (0, i))],
        out_shape=[jax.ShapeDtypeStruct((n, n_qk), bf), jax.ShapeDtypeStruct((n, n_v), bf),
                   jax.ShapeDtypeStruct((n_g, n), jnp.float32)],
        scratch_shapes=[pltpu.VMEM((tm, d), bf)],
        compiler_params=_params("parallel"),
        name="ml_proj",
    )(h, g2, w_qkv, wg_t, b_gate.reshape(n_g, 1))
    og = pl.pallas_call(
        _norm_proj_kernel,
        grid=(n // tm,),
        in_specs=[row(d), whole(g2), whole(w_o)],
        out_specs=row(d),
        out_shape=jax.ShapeDtypeStruct((n, d), jnp.float32),
        scratch_shapes=[pltpu.VMEM((tm, d), bf)],
        compiler_params=_params("parallel"),
        name="ml_o_proj",
    )(h, g2, w_o)
    y = _mlstm(qk, v, og, gates_t, h_gain, batch, seq)
    return _outproj(y, w_out.astype(bf), h)


def kernel(x, norm_mix, norm_ffn, att_w_in, att_q_gain, att_k_gain, att_w_out, ml_w_in, ml_b_gate, ml_h_gain,
           ml_w_out, ffn_w_up, ffn_w_down):
    batch, seq, d = x.shape
    bf = jnp.bfloat16
    h = x.reshape(batch * seq, d)
    for i in range(norm_mix.shape[0]):
        j = i // 2
        if i % 2 == 0:
            h = _attention_layer(h, batch, seq, norm_mix[i], att_w_in[j], att_q_gain[j], att_k_gain[j], att_w_out[j])
        else:
            h = _mlstm_layer(h, batch, seq, norm_mix[i], ml_w_in[j], ml_b_gate[j], ml_h_gain[j], ml_w_out[j])
        h = _ffn(h, norm_ffn[i], ffn_w_up[i].astype(bf), ffn_w_down[i].astype(bf))
    return h.reshape(batch, seq, d)
```

```python
import functools

import jax
import jax.numpy as jnp
from jax import lax
from jax.experimental import pallas as pl
from jax.experimental.pallas import tpu as pltpu

D_MODEL = 2048
ATT_HEADS = 16
ATT_KV_HEADS = 4
ATT_HEAD_DIM = 128
IDX_HEADS = 16
IDX_HEAD_DIM = 64
TOPK_MAX = 256
ML_HEADS = 8
ML_V_DIM = 256
ML_QK_DIM = 128
GATE_SOFTCAP = 15.0
D_FF = 4 * D_MODEL
ROPE_THETA = 500000.0
ROT_FRAC = 4
EPS = 1e-6

LANES = 128
VMEM_LIMIT = 56 * 1024 * 1024
INT_MIN = -(2 ** 31)
NEG_BIG = -1e30

LOG2_E = 1.4426950408889634

ATT_Q_BLOCK = 256
ATT_KEY_CHUNK = 256
SEARCH_UNROLL = 4
ML_CHUNK = 128
ML_HEADS_PER_STEP = 4
PROJ_SUB = 512
FFN_SUB = 256


def _params(*sem):
    return pltpu.CompilerParams(dimension_semantics=sem, vmem_limit_bytes=VMEM_LIMIT)


def _store_rmsnorm(x_ref, g_ref, hn_ref):
    x = x_ref[...]
    ms = jnp.mean(x * x, axis=-1, keepdims=True)
    hn_ref[...] = (x * lax.rsqrt(ms + EPS) * g_ref[...]).astype(hn_ref.dtype)

def _rope_tables(seq, head_dim, heads_per_vreg):
    rot = head_dim // ROT_FRAC
    half = rot // 2
    inv_freq = ROPE_THETA ** (-2.0 * jnp.arange(half, dtype=jnp.float32) / rot)
    ang = jnp.arange(seq).astype(jnp.float32)[:, None] * inv_freq[None, :]
    cos, sin = jnp.cos(ang), jnp.sin(ang)
    ones = jnp.ones((seq, head_dim - rot), jnp.float32)
    zeros_h = jnp.zeros((seq, half), jnp.float32)
    zeros_r = jnp.zeros((seq, head_dim - rot), jnp.float32)
    c = jnp.concatenate([cos, cos, ones], axis=1)
    s1 = jnp.concatenate([-sin, zeros_h, zeros_r], axis=1)
    s2 = jnp.concatenate([zeros_h, sin, zeros_r], axis=1)
    rep = lambda t: jnp.tile(t, (1, heads_per_vreg))
    return rep(c), rep(s1), rep(s2), half


def _rope(y, c, s1, s2, half):
    return y * c + pltpu.roll(y, LANES - half, 1) * s1 + pltpu.roll(y, half, 1) * s2


def _att_proj_kernel(x_ref, g_ref, w_ref, gain_ref, tab_ref, tabi_ref, tabt_ref,
                     qk_ref, v_ref, qi_ref, kw_ref, hn_ref, *, half, half_i):
    _store_rmsnorm(x_ref, g_ref, hn_ref)
    n_qk, n_v, n_qi = qk_ref.shape[1], v_ref.shape[1], qi_ref.shape[1]
    n_q = ATT_HEADS * ATT_HEAD_DIM

    def cols(start, width):
        return jnp.dot(hn_ref[...], w_ref[:, start:start + width], preferred_element_type=jnp.float32)

    for s in range(n_qk // PROJ_SUB):
        acc = cols(s * PROJ_SUB, PROJ_SUB)
        gain = gain_ref[0 if s * PROJ_SUB < n_q else 1]
        for grp in range(PROJ_SUB // LANES):
            xg = acc[:, grp * LANES:(grp + 1) * LANES]
            ms = jnp.mean(xg * xg, axis=-1, keepdims=True)
            y = _rope(xg * lax.rsqrt(ms + EPS) * gain, tab_ref[0], tab_ref[1], tab_ref[2], half)
            qk_ref[:, s * PROJ_SUB + grp * LANES:s * PROJ_SUB + (grp + 1) * LANES] = y.astype(qk_ref.dtype)

    v_ref[...] = cols(n_qk, n_v).astype(v_ref.dtype)

    for s in range(n_qi // PROJ_SUB):
        acc = cols(n_qk + n_v + s * PROJ_SUB, PROJ_SUB)
        for grp in range(PROJ_SUB // LANES):
            y = _rope(acc[:, grp * LANES:(grp + 1) * LANES], tabi_ref[0], tabi_ref[1], tabi_ref[2], half_i)
            qi_ref[:, s * PROJ_SUB + grp * LANES:s * PROJ_SUB + (grp + 1) * LANES] = y.astype(qi_ref.dtype)

    kw_ref[...] = _rope(cols(n_qk + n_v + n_qi, LANES), tabt_ref[0], tabt_ref[1], tabt_ref[2], half_i)


def _ml_proj_kernel(x_ref, g_ref, w_ref, wg_ref, b_ref, qk_ref, v_ref, gates_ref, hn_ref):
    _store_rmsnorm(x_ref, g_ref, hn_ref)
    n_qk = qk_ref.shape[1]
    for s in range((n_qk + v_ref.shape[1]) // PROJ_SUB):
        acc = jnp.dot(hn_ref[...], w_ref[:, s * PROJ_SUB:(s + 1) * PROJ_SUB], preferred_element_type=jnp.float32)
        start = s * PROJ_SUB
        if start < n_qk // 2:
            qk_ref[:, start:start + PROJ_SUB] = acc.astype(qk_ref.dtype)
        elif start < n_qk:
            qk_ref[:, start:start + PROJ_SUB] = (acc * ML_QK_DIM ** -0.5).astype(qk_ref.dtype)
        else:
            v_ref[:, start - n_qk:start - n_qk + PROJ_SUB] = acc.astype(v_ref.dtype)

    g = lax.dot_general(wg_ref[...], hn_ref[...], (((1,), (1,)), ((), ())),
                        preferred_element_type=jnp.float32)
    g = g + b_ref[...]
    g = GATE_SOFTCAP * jnp.tanh(g / GATE_SOFTCAP)
    logf = jnp.minimum(g, 0.0) - jnp.log1p(jnp.exp(-jnp.abs(g)))
    is_forget = lax.broadcasted_iota(jnp.int32, g.shape, 0) >= ML_HEADS
    gates_ref[...] = jnp.where(is_forget, logf, g)


def _norm_proj_kernel(x_ref, g_ref, w_ref, o_ref, hn_ref):
    _store_rmsnorm(x_ref, g_ref, hn_ref)
    for s in range(o_ref.shape[1] // PROJ_SUB):
        cols = slice(s * PROJ_SUB, (s + 1) * PROJ_SUB)
        o_ref[:, cols] = jnp.dot(hn_ref[...], w_ref[:, cols], preferred_element_type=jnp.float32).astype(o_ref.dtype)


def _key_to_f32(key):
    return pltpu.bitcast(jnp.where(key < 0, key ^ jnp.int32(0x7FFFFFFF), key), jnp.float32)


def _select_topk(score_ref, bias_ref, *, topk, n_chunks):
    _, tq, ck = score_ref.shape
    kf = jnp.float32(topk)

    def count(pred, thr):
        part = jnp.zeros((tq, LANES), jnp.float32)
        for c in range(n_chunks):
            for j in range(ck // LANES):
                part = part + jnp.where(pred(score_ref[c, :, j * LANES:(j + 1) * LANES], thr), 1.0, 0.0)
        return jnp.sum(part, axis=1, keepdims=True)

    key_ninf = INT_MIN + 0x7FFFFF
    key_pinf = 0x7F800000

    def search(it, tau):
        cand = tau + lax.shift_left(jnp.int32(1), jnp.int32(31) - it)
        in_range = jnp.logical_and(cand > tau, cand <= key_pinf)
        cnt = count(jnp.greater_equal, _key_to_f32(cand))
        return jnp.where(jnp.logical_and(in_range, cnt >= kf), cand, tau)

    tau = lax.fori_loop(0, 32, search, jnp.full((tq, 1), key_ninf, jnp.int32), unroll=SEARCH_UNROLL)
    thr = _key_to_f32(tau)
    finite = lambda s, th: jnp.logical_and(s >= th, s > -jnp.inf)
    n_ge = count(finite, thr)

    @pl.when(jnp.max(n_ge) <= kf)
    def _():
        for c in range(n_chunks):
            bias_ref[c] = jnp.where(finite(score_ref[c], thr), 0.0, NEG_BIG)

    @pl.when(jnp.max(n_ge) > kf)
    def _():
        need = kf - count(jnp.greater, thr)
        r = lax.broadcasted_iota(jnp.int32, (ck, ck), 0)
        col = lax.broadcasted_iota(jnp.int32, (ck, ck), 1)
        before = jnp.where(r < col, 1.0, 0.0).astype(jnp.bfloat16)
        seen = jnp.zeros((tq, 1), jnp.float32)
        for c in range(n_chunks):
            s = score_ref[c]
            eq = jnp.logical_and(s == thr, s > -jnp.inf)
            e = jnp.where(eq, 1.0, 0.0)
            prefix = jnp.dot(e.astype(jnp.bfloat16), before, preferred_element_type=jnp.float32) + seen
            seen = seen + jnp.sum(e, axis=1, keepdims=True)
            keep = jnp.logical_or(s > thr, jnp.logical_and(eq, prefix < need))
            bias_ref[c] = jnp.where(keep, 0.0, NEG_BIG)


def _for_each_chunk(n, body):
    def pair(p, carry):
        body(2 * p)
        body(2 * p + 1)
        return carry

    lax.fori_loop(0, n // 2, pair, 0)
    pl.when(n % 2 == 1)(lambda: body(n - 1))


def _attention_kernel(q_ref, k_ref, v_ref, qi_ref, wq_ref, kk_ref, o_ref,
                      qs_ref, wb_ref, score_ref, bias_ref, q4_ref, s_ref, m_ref, l_ref, acc_ref, *, topk):
    tq = q_ref.shape[0]
    ck = ATT_KEY_CHUNK
    half = tq // 2
    nt = (((1,), (1,)), ((), ()))
    i = pl.program_id(1)
    q0 = i * tq
    n_chunks = (q0 + tq) // ck

    for r in range(2):
        rows = slice(r * half, (r + 1) * half)
        for h in range(IDX_HEADS):
            qs_ref[r, h * half:(h + 1) * half, :] = (
                qi_ref[rows, h * IDX_HEAD_DIM:(h + 1) * IDX_HEAD_DIM].astype(jnp.bfloat16))
            wb_ref[r * IDX_HEADS + h] = jnp.broadcast_to(
                wq_ref[rows, IDX_HEAD_DIM + h:IDX_HEAD_DIM + h + 1], (half, LANES))

    def score_chunk(c):
        off = pl.multiple_of(c * ck, ck)
        ki = kk_ref[pl.ds(off, ck), :].astype(jnp.bfloat16)[:, :IDX_HEAD_DIM]
        spos = off + lax.broadcasted_iota(jnp.int32, (half, LANES), 1)
        for r in range(2):
            d = lax.dot_general(qs_ref[r], ki, nt, preferred_element_type=jnp.float32)
            tpos = q0 + r * half + lax.broadcasted_iota(jnp.int32, (half, LANES), 0)
            for j in range(ck // LANES):
                acc = jnp.zeros((half, LANES), jnp.float32)
                for h in range(IDX_HEADS):
                    dh = d[h * half:(h + 1) * half, j * LANES:(j + 1) * LANES]
                    acc = acc + jnp.maximum(dh, 0.0) * wb_ref[r * IDX_HEADS + h]
                score_ref[c, r * half:(r + 1) * half, j * LANES:(j + 1) * LANES] = (
                    jnp.where(spos + j * LANES <= tpos, acc, -jnp.inf))

    _for_each_chunk(n_chunks, score_chunk)

    for n in range(1, score_ref.shape[0] + 1):
        pl.when(n_chunks == n)(functools.partial(_select_topk, score_ref, bias_ref, topk=topk, n_chunks=n))

    exp2_scale = ATT_HEAD_DIM ** -0.5 * LOG2_E
    rep = ATT_HEADS // ATT_KV_HEADS
    pair = s_ref.shape[0]
    head_cols = lambda h: slice(h * ATT_HEAD_DIM, (h + 1) * ATT_HEAD_DIM)
    for g0 in range(0, ATT_KV_HEADS, pair):
        for u in range(pair):
            for r in range(rep):
                q4_ref[u, r * tq:(r + 1) * tq, :] = q_ref[:, head_cols((g0 + u) * rep + r)]
        m_ref[...] = jnp.full(m_ref.shape, NEG_BIG, jnp.float32)

        def logits_chunk(c):
            off = pl.multiple_of(c * ck, ck)
            for u in range(pair):
                s = lax.dot_general(q4_ref[u], k_ref[pl.ds(off, ck), head_cols(g0 + u)], nt,
                                    preferred_element_type=jnp.float32)
                s = ((s.reshape(rep, tq, ck) + bias_ref[c][None]) * exp2_scale).reshape(rep * tq, ck)
                s_ref[u, c] = s
                m = m_ref[u]
                for j in range(ck // LANES):
                    m = jnp.maximum(m, s[:, j * LANES:(j + 1) * LANES])
                m_ref[u] = m

        _for_each_chunk(n_chunks, logits_chunk)
        for u in range(pair):
            m_ref[u] = jnp.broadcast_to(jnp.max(m_ref[u], axis=1, keepdims=True), m_ref.shape[1:])
        l_ref[...] = jnp.zeros(l_ref.shape, jnp.float32)
        acc_ref[...] = jnp.zeros(acc_ref.shape, jnp.float32)

        def pv_chunk(c):
            off = pl.multiple_of(c * ck, ck)
            for u in range(pair):
                p = jnp.exp2(s_ref[u, c] - jnp.concatenate([m_ref[u]] * (ck // LANES), axis=1))
                l = l_ref[u]
                for j in range(ck // LANES):
                    l = l + p[:, j * LANES:(j + 1) * LANES]
                l_ref[u] = l
                acc_ref[u] += jnp.dot(p.astype(jnp.bfloat16), v_ref[pl.ds(off, ck), head_cols(g0 + u)],
                                      preferred_element_type=jnp.float32)

        _for_each_chunk(n_chunks, pv_chunk)
        for u in range(pair):
            o4 = acc_ref[u] / jnp.sum(l_ref[u], axis=1, keepdims=True)
            for r in range(rep):
                o_ref[:, head_cols((g0 + u) * rep + r)] = o4[r * tq:(r + 1) * tq].astype(o_ref.dtype)


def _attention(qk, v, qi, kw, batch, seq):
    n = qk.shape[0]
    tq, ck = ATT_Q_BLOCK, ATT_KEY_CHUNK
    nq = seq // tq
    rep = ATT_HEADS // ATT_KV_HEADS
    pair = 2
    topk = min(TOPK_MAX, seq // 4)
    q_w, kv_w = ATT_HEADS * ATT_HEAD_DIM, ATT_KV_HEADS * ATT_HEAD_DIM
    qi_w = IDX_HEADS * IDX_HEAD_DIM
    return pl.pallas_call(
        functools.partial(_attention_kernel, topk=topk),
        grid=(batch, nq),
        in_specs=[pl.BlockSpec((tq, q_w), lambda b, i: (b * nq + i, 0)),
                  pl.BlockSpec((seq, kv_w), lambda b, i: (b, q_w // kv_w)),
                  pl.BlockSpec((seq, kv_w), lambda b, i: (b, 0)),
                  pl.BlockSpec((tq, qi_w), lambda b, i: (b * nq + i, 0)),
                  pl.BlockSpec((tq, LANES), lambda b, i: (b * nq + i, 0)),
                  pl.BlockSpec((seq, LANES), lambda b, i: (b, 0))],
        out_specs=pl.BlockSpec((tq, q_w), lambda b, i: (b * nq + i, 0)),
        out_shape=jax.ShapeDtypeStruct((n, q_w), jnp.bfloat16),
        scratch_shapes=[pltpu.VMEM((2, IDX_HEADS * tq // 2, IDX_HEAD_DIM), jnp.bfloat16),
                        pltpu.VMEM((2 * IDX_HEADS, tq // 2, LANES), jnp.float32),
                        pltpu.VMEM((seq // ck, tq, ck), jnp.float32),
                        pltpu.VMEM((seq // ck, tq, ck), jnp.float32),
                        pltpu.VMEM((pair, rep * tq, ATT_HEAD_DIM), jnp.bfloat16),
                        pltpu.VMEM((pair, seq // ck, rep * tq, ck), jnp.float32),
                        pltpu.VMEM((pair, rep * tq, LANES), jnp.float32),
                        pltpu.VMEM((pair, rep * tq, LANES), jnp.float32),
                        pltpu.VMEM((pair, rep * tq, ATT_HEAD_DIM), jnp.float32)],
        compiler_params=_params("parallel", "arbitrary"),
        name="dsa_attention",
    )(qk, qk, v, qi, kw, kw)


def _mlstm_kernel(q_ref, k_ref, v_ref, og_ref, ig_ref, lf_ref, hg_ref, o_ref):
    L = ML_CHUNK
    nc = q_ref.shape[0] // L
    dk, dv = ML_QK_DIM, ML_V_DIM
    heads = q_ref.shape[1] // dk
    row = lax.broadcasted_iota(jnp.int32, (L, L), 0)
    col = lax.broadcasted_iota(jnp.int32, (L, L), 1)
    tri = col <= row
    eye = col == row
    gain = hg_ref[...]

    def to_col(x_row):
        return jnp.sum(jnp.where(eye, jnp.broadcast_to(x_row, (L, L)), 0.0), axis=1, keepdims=True)

    def chunk(c, carry):
        return tuple(head_chunk(hd, c, carry[hd]) for hd in range(heads))

    def head_chunk(hd, c, state):
        C, n_row, m = state
        off = pl.multiple_of(c * L, L)
        qc = q_ref[pl.ds(off, L), hd * dk:(hd + 1) * dk]
        kc = k_ref[pl.ds(off, L), hd * dk:(hd + 1) * dk]
        vc = v_ref[pl.ds(off, L), hd * dv:(hd + 1) * dv]
        ig_r = ig_ref[hd, c]
        lf_r = lf_ref[hd, c]

        lf_b = jnp.broadcast_to(lf_r, (L, L))
        b_col = jnp.sum(jnp.where(tri, lf_b, 0.0), axis=1, keepdims=True)
        lf_col = to_col(lf_r)
        b_row = jnp.sum(jnp.where(row <= col, jnp.broadcast_to(lf_col, (L, L)), 0.0), axis=0, keepdims=True)
        a = jnp.sum(lf_r, axis=1, keepdims=True)

        g_row = a - b_row + ig_r
        m_loc = jnp.max(g_row, axis=1, keepdims=True)
        w_row = jnp.exp(g_row - m_loc)
        w_col = to_col(w_row)
        c_loc = lax.dot_general(kc, (w_col * vc.astype(jnp.float32)).astype(jnp.bfloat16),
                                (((0,), (0,)), ((), ())), preferred_element_type=jnp.float32)
        n_loc = jnp.dot(jnp.broadcast_to(w_row, (8, L)).astype(jnp.bfloat16), kc,
                        preferred_element_type=jnp.float32)[0:1]

        dm = jnp.where(tri, b_col - b_row + ig_r, -jnp.inf)
        inter = b_col + m
        m_t = jnp.maximum(inter, jnp.max(dm, axis=1, keepdims=True))
        s = lax.dot_general(qc, kc, (((1,), (1,)), ((), ())), preferred_element_type=jnp.float32)
        s = s * jnp.exp(dm - m_t)
        s_inter = jnp.exp(inter - m_t)
        qf = qc.astype(jnp.float32)
        num = (jnp.dot(s.astype(jnp.bfloat16), vc, preferred_element_type=jnp.float32)
               + s_inter * jnp.dot(qc, C.astype(jnp.bfloat16), preferred_element_type=jnp.float32))
        den = jnp.sum(s, axis=1, keepdims=True) + s_inter * jnp.sum(qf * n_row, axis=1, keepdims=True)
        h = num / jnp.maximum(jnp.abs(den), jnp.exp(-m_t))

        ms = jnp.mean(h * h, axis=-1, keepdims=True)
        hn = h * lax.rsqrt(ms + EPS) * gain
        og = og_ref[pl.ds(off, L), hd * dv:(hd + 1) * dv]
        o_ref[pl.ds(off, L), hd * dv:(hd + 1) * dv] = (jax.nn.sigmoid(og) * hn).astype(o_ref.dtype)

        m_new = jnp.maximum(a + m, m_loc)
        s_old = jnp.exp(a + m - m_new)
        s_new = jnp.exp(m_loc - m_new)
        return (s_old * C + s_new * c_loc, s_old * n_row + s_new * n_loc, m_new)

    init = (jnp.zeros((dk, dv), jnp.float32), jnp.zeros((1, dk), jnp.float32), jnp.zeros((1, 1), jnp.float32))
    lax.fori_loop(0, nc, chunk, (init,) * heads)


def _mlstm(qk, v, og, gates_t, h_gain, batch, seq):
    n = qk.shape[0]
    L = ML_CHUNK
    nc = seq // L
    hb = ML_HEADS_PER_STEP
    nhb = ML_HEADS // hb
    g4 = gates_t.reshape(2 * ML_HEADS, batch * nc, 1, L)
    return pl.pallas_call(
        _mlstm_kernel,
        grid=(batch, nhb),
        in_specs=[pl.BlockSpec((seq, hb * ML_QK_DIM), lambda b, h: (b, h)),
                  pl.BlockSpec((seq, hb * ML_QK_DIM), lambda b, h: (b, nhb + h)),
                  pl.BlockSpec((seq, hb * ML_V_DIM), lambda b, h: (b, h)),
                  pl.BlockSpec((seq, hb * ML_V_DIM), lambda b, h: (b, h)),
                  pl.BlockSpec((hb, nc, 1, L), lambda b, h: (h, b, 0, 0)),
                  pl.BlockSpec((hb, nc, 1, L), lambda b, h: (nhb + h, b, 0, 0)),
                  pl.BlockSpec((1, ML_V_DIM), lambda b, h: (0, 0))],
        out_specs=pl.BlockSpec((seq, hb * ML_V_DIM), lambda b, h: (b, h)),
        out_shape=jax.ShapeDtypeStruct((n, ML_HEADS * ML_V_DIM), jnp.bfloat16),
        compiler_params=_params("parallel", "parallel"),
        name="mlstm",
    )(qk, qk, v, og, g4, g4, h_gain.reshape(1, ML_V_DIM))


def _outproj_kernel(a_ref, w_ref, x_ref, o_ref):
    o_ref[...] = x_ref[...] + jnp.dot(a_ref[...], w_ref[...], preferred_element_type=jnp.float32)


def _outproj(a, w, x, tm=512):
    n, d = x.shape
    k = a.shape[1]
    return pl.pallas_call(
        _outproj_kernel,
        grid=(n // tm,),
        in_specs=[pl.BlockSpec((tm, k), lambda i: (i, 0)),
                  pl.BlockSpec((k, d), lambda i: (0, 0)),
                  pl.BlockSpec((tm, d), lambda i: (i, 0))],
        out_specs=pl.BlockSpec((tm, d), lambda i: (i, 0)),
        out_shape=jax.ShapeDtypeStruct((n, d), jnp.float32),
        compiler_params=_params("parallel"),
        name="outproj",
    )(a, w, x)


def _ffn_kernel(x_ref, g_ref, wu_ref, wd_ref, o_ref, hn_ref):
    @pl.when(pl.program_id(1) == 0)
    def _():
        x = x_ref[...]
        ms = jnp.mean(x * x, axis=-1, keepdims=True)
        hn_ref[...] = (x * lax.rsqrt(ms + EPS) * g_ref[...]).astype(hn_ref.dtype)
        o_ref[...] = x

    acts = []
    for s in range(wu_ref.shape[1] // FFN_SUB):
        u = jnp.dot(hn_ref[...], wu_ref[:, s * FFN_SUB:(s + 1) * FFN_SUB], preferred_element_type=jnp.float32)
        acts.append(jnp.square(jnp.maximum(u, 0.0)).astype(jnp.bfloat16))
    o_ref[...] += jnp.dot(jnp.concatenate(acts, axis=1), wd_ref[...], preferred_element_type=jnp.float32)


def _ffn(x, g, w_up, w_down, layer, tm=1024, tf=512):
    n, d = x.shape
    tm = min(tm, n)
    f = w_up.shape[2]
    return pl.pallas_call(
        _ffn_kernel,
        grid=(n // tm, f // tf),
        in_specs=[pl.BlockSpec((tm, d), lambda i, j: (i, 0)),
                  pl.BlockSpec((1, d), lambda i, j: (0, 0)),
                  pl.BlockSpec((None, d, tf), lambda i, j: (layer, 0, j)),
                  pl.BlockSpec((None, tf, d), lambda i, j: (layer, j, 0))],
        out_specs=pl.BlockSpec((tm, d), lambda i, j: (i, 0)),
        out_shape=jax.ShapeDtypeStruct((n, d), jnp.float32),
        scratch_shapes=[pltpu.VMEM((tm, d), jnp.bfloat16)],
        compiler_params=_params("parallel", "arbitrary"),
        name="ffn",
    )(x, g.reshape(1, d), w_up, w_down)


def _attention_layer(h, batch, seq, g_mix, w_in, q_gain, k_gain, w_out):
    bf = jnp.bfloat16
    n, d = h.shape
    n_qk = (ATT_HEADS + ATT_KV_HEADS) * ATT_HEAD_DIM
    n_v = ATT_KV_HEADS * ATT_HEAD_DIM
    n_qi = IDX_HEADS * IDX_HEAD_DIM
    tm = min(512, seq)

    c, s1, s2, half = _rope_tables(seq, ATT_HEAD_DIM, 1)
    ci, s1i, s2i, half_i = _rope_tables(seq, IDX_HEAD_DIM, LANES // IDX_HEAD_DIM)
    lane = jnp.arange(LANES)
    is_k = (lane < IDX_HEAD_DIM)[None, :]
    is_w = jnp.logical_and(lane >= IDX_HEAD_DIM, lane < IDX_HEAD_DIM + IDX_HEADS)[None, :]
    w_scale = IDX_HEADS ** -0.5 * IDX_HEAD_DIM ** -0.5
    tab = jnp.stack([c, s1, s2])
    tab_i = jnp.stack([ci, s1i, s2i])
    tab_t = jnp.stack([jnp.where(is_k, ci, jnp.where(is_w, w_scale, 0.0)),
                       jnp.where(is_k, s1i, 0.0), jnp.where(is_k, s2i, 0.0)])
    gains = jnp.stack([q_gain, k_gain]).reshape(2, 1, ATT_HEAD_DIM)
    w = jnp.pad(w_in, ((0, 0), (0, (-w_in.shape[1]) % LANES))).astype(bf)

    row = lambda width: pl.BlockSpec((tm, width), lambda i: (i, 0))
    whole = lambda a: pl.BlockSpec(a.shape, lambda i: (0,) * a.ndim, pipeline_mode=pl.Buffered(1))
    tbl = pl.BlockSpec((3, tm, LANES), lambda i: (0, i % (seq // tm), 0))
    qk, v, qi, kw = pl.pallas_call(
        functools.partial(_att_proj_kernel, half=half, half_i=half_i),
        grid=(n // tm,),
        in_specs=[row(d), whole(g_mix.reshape(1, d)), whole(w), whole(gains), tbl, tbl, tbl],
        out_specs=[row(n_qk), row(n_v), row(n_qi), row(LANES)],
        out_shape=[jax.ShapeDtypeStruct((n, n_qk), bf), jax.ShapeDtypeStruct((n, n_v), bf),
                   jax.ShapeDtypeStruct((n, n_qi), bf), jax.ShapeDtypeStruct((n, LANES), jnp.float32)],
        scratch_shapes=[pltpu.VMEM((tm, d), bf)],
        compiler_params=_params("parallel"),
        name="att_proj",
    )(h, g_mix.reshape(1, d), w, gains, tab, tab_i, tab_t)

    o = _attention(qk, v, qi, kw, batch, seq)
    return _outproj(o, w_out.astype(bf), h)


def _mlstm_layer(h, batch, seq, g_mix, w_in, b_gate, h_gain, w_out):
    bf = jnp.bfloat16
    n, d = h.shape
    n_qk, n_v = 2 * ML_HEADS * ML_QK_DIM, ML_HEADS * ML_V_DIM
    n_g = 2 * ML_HEADS
    tm = min(512, seq)
    g2 = g_mix.reshape(1, d)
    w = w_in.astype(bf)
    wg_t = w_in[:, n_qk + n_v + d:].T.astype(bf)
    assert (n_qk + n_v) % d == 0

    row = lambda width: pl.BlockSpec((tm, width), lambda i: (i, 0))
    whole = lambda a: pl.BlockSpec(a.shape, lambda i: (0,) * a.ndim, pipeline_mode=pl.Buffered(1))
    wcols = lambda width, blk: pl.BlockSpec((d, width), lambda i: (0, blk), pipeline_mode=pl.Buffered(1))
    qk, v, gates_t = pl.pallas_call(
        _ml_proj_kernel,
        grid=(n // tm,),
        in_specs=[row(d), whole(g2), wcols(n_qk + n_v, 0), whole(wg_t), pl.BlockSpec((n_g, 1), lambda i: (0, 0))],
        out_specs=[row(n_qk), row(n_v), pl.BlockSpec((n_g, tm), lambda i: (0, i))],
        out_shape=[jax.ShapeDtypeStruct((n, n_qk), bf), jax.ShapeDtypeStruct((n, n_v), bf),
                   jax.ShapeDtypeStruct((n_g, n), jnp.float32)],
        scratch_shapes=[pltpu.VMEM((tm, d), bf)],
        compiler_params=_params("parallel"),
        name="ml_proj",
    )(h, g2, w, wg_t, b_gate.reshape(n_g, 1))
    og = pl.pallas_call(
        _norm_proj_kernel,
        grid=(n // tm,),
        in_specs=[row(d), whole(g2), wcols(d, (n_qk + n_v) // d)],
        out_specs=row(d),
        out_shape=jax.ShapeDtypeStruct((n, d), jnp.float32),
        scratch_shapes=[pltpu.VMEM((tm, d), bf)],
        compiler_params=_params("parallel"),
        name="ml_o_proj",
    )(h, g2, w)
    y = _mlstm(qk, v, og, gates_t, h_gain, batch, seq)
    return _outproj(y, w_out.astype(bf), h)


def kernel(x, norm_mix, norm_ffn, att_w_in, att_q_gain, att_k_gain, att_w_out, ml_w_in, ml_b_gate, ml_h_gain,
           ml_w_out, ffn_w_up, ffn_w_down):
    batch, seq, d = x.shape
    bf = jnp.bfloat16
    h = x.reshape(batch * seq, d)
    w_up, w_down = ffn_w_up.astype(bf), ffn_w_down.astype(bf)
    for i in range(norm_mix.shape[0]):
        j = i // 2
        if i % 2 == 0:
            h = _attention_layer(h, batch, seq, norm_mix[i], att_w_in[j], att_q_gain[j], att_k_gain[j], att_w_out[j])
        else:
            h = _mlstm_layer(h, batch, seq, norm_mix[i], ml_w_in[j], ml_b_gate[j], ml_h_gain[j], ml_w_out[j])
        h = _ffn(h, norm_ffn[i], w_up, w_down, i)
    return h.reshape(batch, seq, d)
```

```python
import functools

import jax
import jax.numpy as jnp
from jax import lax
from jax.experimental import pallas as pl
from jax.experimental.pallas import tpu as pltpu

D_MODEL = 2048
ATT_HEADS = 16
ATT_KV_HEADS = 4
ATT_HEAD_DIM = 128
IDX_HEADS = 16
IDX_HEAD_DIM = 64
TOPK_MAX = 256
ML_HEADS = 8
ML_V_DIM = 256
ML_QK_DIM = 128
GATE_SOFTCAP = 15.0
D_FF = 4 * D_MODEL
ROPE_THETA = 500000.0
ROT_FRAC = 4
EPS = 1e-6

LANES = 128
VMEM_LIMIT = 56 * 1024 * 1024
INT_MIN = -(2 ** 31)
NEG_BIG = -1e30

LOG2_E = 1.4426950408889634

ATT_Q_BLOCK = 256
ATT_KEY_CHUNK = 256
SEARCH_UNROLL = 4
ML_CHUNK = 128
ML_HEADS_PER_STEP = 4
PROJ_SUB = 512
FFN_SUB = 256


def _params(*sem):
    return pltpu.CompilerParams(dimension_semantics=sem, vmem_limit_bytes=VMEM_LIMIT)


def _store_rmsnorm(x_ref, g_ref, hn_ref):
    x = x_ref[...]
    ms = jnp.mean(x * x, axis=-1, keepdims=True)
    hn_ref[...] = (x * lax.rsqrt(ms + EPS) * g_ref[...]).astype(hn_ref.dtype)

def _rope_tables(seq, head_dim, heads_per_vreg):
    rot = head_dim // ROT_FRAC
    half = rot // 2
    inv_freq = ROPE_THETA ** (-2.0 * jnp.arange(half, dtype=jnp.float32) / rot)
    ang = jnp.arange(seq).astype(jnp.float32)[:, None] * inv_freq[None, :]
    cos, sin = jnp.cos(ang), jnp.sin(ang)
    ones = jnp.ones((seq, head_dim - rot), jnp.float32)
    zeros_h = jnp.zeros((seq, half), jnp.float32)
    zeros_r = jnp.zeros((seq, head_dim - rot), jnp.float32)
    c = jnp.concatenate([cos, cos, ones], axis=1)
    s1 = jnp.concatenate([-sin, zeros_h, zeros_r], axis=1)
    s2 = jnp.concatenate([zeros_h, sin, zeros_r], axis=1)
    rep = lambda t: jnp.tile(t, (1, heads_per_vreg))
    return rep(c), rep(s1), rep(s2), half


def _rope(y, c, s1, s2, half):
    return y * c + pltpu.roll(y, LANES - half, 1) * s1 + pltpu.roll(y, half, 1) * s2


def _att_proj_kernel(x_ref, g_ref, w_ref, wt_ref, gain_ref, tab_ref, tabi_ref, tabt_ref,
                     qk_ref, v_ref, qi_ref, kw_ref, hn_ref, *, half, half_i):
    _store_rmsnorm(x_ref, g_ref, hn_ref)
    n_qk, n_v, n_qi = qk_ref.shape[1], v_ref.shape[1], qi_ref.shape[1]
    n_q = ATT_HEADS * ATT_HEAD_DIM

    def cols(start, width):
        return jnp.dot(hn_ref[...], w_ref[:, start:start + width], preferred_element_type=jnp.float32)

    for s in range(n_qk // PROJ_SUB):
        acc = cols(s * PROJ_SUB, PROJ_SUB)
        gain = gain_ref[0 if s * PROJ_SUB < n_q else 1]
        for grp in range(PROJ_SUB // LANES):
            xg = acc[:, grp * LANES:(grp + 1) * LANES]
            ms = jnp.mean(xg * xg, axis=-1, keepdims=True)
            y = _rope(xg * lax.rsqrt(ms + EPS) * gain, tab_ref[0], tab_ref[1], tab_ref[2], half)
            qk_ref[:, s * PROJ_SUB + grp * LANES:s * PROJ_SUB + (grp + 1) * LANES] = y.astype(qk_ref.dtype)

    v_ref[...] = cols(n_qk, n_v).astype(v_ref.dtype)

    for s in range(n_qi // PROJ_SUB):
        acc = cols(n_qk + n_v + s * PROJ_SUB, PROJ_SUB)
        for grp in range(PROJ_SUB // LANES):
            y = _rope(acc[:, grp * LANES:(grp + 1) * LANES], tabi_ref[0], tabi_ref[1], tabi_ref[2], half_i)
            qi_ref[:, s * PROJ_SUB + grp * LANES:s * PROJ_SUB + (grp + 1) * LANES] = y.astype(qi_ref.dtype)

    tail = jnp.dot(hn_ref[...], wt_ref[...], preferred_element_type=jnp.float32)
    kw_ref[...] = _rope(tail, tabt_ref[0], tabt_ref[1], tabt_ref[2], half_i)


def _ml_proj_kernel(x_ref, g_ref, w_ref, wg_ref, b_ref, qk_ref, v_ref, gates_ref, hn_ref):
    _store_rmsnorm(x_ref, g_ref, hn_ref)
    n_qk = qk_ref.shape[1]
    for s in range((n_qk + v_ref.shape[1]) // PROJ_SUB):
        acc = jnp.dot(hn_ref[...], w_ref[:, s * PROJ_SUB:(s + 1) * PROJ_SUB], preferred_element_type=jnp.float32)
        start = s * PROJ_SUB
        if start < n_qk // 2:
            qk_ref[:, start:start + PROJ_SUB] = acc.astype(qk_ref.dtype)
        elif start < n_qk:
            qk_ref[:, start:start + PROJ_SUB] = (acc * ML_QK_DIM ** -0.5).astype(qk_ref.dtype)
        else:
            v_ref[:, start - n_qk:start - n_qk + PROJ_SUB] = acc.astype(v_ref.dtype)

    g = lax.dot_general(wg_ref[...], hn_ref[...], (((1,), (1,)), ((), ())),
                        preferred_element_type=jnp.float32)
    g = g + b_ref[...]
    g = GATE_SOFTCAP * jnp.tanh(g / GATE_SOFTCAP)
    logf = jnp.minimum(g, 0.0) - jnp.log1p(jnp.exp(-jnp.abs(g)))
    is_forget = lax.broadcasted_iota(jnp.int32, g.shape, 0) >= ML_HEADS
    gates_ref[...] = jnp.where(is_forget, logf, g)


def _norm_proj_kernel(x_ref, g_ref, w_ref, o_ref, hn_ref):
    _store_rmsnorm(x_ref, g_ref, hn_ref)
    for s in range(o_ref.shape[1] // PROJ_SUB):
        cols = slice(s * PROJ_SUB, (s + 1) * PROJ_SUB)
        o_ref[:, cols] = jnp.dot(hn_ref[...], w_ref[:, cols], preferred_element_type=jnp.float32).astype(o_ref.dtype)


def _key_to_f32(key):
    return pltpu.bitcast(jnp.where(key < 0, key ^ jnp.int32(0x7FFFFFFF), key), jnp.float32)


def _select_topk(score_ref, bias_ref, *, topk, n_chunks):
    _, tq, ck = score_ref.shape
    kf = jnp.float32(topk)

    def count(pred, thr):
        part = jnp.zeros((tq, LANES), jnp.float32)
        for c in range(n_chunks):
            for j in range(ck // LANES):
                part = part + jnp.where(pred(score_ref[c, :, j * LANES:(j + 1) * LANES], thr), 1.0, 0.0)
        return jnp.sum(part, axis=1, keepdims=True)

    key_ninf = INT_MIN + 0x7FFFFF
    key_pinf = 0x7F800000

    def search(it, tau):
        cand = tau + lax.shift_left(jnp.int32(1), jnp.int32(31) - it)
        in_range = jnp.logical_and(cand > tau, cand <= key_pinf)
        cnt = count(jnp.greater_equal, _key_to_f32(cand))
        return jnp.where(jnp.logical_and(in_range, cnt >= kf), cand, tau)

    tau = lax.fori_loop(0, 32, search, jnp.full((tq, 1), key_ninf, jnp.int32), unroll=SEARCH_UNROLL)
    thr = _key_to_f32(tau)
    finite = lambda s, th: jnp.logical_and(s >= th, s > -jnp.inf)
    n_ge = count(finite, thr)

    @pl.when(jnp.max(n_ge) <= kf)
    def _():
        for c in range(n_chunks):
            bias_ref[c] = jnp.where(finite(score_ref[c], thr), 0.0, NEG_BIG)

    @pl.when(jnp.max(n_ge) > kf)
    def _():
        need = kf - count(jnp.greater, thr)
        r = lax.broadcasted_iota(jnp.int32, (ck, ck), 0)
        col = lax.broadcasted_iota(jnp.int32, (ck, ck), 1)
        before = jnp.where(r < col, 1.0, 0.0).astype(jnp.bfloat16)
        seen = jnp.zeros((tq, 1), jnp.float32)
        for c in range(n_chunks):
            s = score_ref[c]
            eq = jnp.logical_and(s == thr, s > -jnp.inf)
            e = jnp.where(eq, 1.0, 0.0)
            prefix = jnp.dot(e.astype(jnp.bfloat16), before, preferred_element_type=jnp.float32) + seen
            seen = seen + jnp.sum(e, axis=1, keepdims=True)
            keep = jnp.logical_or(s > thr, jnp.logical_and(eq, prefix < need))
            bias_ref[c] = jnp.where(keep, 0.0, NEG_BIG)


def _for_each_chunk(n, body):
    def pair(p, carry):
        body(2 * p)
        body(2 * p + 1)
        return carry

    lax.fori_loop(0, n // 2, pair, 0)
    pl.when(n % 2 == 1)(lambda: body(n - 1))


def _attention_kernel(q_ref, k_ref, v_ref, qi_ref, wq_ref, kk_ref, o_ref,
                      qs_ref, wb_ref, score_ref, bias_ref, q4_ref, s_ref, m_ref, l_ref, acc_ref, *, topk):
    tq = q_ref.shape[0]
    ck = ATT_KEY_CHUNK
    half = tq // 2
    nt = (((1,), (1,)), ((), ()))
    i = pl.program_id(1)
    q0 = i * tq
    n_chunks = (q0 + tq) // ck

    for r in range(2):
        rows = slice(r * half, (r + 1) * half)
        for h in range(IDX_HEADS):
            qs_ref[r, h * half:(h + 1) * half, :] = (
                qi_ref[rows, h * IDX_HEAD_DIM:(h + 1) * IDX_HEAD_DIM].astype(jnp.bfloat16))
            wb_ref[r * IDX_HEADS + h] = jnp.broadcast_to(
                wq_ref[rows, IDX_HEAD_DIM + h:IDX_HEAD_DIM + h + 1], (half, LANES))

    def score_chunk(c):
        off = pl.multiple_of(c * ck, ck)
        ki = kk_ref[pl.ds(off, ck), :].astype(jnp.bfloat16)[:, :IDX_HEAD_DIM]
        spos = off + lax.broadcasted_iota(jnp.int32, (half, LANES), 1)
        for r in range(2):
            d = lax.dot_general(qs_ref[r], ki, nt, preferred_element_type=jnp.float32)
            tpos = q0 + r * half + lax.broadcasted_iota(jnp.int32, (half, LANES), 0)
            for j in range(ck // LANES):
                acc = jnp.zeros((half, LANES), jnp.float32)
                for h in range(IDX_HEADS):
                    dh = d[h * half:(h + 1) * half, j * LANES:(j + 1) * LANES]
                    acc = acc + jnp.maximum(dh, 0.0) * wb_ref[r * IDX_HEADS + h]
                score_ref[c, r * half:(r + 1) * half, j * LANES:(j + 1) * LANES] = (
                    jnp.where(spos + j * LANES <= tpos, acc, -jnp.inf))

    _for_each_chunk(n_chunks, score_chunk)

    for n in range(1, score_ref.shape[0] + 1):
        pl.when(n_chunks == n)(functools.partial(_select_topk, score_ref, bias_ref, topk=topk, n_chunks=n))

    exp2_scale = ATT_HEAD_DIM ** -0.5 * LOG2_E
    rep = ATT_HEADS // ATT_KV_HEADS
    pair = s_ref.shape[0]
    head_cols = lambda h: slice(h * ATT_HEAD_DIM, (h + 1) * ATT_HEAD_DIM)
    for g0 in range(0, ATT_KV_HEADS, pair):
        for u in range(pair):
            for r in range(rep):
                q4_ref[u, r * tq:(r + 1) * tq, :] = q_ref[:, head_cols((g0 + u) * rep + r)]
        m_ref[...] = jnp.full(m_ref.shape, NEG_BIG, jnp.float32)

        def logits_chunk(c):
            off = pl.multiple_of(c * ck, ck)
            for u in range(pair):
                s = lax.dot_general(q4_ref[u], k_ref[pl.ds(off, ck), head_cols(g0 + u)], nt,
                                    preferred_element_type=jnp.float32)
                s = ((s.reshape(rep, tq, ck) + bias_ref[c][None]) * exp2_scale).reshape(rep * tq, ck)
                s_ref[u, c] = s
                m = m_ref[u]
                for j in range(ck // LANES):
                    m = jnp.maximum(m, s[:, j * LANES:(j + 1) * LANES])
                m_ref[u] = m

        _for_each_chunk(n_chunks, logits_chunk)
        for u in range(pair):
            m_ref[u] = jnp.broadcast_to(jnp.max(m_ref[u], axis=1, keepdims=True), m_ref.shape[1:])
        l_ref[...] = jnp.zeros(l_ref.shape, jnp.float32)
        acc_ref[...] = jnp.zeros(acc_ref.shape, jnp.float32)

        def pv_chunk(c):
            off = pl.multiple_of(c * ck, ck)
            for u in range(pair):
                p = jnp.exp2(s_ref[u, c] - jnp.concatenate([m_ref[u]] * (ck // LANES), axis=1))
                l = l_ref[u]
                for j in range(ck // LANES):
                    l = l + p[:, j * LANES:(j + 1) * LANES]
                l_ref[u] = l
                acc_ref[u] += jnp.dot(p.astype(jnp.bfloat16), v_ref[pl.ds(off, ck), head_cols(g0 + u)],
                                      preferred_element_type=jnp.float32)

        _for_each_chunk(n_chunks, pv_chunk)
        for u in range(pair):
            o4 = acc_ref[u] / jnp.sum(l_ref[u], axis=1, keepdims=True)
            for r in range(rep):
                o_ref[:, head_cols((g0 + u) * rep + r)] = o4[r * tq:(r + 1) * tq].astype(o_ref.dtype)


def _attention(qk, v, qi, kw, batch, seq):
    n = qk.shape[0]
    tq, ck = ATT_Q_BLOCK, ATT_KEY_CHUNK
    nq = seq // tq
    rep = ATT_HEADS // ATT_KV_HEADS
    pair = 2
    topk = min(TOPK_MAX, seq // 4)
    q_w, kv_w = ATT_HEADS * ATT_HEAD_DIM, ATT_KV_HEADS * ATT_HEAD_DIM
    qi_w = IDX_HEADS * IDX_HEAD_DIM
    return pl.pallas_call(
        functools.partial(_attention_kernel, topk=topk),
        grid=(batch, nq),
        in_specs=[pl.BlockSpec((tq, q_w), lambda b, i: (b * nq + i, 0)),
                  pl.BlockSpec((seq, kv_w), lambda b, i: (b, q_w // kv_w)),
                  pl.BlockSpec((seq, kv_w), lambda b, i: (b, 0)),
                  pl.BlockSpec((tq, qi_w), lambda b, i: (b * nq + i, 0)),
                  pl.BlockSpec((tq, LANES), lambda b, i: (b * nq + i, 0)),
                  pl.BlockSpec((seq, LANES), lambda b, i: (b, 0))],
        out_specs=pl.BlockSpec((tq, q_w), lambda b, i: (b * nq + i, 0)),
        out_shape=jax.ShapeDtypeStruct((n, q_w), jnp.bfloat16),
        scratch_shapes=[pltpu.VMEM((2, IDX_HEADS * tq // 2, IDX_HEAD_DIM), jnp.bfloat16),
                        pltpu.VMEM((2 * IDX_HEADS, tq // 2, LANES), jnp.float32),
                        pltpu.VMEM((seq // ck, tq, ck), jnp.float32),
                        pltpu.VMEM((seq // ck, tq, ck), jnp.float32),
                        pltpu.VMEM((pair, rep * tq, ATT_HEAD_DIM), jnp.bfloat16),
                        pltpu.VMEM((pair, seq // ck, rep * tq, ck), jnp.float32),
                        pltpu.VMEM((pair, rep * tq, LANES), jnp.float32),
                        pltpu.VMEM((pair, rep * tq, LANES), jnp.float32),
                        pltpu.VMEM((pair, rep * tq, ATT_HEAD_DIM), jnp.float32)],
        compiler_params=_params("parallel", "arbitrary"),
        name="dsa_attention",
    )(qk, qk, v, qi, kw, kw)


def _mlstm_kernel(q_ref, k_ref, v_ref, og_ref, ig_ref, lf_ref, hg_ref, o_ref):
    L = ML_CHUNK
    nc = q_ref.shape[0] // L
    dk, dv = ML_QK_DIM, ML_V_DIM
    heads = q_ref.shape[1] // dk
    row = lax.broadcasted_iota(jnp.int32, (L, L), 0)
    col = lax.broadcasted_iota(jnp.int32, (L, L), 1)
    tri = col <= row
    eye = col == row
    gain = hg_ref[...]

    def to_col(x_row):
        return jnp.sum(jnp.where(eye, jnp.broadcast_to(x_row, (L, L)), 0.0), axis=1, keepdims=True)

    def chunk(c, carry):
        return tuple(head_chunk(hd, c, carry[hd]) for hd in range(heads))

    def head_chunk(hd, c, state):
        C, n_row, m = state
        off = pl.multiple_of(c * L, L)
        qc = q_ref[pl.ds(off, L), hd * dk:(hd + 1) * dk]
        kc = k_ref[pl.ds(off, L), hd * dk:(hd + 1) * dk]
        vc = v_ref[pl.ds(off, L), hd * dv:(hd + 1) * dv]
        ig_r = ig_ref[hd, c]
        lf_r = lf_ref[hd, c]

        lf_b = jnp.broadcast_to(lf_r, (L, L))
        b_col = jnp.sum(jnp.where(tri, lf_b, 0.0), axis=1, keepdims=True)
        lf_col = to_col(lf_r)
        b_row = jnp.sum(jnp.where(row <= col, jnp.broadcast_to(lf_col, (L, L)), 0.0), axis=0, keepdims=True)
        a = jnp.sum(lf_r, axis=1, keepdims=True)

        g_row = a - b_row + ig_r
        m_loc = jnp.max(g_row, axis=1, keepdims=True)
        w_row = jnp.exp(g_row - m_loc)
        w_col = to_col(w_row)
        c_loc = lax.dot_general(kc, (w_col * vc.astype(jnp.float32)).astype(jnp.bfloat16),
                                (((0,), (0,)), ((), ())), preferred_element_type=jnp.float32)
        n_loc = jnp.dot(jnp.broadcast_to(w_row, (8, L)).astype(jnp.bfloat16), kc,
                        preferred_element_type=jnp.float32)[0:1]

        dm = jnp.where(tri, b_col - b_row + ig_r, -jnp.inf)
        inter = b_col + m
        m_t = jnp.maximum(inter, jnp.max(dm, axis=1, keepdims=True))
        s = lax.dot_general(qc, kc, (((1,), (1,)), ((), ())), preferred_element_type=jnp.float32)
        s = s * jnp.exp(dm - m_t)
        s_inter = jnp.exp(inter - m_t)
        qf = qc.astype(jnp.float32)
        num = (jnp.dot(s.astype(jnp.bfloat16), vc, preferred_element_type=jnp.float32)
               + s_inter * jnp.dot(qc, C.astype(jnp.bfloat16), preferred_element_type=jnp.float32))
        den = jnp.sum(s, axis=1, keepdims=True) + s_inter * jnp.sum(qf * n_row, axis=1, keepdims=True)
        h = num / jnp.maximum(jnp.abs(den), jnp.exp(-m_t))

        ms = jnp.mean(h * h, axis=-1, keepdims=True)
        hn = h * lax.rsqrt(ms + EPS) * gain
        og = og_ref[pl.ds(off, L), hd * dv:(hd + 1) * dv]
        o_ref[pl.ds(off, L), hd * dv:(hd + 1) * dv] = (jax.nn.sigmoid(og) * hn).astype(o_ref.dtype)

        m_new = jnp.maximum(a + m, m_loc)
        s_old = jnp.exp(a + m - m_new)
        s_new = jnp.exp(m_loc - m_new)
        return (s_old * C + s_new * c_loc, s_old * n_row + s_new * n_loc, m_new)

    init = (jnp.zeros((dk, dv), jnp.float32), jnp.zeros((1, dk), jnp.float32), jnp.zeros((1, 1), jnp.float32))
    lax.fori_loop(0, nc, chunk, (init,) * heads)


def _mlstm(qk, v, og, gates_t, h_gain, batch, seq):
    n = qk.shape[0]
    L = ML_CHUNK
    nc = seq // L
    hb = ML_HEADS_PER_STEP
    nhb = ML_HEADS // hb
    g4 = gates_t.reshape(2 * ML_HEADS, batch * nc, 1, L)
    return pl.pallas_call(
        _mlstm_kernel,
        grid=(batch, nhb),
        in_specs=[pl.BlockSpec((seq, hb * ML_QK_DIM), lambda b, h: (b, h)),
                  pl.BlockSpec((seq, hb * ML_QK_DIM), lambda b, h: (b, nhb + h)),
                  pl.BlockSpec((seq, hb * ML_V_DIM), lambda b, h: (b, h)),
                  pl.BlockSpec((seq, hb * ML_V_DIM), lambda b, h: (b, h)),
                  pl.BlockSpec((hb, nc, 1, L), lambda b, h: (h, b, 0, 0)),
                  pl.BlockSpec((hb, nc, 1, L), lambda b, h: (nhb + h, b, 0, 0)),
                  pl.BlockSpec((1, ML_V_DIM), lambda b, h: (0, 0))],
        out_specs=pl.BlockSpec((seq, hb * ML_V_DIM), lambda b, h: (b, h)),
        out_shape=jax.ShapeDtypeStruct((n, ML_HEADS * ML_V_DIM), jnp.bfloat16),
        compiler_params=_params("parallel", "parallel"),
        name="mlstm",
    )(qk, qk, v, og, g4, g4, h_gain.reshape(1, ML_V_DIM))


def _outproj_kernel(a_ref, w_ref, x_ref, o_ref):
    o_ref[...] = x_ref[...] + jnp.dot(a_ref[...], w_ref[...], preferred_element_type=jnp.float32)


def _outproj(a, w, x, tm=512):
    n, d = x.shape
    k = a.shape[1]
    return pl.pallas_call(
        _outproj_kernel,
        grid=(n // tm,),
        in_specs=[pl.BlockSpec((tm, k), lambda i: (i, 0)),
                  pl.BlockSpec((k, d), lambda i: (0, 0)),
                  pl.BlockSpec((tm, d), lambda i: (i, 0))],
        out_specs=pl.BlockSpec((tm, d), lambda i: (i, 0)),
        out_shape=jax.ShapeDtypeStruct((n, d), jnp.float32),
        compiler_params=_params("parallel"),
        name="outproj",
    )(a, w, x)


def _ffn_kernel(x_ref, g_ref, wu_ref, wd_ref, *rest):
    n_cast = (len(rest) - 2) // 2
    cast_src, o_ref, cast_dst, hn_ref = rest[:n_cast], rest[n_cast], rest[n_cast + 1:-1], rest[-1]
    for src, dst in zip(cast_src, cast_dst):
        dst[...] = src[...].astype(dst.dtype)

    @pl.when(pl.program_id(1) == 0)
    def _():
        x = x_ref[...]
        ms = jnp.mean(x * x, axis=-1, keepdims=True)
        hn_ref[...] = (x * lax.rsqrt(ms + EPS) * g_ref[...]).astype(hn_ref.dtype)
        o_ref[...] = x

    acts = []
    for s in range(wu_ref.shape[1] // FFN_SUB):
        u = jnp.dot(hn_ref[...], wu_ref[:, s * FFN_SUB:(s + 1) * FFN_SUB], preferred_element_type=jnp.float32)
        acts.append(jnp.square(jnp.maximum(u, 0.0)).astype(jnp.bfloat16))
    o_ref[...] += jnp.dot(jnp.concatenate(acts, axis=1), wd_ref[...], preferred_element_type=jnp.float32)


def _ffn(x, g, w_up, w_down, cast=(), tm=1024, tf=512):
    n, d = x.shape
    tm = min(tm, n)
    f = w_up.shape[1]
    ni, nj = n // tm, f // tf
    slabs = [a.reshape(ni * nj, a.shape[0] // (ni * nj), a.shape[1]) for a in cast]
    slab_spec = lambda a: pl.BlockSpec((None,) + a.shape[1:], lambda i, j: (i * nj + j, 0, 0))
    out = pl.pallas_call(
        _ffn_kernel,
        grid=(ni, nj),
        in_specs=[pl.BlockSpec((tm, d), lambda i, j: (i, 0)),
                  pl.BlockSpec((1, d), lambda i, j: (0, 0)),
                  pl.BlockSpec((d, tf), lambda i, j: (0, j)),
                  pl.BlockSpec((tf, d), lambda i, j: (j, 0))] + [slab_spec(a) for a in slabs],
        out_specs=[pl.BlockSpec((tm, d), lambda i, j: (i, 0))] + [slab_spec(a) for a in slabs],
        out_shape=[jax.ShapeDtypeStruct((n, d), jnp.float32)]
        + [jax.ShapeDtypeStruct(a.shape, jnp.bfloat16) for a in slabs],
        scratch_shapes=[pltpu.VMEM((tm, d), jnp.bfloat16)],
        compiler_params=_params("arbitrary", "arbitrary"),
        name="ffn",
    )(x, g.reshape(1, d), w_up, w_down, *slabs)
    return out[0], [o.reshape(a.shape) for o, a in zip(out[1:], cast)]


def _attention_layer(h, batch, seq, g_mix, w_in, q_gain, k_gain, w_out):
    bf = jnp.bfloat16
    n, d = h.shape
    n_qk = (ATT_HEADS + ATT_KV_HEADS) * ATT_HEAD_DIM
    n_v = ATT_KV_HEADS * ATT_HEAD_DIM
    n_qi = IDX_HEADS * IDX_HEAD_DIM
    tm = min(512, seq)

    c, s1, s2, half = _rope_tables(seq, ATT_HEAD_DIM, 1)
    ci, s1i, s2i, half_i = _rope_tables(seq, IDX_HEAD_DIM, LANES // IDX_HEAD_DIM)
    lane = jnp.arange(LANES)
    is_k = (lane < IDX_HEAD_DIM)[None, :]
    is_w = jnp.logical_and(lane >= IDX_HEAD_DIM, lane < IDX_HEAD_DIM + IDX_HEADS)[None, :]
    w_scale = IDX_HEADS ** -0.5 * IDX_HEAD_DIM ** -0.5
    tab = jnp.stack([c, s1, s2])
    tab_i = jnp.stack([ci, s1i, s2i])
    tab_t = jnp.stack([jnp.where(is_k, ci, jnp.where(is_w, w_scale, 0.0)),
                       jnp.where(is_k, s1i, 0.0), jnp.where(is_k, s2i, 0.0)])
    gains = jnp.stack([q_gain, k_gain]).reshape(2, 1, ATT_HEAD_DIM)
    n_main = n_qk + n_v + n_qi
    w = w_in.astype(bf)
    w_tail = jnp.pad(w[:, n_main:], ((0, 0), (0, n_main + LANES - w.shape[1])))

    row = lambda width: pl.BlockSpec((tm, width), lambda i: (i, 0))
    whole = lambda a: pl.BlockSpec(a.shape, lambda i: (0,) * a.ndim, pipeline_mode=pl.Buffered(1))
    tbl = pl.BlockSpec((3, tm, LANES), lambda i: (0, i % (seq // tm), 0))
    qk, v, qi, kw = pl.pallas_call(
        functools.partial(_att_proj_kernel, half=half, half_i=half_i),
        grid=(n // tm,),
        in_specs=[row(d), whole(g_mix.reshape(1, d)),
                  pl.BlockSpec((d, n_main), lambda i: (0, 0), pipeline_mode=pl.Buffered(1)), whole(w_tail),
                  whole(gains), tbl, tbl, tbl],
        out_specs=[row(n_qk), row(n_v), row(n_qi), row(LANES)],
        out_shape=[jax.ShapeDtypeStruct((n, n_qk), bf), jax.ShapeDtypeStruct((n, n_v), bf),
                   jax.ShapeDtypeStruct((n, n_qi), bf), jax.ShapeDtypeStruct((n, LANES), jnp.float32)],
        scratch_shapes=[pltpu.VMEM((tm, d), bf)],
        compiler_params=_params("parallel"),
        name="att_proj",
    )(h, g_mix.reshape(1, d), w, w_tail, gains, tab, tab_i, tab_t)

    o = _attention(qk, v, qi, kw, batch, seq)
    return _outproj(o, w_out.astype(bf), h)


def _mlstm_layer(h, batch, seq, g_mix, w_in, b_gate, h_gain, w_out):
    bf = jnp.bfloat16
    n, d = h.shape
    n_qk, n_v = 2 * ML_HEADS * ML_QK_DIM, ML_HEADS * ML_V_DIM
    n_g = 2 * ML_HEADS
    tm = min(512, seq)
    g2 = g_mix.reshape(1, d)
    w = w_in.astype(bf)
    wg_t = w_in[:, n_qk + n_v + d:].T.astype(bf)
    assert (n_qk + n_v) % d == 0

    row = lambda width: pl.BlockSpec((tm, width), lambda i: (i, 0))
    whole = lambda a: pl.BlockSpec(a.shape, lambda i: (0,) * a.ndim, pipeline_mode=pl.Buffered(1))
    wcols = lambda width, blk: pl.BlockSpec((d, width), lambda i: (0, blk), pipeline_mode=pl.Buffered(1))
    qk, v, gates_t = pl.pallas_call(
        _ml_proj_kernel,
        grid=(n // tm,),
        in_specs=[row(d), whole(g2), wcols(n_qk + n_v, 0), whole(wg_t), pl.BlockSpec((n_g, 1), lambda i: (0, 0))],
        out_specs=[row(n_qk), row(n_v), pl.BlockSpec((n_g, tm), lambda i: (0, i))],
        out_shape=[jax.ShapeDtypeStruct((n, n_qk), bf), jax.ShapeDtypeStruct((n, n_v), bf),
                   jax.ShapeDtypeStruct((n_g, n), jnp.float32)],
        scratch_shapes=[pltpu.VMEM((tm, d), bf)],
        compiler_params=_params("parallel"),
        name="ml_proj",
    )(h, g2, w, wg_t, b_gate.reshape(n_g, 1))
    og = pl.pallas_call(
        _norm_proj_kernel,
        grid=(n // tm,),
        in_specs=[row(d), whole(g2), wcols(d, (n_qk + n_v) // d)],
        out_specs=row(d),
        out_shape=jax.ShapeDtypeStruct((n, d), jnp.float32),
        scratch_shapes=[pltpu.VMEM((tm, d), bf)],
        compiler_params=_params("parallel"),
        name="ml_o_proj",
    )(h, g2, w)
    y = _mlstm(qk, v, og, gates_t, h_gain, batch, seq)
    return _outproj(y, w_out.astype(bf), h)


def kernel(x, norm_mix, norm_ffn, att_w_in, att_q_gain, att_k_gain, att_w_out, ml_w_in, ml_b_gate, ml_h_gain,
           ml_w_out, ffn_w_up, ffn_w_down):
    batch, seq, d = x.shape
    bf = jnp.bfloat16
    h = x.reshape(batch * seq, d)
    depth = norm_mix.shape[0]
    ready = {"w_up": ffn_w_up[0].astype(bf), "w_down": ffn_w_down[0].astype(bf),
             "w_in": att_w_in[0].astype(bf), "w_out": att_w_out[0].astype(bf)}
    for i in range(depth):
        j = i // 2
        if i % 2 == 0:
            h = _attention_layer(h, batch, seq, norm_mix[i], ready["w_in"], att_q_gain[j], att_k_gain[j],
                                 ready["w_out"])
        else:
            h = _mlstm_layer(h, batch, seq, norm_mix[i], ready["w_in"], ml_b_gate[j], ml_h_gain[j], ready["w_out"])
        nxt = i + 1
        cast = []
        if nxt < depth:
            mixer_w = (att_w_in, att_w_out) if nxt % 2 == 0 else (ml_w_in, ml_w_out)
            cast = [ffn_w_up[nxt], ffn_w_down[nxt], mixer_w[0][nxt // 2], mixer_w[1][nxt // 2]]
        h, done = _ffn(h, norm_ffn[i], ready["w_up"], ready["w_down"], cast)
        if done:
            ready = dict(zip(("w_up", "w_down", "w_in", "w_out"), done))
    return h.reshape(batch, seq, d)
```

```python
import functools

import jax
import jax.numpy as jnp
from jax import lax
from jax.experimental import pallas as pl
from jax.experimental.pallas import tpu as pltpu

D_MODEL = 2048
ATT_HEADS = 16
ATT_KV_HEADS = 4
ATT_HEAD_DIM = 128
IDX_HEADS = 16
IDX_HEAD_DIM = 64
TOPK_MAX = 256
ML_HEADS = 8
ML_V_DIM = 256
ML_QK_DIM = 128
GATE_SOFTCAP = 15.0
D_FF = 4 * D_MODEL
ROPE_THETA = 500000.0
ROT_FRAC = 4
EPS = 1e-6

LANES = 128
VMEM_LIMIT = 56 * 1024 * 1024
INT_MIN = -(2 ** 31)
NEG_BIG = -1e30

LOG2_E = 1.4426950408889634

ATT_Q_BLOCK = 256
ATT_KEY_CHUNK = 256
SEARCH_UNROLL = 4
ML_CHUNK = 128
ML_HEADS_PER_STEP = 4
PROJ_SUB = 512
FFN_SUB = 256


def _params(*sem):
    return pltpu.CompilerParams(dimension_semantics=sem, vmem_limit_bytes=VMEM_LIMIT)


def _store_rmsnorm(x_ref, g_ref, hn_ref):
    x = x_ref[...]
    ms = jnp.mean(x * x, axis=-1, keepdims=True)
    hn_ref[...] = (x * lax.rsqrt(ms + EPS) * g_ref[...]).astype(hn_ref.dtype)

def _rope_tables(seq, head_dim, heads_per_vreg):
    rot = head_dim // ROT_FRAC
    half = rot // 2
    inv_freq = ROPE_THETA ** (-2.0 * jnp.arange(half, dtype=jnp.float32) / rot)
    ang = jnp.arange(seq).astype(jnp.float32)[:, None] * inv_freq[None, :]
    cos, sin = jnp.cos(ang), jnp.sin(ang)
    ones = jnp.ones((seq, head_dim - rot), jnp.float32)
    zeros_h = jnp.zeros((seq, half), jnp.float32)
    zeros_r = jnp.zeros((seq, head_dim - rot), jnp.float32)
    c = jnp.concatenate([cos, cos, ones], axis=1)
    s1 = jnp.concatenate([-sin, zeros_h, zeros_r], axis=1)
    s2 = jnp.concatenate([zeros_h, sin, zeros_r], axis=1)
    rep = lambda t: jnp.tile(t, (1, heads_per_vreg))
    return rep(c), rep(s1), rep(s2), half


def _rope(y, c, s1, s2, half):
    return y * c + pltpu.roll(y, LANES - half, 1) * s1 + pltpu.roll(y, half, 1) * s2


def _att_proj_kernel(x_ref, g_ref, w_ref, wt_ref, gain_ref, tab_ref, tabi_ref, tabt_ref,
                     qk_ref, v_ref, qi_ref, kw_ref, hn_ref, *, half, half_i):
    _store_rmsnorm(x_ref, g_ref, hn_ref)
    n_qk, n_v, n_qi = qk_ref.shape[1], v_ref.shape[1], qi_ref.shape[1]
    n_q = ATT_HEADS * ATT_HEAD_DIM

    def cols(start, width):
        return jnp.dot(hn_ref[...], w_ref[:, start:start + width], preferred_element_type=jnp.float32)

    for s in range(n_qk // PROJ_SUB):
        acc = cols(s * PROJ_SUB, PROJ_SUB)
        gain = gain_ref[0 if s * PROJ_SUB < n_q else 1]
        for grp in range(PROJ_SUB // LANES):
            xg = acc[:, grp * LANES:(grp + 1) * LANES]
            ms = jnp.mean(xg * xg, axis=-1, keepdims=True)
            y = _rope(xg * lax.rsqrt(ms + EPS) * gain, tab_ref[0], tab_ref[1], tab_ref[2], half)
            qk_ref[:, s * PROJ_SUB + grp * LANES:s * PROJ_SUB + (grp + 1) * LANES] = y.astype(qk_ref.dtype)

    v_ref[...] = cols(n_qk, n_v).astype(v_ref.dtype)

    for s in range(n_qi // PROJ_SUB):
        acc = cols(n_qk + n_v + s * PROJ_SUB, PROJ_SUB)
        for grp in range(PROJ_SUB // LANES):
            y = _rope(acc[:, grp * LANES:(grp + 1) * LANES], tabi_ref[0], tabi_ref[1], tabi_ref[2], half_i)
            qi_ref[:, s * PROJ_SUB + grp * LANES:s * PROJ_SUB + (grp + 1) * LANES] = y.astype(qi_ref.dtype)

    tail = jnp.dot(hn_ref[...], wt_ref[...], preferred_element_type=jnp.float32)
    kw_ref[...] = _rope(tail, tabt_ref[0], tabt_ref[1], tabt_ref[2], half_i)


def _ml_proj_kernel(x_ref, g_ref, w_ref, wg_ref, b_ref, qk_ref, v_ref, gates_ref, hn_ref):
    _store_rmsnorm(x_ref, g_ref, hn_ref)
    n_qk = qk_ref.shape[1]
    for s in range((n_qk + v_ref.shape[1]) // PROJ_SUB):
        acc = jnp.dot(hn_ref[...], w_ref[:, s * PROJ_SUB:(s + 1) * PROJ_SUB], preferred_element_type=jnp.float32)
        start = s * PROJ_SUB
        if start < n_qk // 2:
            qk_ref[:, start:start + PROJ_SUB] = acc.astype(qk_ref.dtype)
        elif start < n_qk:
            qk_ref[:, start:start + PROJ_SUB] = (acc * ML_QK_DIM ** -0.5).astype(qk_ref.dtype)
        else:
            v_ref[:, start - n_qk:start - n_qk + PROJ_SUB] = acc.astype(v_ref.dtype)

    g = lax.dot_general(wg_ref[...], hn_ref[...], (((1,), (1,)), ((), ())),
                        preferred_element_type=jnp.float32)
    g = g + b_ref[...]
    g = GATE_SOFTCAP * jnp.tanh(g / GATE_SOFTCAP)
    logf = jnp.minimum(g, 0.0) - jnp.log1p(jnp.exp(-jnp.abs(g)))
    is_forget = lax.broadcasted_iota(jnp.int32, g.shape, 0) >= ML_HEADS
    gates_ref[...] = jnp.where(is_forget, logf, g)


def _norm_proj_kernel(x_ref, g_ref, w_ref, o_ref, hn_ref):
    _store_rmsnorm(x_ref, g_ref, hn_ref)
    for s in range(o_ref.shape[1] // PROJ_SUB):
        cols = slice(s * PROJ_SUB, (s + 1) * PROJ_SUB)
        o_ref[:, cols] = jnp.dot(hn_ref[...], w_ref[:, cols], preferred_element_type=jnp.float32).astype(o_ref.dtype)


def _key_to_f32(key):
    return pltpu.bitcast(jnp.where(key < 0, key ^ jnp.int32(0x7FFFFFFF), key), jnp.float32)


def _select_topk(score_ref, bias_ref, *, topk, n_chunks):
    _, tq, ck = score_ref.shape
    kf = jnp.float32(topk)

    def count(pred, thr):
        part = jnp.zeros((tq, LANES), jnp.float32)
        for c in range(n_chunks):
            for j in range(ck // LANES):
                part = part + jnp.where(pred(score_ref[c, :, j * LANES:(j + 1) * LANES], thr), 1.0, 0.0)
        return jnp.sum(part, axis=1, keepdims=True)

    key_ninf = INT_MIN + 0x7FFFFF
    key_pinf = 0x7F800000

    def search(it, tau):
        cand = tau + lax.shift_left(jnp.int32(1), jnp.int32(31) - it)
        in_range = jnp.logical_and(cand > tau, cand <= key_pinf)
        cnt = count(jnp.greater_equal, _key_to_f32(cand))
        return jnp.where(jnp.logical_and(in_range, cnt >= kf), cand, tau)

    tau = lax.fori_loop(0, 32, search, jnp.full((tq, 1), key_ninf, jnp.int32), unroll=SEARCH_UNROLL)
    thr = _key_to_f32(tau)
    finite = lambda s, th: jnp.logical_and(s >= th, s > -jnp.inf)
    n_ge = count(finite, thr)

    @pl.when(jnp.max(n_ge) <= kf)
    def _():
        for c in range(n_chunks):
            bias_ref[c] = jnp.where(finite(score_ref[c], thr), 0.0, NEG_BIG)

    @pl.when(jnp.max(n_ge) > kf)
    def _():
        need = kf - count(jnp.greater, thr)
        r = lax.broadcasted_iota(jnp.int32, (ck, ck), 0)
        col = lax.broadcasted_iota(jnp.int32, (ck, ck), 1)
        before = jnp.where(r < col, 1.0, 0.0).astype(jnp.bfloat16)
        seen = jnp.zeros((tq, 1), jnp.float32)
        for c in range(n_chunks):
            s = score_ref[c]
            eq = jnp.logical_and(s == thr, s > -jnp.inf)
            e = jnp.where(eq, 1.0, 0.0)
            prefix = jnp.dot(e.astype(jnp.bfloat16), before, preferred_element_type=jnp.float32) + seen
            seen = seen + jnp.sum(e, axis=1, keepdims=True)
            keep = jnp.logical_or(s > thr, jnp.logical_and(eq, prefix < need))
            bias_ref[c] = jnp.where(keep, 0.0, NEG_BIG)


def _for_each_chunk(n, body):
    def pair(p, carry):
        body(2 * p)
        body(2 * p + 1)
        return carry

    lax.fori_loop(0, n // 2, pair, 0)
    pl.when(n % 2 == 1)(lambda: body(n - 1))


def _attention_kernel(q_ref, k_ref, v_ref, qi_ref, wq_ref, kk_ref, o_ref,
                      qs_ref, wb_ref, score_ref, bias_ref, q4_ref, s_ref, m_ref, l_ref, acc_ref, *, topk):
    tq = q_ref.shape[0]
    ck = ATT_KEY_CHUNK
    half = tq // 2
    nt = (((1,), (1,)), ((), ()))
    i = pl.program_id(1)
    q0 = i * tq
    n_chunks = (q0 + tq) // ck

    for r in range(2):
        rows = slice(r * half, (r + 1) * half)
        for h in range(IDX_HEADS):
            qs_ref[r, h * half:(h + 1) * half, :] = (
                qi_ref[rows, h * IDX_HEAD_DIM:(h + 1) * IDX_HEAD_DIM].astype(jnp.bfloat16))
            wb_ref[r * IDX_HEADS + h] = jnp.broadcast_to(
                wq_ref[rows, IDX_HEAD_DIM + h:IDX_HEAD_DIM + h + 1], (half, LANES))

    def score_chunk(c):
        off = pl.multiple_of(c * ck, ck)
        ki = kk_ref[pl.ds(off, ck), :].astype(jnp.bfloat16)[:, :IDX_HEAD_DIM]
        spos = off + lax.broadcasted_iota(jnp.int32, (half, LANES), 1)
        for r in range(2):
            d = lax.dot_general(qs_ref[r], ki, nt, preferred_element_type=jnp.float32)
            tpos = q0 + r * half + lax.broadcasted_iota(jnp.int32, (half, LANES), 0)
            for j in range(ck // LANES):
                acc = jnp.zeros((half, LANES), jnp.float32)
                for h in range(IDX_HEADS):
                    dh = d[h * half:(h + 1) * half, j * LANES:(j + 1) * LANES]
                    acc = acc + jnp.maximum(dh, 0.0) * wb_ref[r * IDX_HEADS + h]
                score_ref[c, r * half:(r + 1) * half, j * LANES:(j + 1) * LANES] = (
                    jnp.where(spos + j * LANES <= tpos, acc, -jnp.inf))

    _for_each_chunk(n_chunks, score_chunk)

    for n in range(1, score_ref.shape[0] + 1):
        pl.when(n_chunks == n)(functools.partial(_select_topk, score_ref, bias_ref, topk=topk, n_chunks=n))

    exp2_scale = ATT_HEAD_DIM ** -0.5 * LOG2_E
    rep = ATT_HEADS // ATT_KV_HEADS
    pair = s_ref.shape[0]
    head_cols = lambda h: slice(h * ATT_HEAD_DIM, (h + 1) * ATT_HEAD_DIM)
    for g0 in range(0, ATT_KV_HEADS, pair):
        for u in range(pair):
            for r in range(rep):
                q4_ref[u, r * tq:(r + 1) * tq, :] = q_ref[:, head_cols((g0 + u) * rep + r)]
        m_ref[...] = jnp.full(m_ref.shape, NEG_BIG, jnp.float32)

        def logits_chunk(c):
            off = pl.multiple_of(c * ck, ck)
            for u in range(pair):
                s = lax.dot_general(q4_ref[u], k_ref[pl.ds(off, ck), head_cols(g0 + u)], nt,
                                    preferred_element_type=jnp.float32)
                s = ((s.reshape(rep, tq, ck) + bias_ref[c][None]) * exp2_scale).reshape(rep * tq, ck)
                s_ref[u, c] = s
                m = m_ref[u]
                for j in range(ck // LANES):
                    m = jnp.maximum(m, s[:, j * LANES:(j + 1) * LANES])
                m_ref[u] = m

        _for_each_chunk(n_chunks, logits_chunk)
        for u in range(pair):
            m_ref[u] = jnp.broadcast_to(jnp.max(m_ref[u], axis=1, keepdims=True), m_ref.shape[1:])
        l_ref[...] = jnp.zeros(l_ref.shape, jnp.float32)
        acc_ref[...] = jnp.zeros(acc_ref.shape, jnp.float32)

        def pv_chunk(c):
            off = pl.multiple_of(c * ck, ck)
            for u in range(pair):
                p = jnp.exp2(s_ref[u, c] - jnp.concatenate([m_ref[u]] * (ck // LANES), axis=1))
                l = l_ref[u]
                for j in range(ck // LANES):
                    l = l + p[:, j * LANES:(j + 1) * LANES]
                l_ref[u] = l
                acc_ref[u] += jnp.dot(p.astype(jnp.bfloat16), v_ref[pl.ds(off, ck), head_cols(g0 + u)],
                                      preferred_element_type=jnp.float32)

        _for_each_chunk(n_chunks, pv_chunk)
        for u in range(pair):
            o4 = acc_ref[u] / jnp.sum(l_ref[u], axis=1, keepdims=True)
            for r in range(rep):
                o_ref[:, head_cols((g0 + u) * rep + r)] = o4[r * tq:(r + 1) * tq].astype(o_ref.dtype)


def _attention(qk, v, qi, kw, batch, seq):
    n = qk.shape[0]
    tq, ck = ATT_Q_BLOCK, ATT_KEY_CHUNK
    nq = seq // tq
    rep = ATT_HEADS // ATT_KV_HEADS
    pair = 2
    topk = min(TOPK_MAX, seq // 4)
    q_w, kv_w = ATT_HEADS * ATT_HEAD_DIM, ATT_KV_HEADS * ATT_HEAD_DIM
    qi_w = IDX_HEADS * IDX_HEAD_DIM
    return pl.pallas_call(
        functools.partial(_attention_kernel, topk=topk),
        grid=(batch, nq),
        in_specs=[pl.BlockSpec((tq, q_w), lambda b, i: (b * nq + i, 0)),
                  pl.BlockSpec((seq, kv_w), lambda b, i: (b, q_w // kv_w)),
                  pl.BlockSpec((seq, kv_w), lambda b, i: (b, 0)),
                  pl.BlockSpec((tq, qi_w), lambda b, i: (b * nq + i, 0)),
                  pl.BlockSpec((tq, LANES), lambda b, i: (b * nq + i, 0)),
                  pl.BlockSpec((seq, LANES), lambda b, i: (b, 0))],
        out_specs=pl.BlockSpec((tq, q_w), lambda b, i: (b * nq + i, 0)),
        out_shape=jax.ShapeDtypeStruct((n, q_w), jnp.bfloat16),
        scratch_shapes=[pltpu.VMEM((2, IDX_HEADS * tq // 2, IDX_HEAD_DIM), jnp.bfloat16),
                        pltpu.VMEM((2 * IDX_HEADS, tq // 2, LANES), jnp.float32),
                        pltpu.VMEM((seq // ck, tq, ck), jnp.float32),
                        pltpu.VMEM((seq // ck, tq, ck), jnp.float32),
                        pltpu.VMEM((pair, rep * tq, ATT_HEAD_DIM), jnp.bfloat16),
                        pltpu.VMEM((pair, seq // ck, rep * tq, ck), jnp.float32),
                        pltpu.VMEM((pair, rep * tq, LANES), jnp.float32),
                        pltpu.VMEM((pair, rep * tq, LANES), jnp.float32),
                        pltpu.VMEM((pair, rep * tq, ATT_HEAD_DIM), jnp.float32)],
        compiler_params=_params("parallel", "arbitrary"),
        name="dsa_attention",
    )(qk, qk, v, qi, kw, kw)


def _mlstm_kernel(q_ref, k_ref, v_ref, og_ref, ig_ref, lf_ref, hg_ref, o_ref):
    L = ML_CHUNK
    nc = q_ref.shape[0] // L
    dk, dv = ML_QK_DIM, ML_V_DIM
    heads = q_ref.shape[1] // dk
    row = lax.broadcasted_iota(jnp.int32, (L, L), 0)
    col = lax.broadcasted_iota(jnp.int32, (L, L), 1)
    tri = col <= row
    eye = col == row
    gain = hg_ref[...]

    def to_col(x_row):
        return jnp.sum(jnp.where(eye, jnp.broadcast_to(x_row, (L, L)), 0.0), axis=1, keepdims=True)

    def chunk(c, carry):
        return tuple(head_chunk(hd, c, carry[hd]) for hd in range(heads))

    def head_chunk(hd, c, state):
        C, n_row, m = state
        off = pl.multiple_of(c * L, L)
        qc = q_ref[pl.ds(off, L), hd * dk:(hd + 1) * dk]
        kc = k_ref[pl.ds(off, L), hd * dk:(hd + 1) * dk]
        vc = v_ref[pl.ds(off, L), hd * dv:(hd + 1) * dv]
        ig_r = ig_ref[hd, c]
        lf_r = lf_ref[hd, c]

        lf_b = jnp.broadcast_to(lf_r, (L, L))
        b_col = jnp.sum(jnp.where(tri, lf_b, 0.0), axis=1, keepdims=True)
        lf_col = to_col(lf_r)
        b_row = jnp.sum(jnp.where(row <= col, jnp.broadcast_to(lf_col, (L, L)), 0.0), axis=0, keepdims=True)
        a = jnp.sum(lf_r, axis=1, keepdims=True)

        g_row = a - b_row + ig_r
        m_loc = jnp.max(g_row, axis=1, keepdims=True)
        w_row = jnp.exp(g_row - m_loc)
        w_col = to_col(w_row)
        c_loc = lax.dot_general(kc, (w_col * vc.astype(jnp.float32)).astype(jnp.bfloat16),
                                (((0,), (0,)), ((), ())), preferred_element_type=jnp.float32)
        n_loc = jnp.dot(jnp.broadcast_to(w_row, (8, L)).astype(jnp.bfloat16), kc,
                        preferred_element_type=jnp.float32)[0:1]

        dm = jnp.where(tri, b_col - b_row + ig_r, -jnp.inf)
        inter = b_col + m
        m_t = jnp.maximum(inter, jnp.max(dm, axis=1, keepdims=True))
        s = lax.dot_general(qc, kc, (((1,), (1,)), ((), ())), preferred_element_type=jnp.float32)
        s = s * jnp.exp(dm - m_t)
        s_inter = jnp.exp(inter - m_t)
        qf = qc.astype(jnp.float32)
        num = (jnp.dot(s.astype(jnp.bfloat16), vc, preferred_element_type=jnp.float32)
               + s_inter * jnp.dot(qc, C.astype(jnp.bfloat16), preferred_element_type=jnp.float32))
        den = jnp.sum(s, axis=1, keepdims=True) + s_inter * jnp.sum(qf * n_row, axis=1, keepdims=True)
        h = num / jnp.maximum(jnp.abs(den), jnp.exp(-m_t))

        ms = jnp.mean(h * h, axis=-1, keepdims=True)
        hn = h * lax.rsqrt(ms + EPS) * gain
        og = og_ref[pl.ds(off, L), hd * dv:(hd + 1) * dv]
        o_ref[pl.ds(off, L), hd * dv:(hd + 1) * dv] = (jax.nn.sigmoid(og) * hn).astype(o_ref.dtype)

        m_new = jnp.maximum(a + m, m_loc)
        s_old = jnp.exp(a + m - m_new)
        s_new = jnp.exp(m_loc - m_new)
        return (s_old * C + s_new * c_loc, s_old * n_row + s_new * n_loc, m_new)

    init = (jnp.zeros((dk, dv), jnp.float32), jnp.zeros((1, dk), jnp.float32), jnp.zeros((1, 1), jnp.float32))
    lax.fori_loop(0, nc, chunk, (init,) * heads)


def _mlstm(qk, v, og, gates_t, h_gain, batch, seq):
    n = qk.shape[0]
    L = ML_CHUNK
    nc = seq // L
    hb = ML_HEADS_PER_STEP
    nhb = ML_HEADS // hb
    g4 = gates_t.reshape(2 * ML_HEADS, batch * nc, 1, L)
    return pl.pallas_call(
        _mlstm_kernel,
        grid=(batch, nhb),
        in_specs=[pl.BlockSpec((seq, hb * ML_QK_DIM), lambda b, h: (b, h)),
                  pl.BlockSpec((seq, hb * ML_QK_DIM), lambda b, h: (b, nhb + h)),
                  pl.BlockSpec((seq, hb * ML_V_DIM), lambda b, h: (b, h)),
                  pl.BlockSpec((seq, hb * ML_V_DIM), lambda b, h: (b, h)),
                  pl.BlockSpec((hb, nc, 1, L), lambda b, h: (h, b, 0, 0)),
                  pl.BlockSpec((hb, nc, 1, L), lambda b, h: (nhb + h, b, 0, 0)),
                  pl.BlockSpec((1, ML_V_DIM), lambda b, h: (0, 0))],
        out_specs=pl.BlockSpec((seq, hb * ML_V_DIM), lambda b, h: (b, h)),
        out_shape=jax.ShapeDtypeStruct((n, ML_HEADS * ML_V_DIM), jnp.bfloat16),
        compiler_params=_params("parallel", "parallel"),
        name="mlstm",
    )(qk, qk, v, og, g4, g4, h_gain.reshape(1, ML_V_DIM))


def _outproj_kernel(a_ref, w_ref, x_ref, o_ref):
    o_ref[...] = x_ref[...] + jnp.dot(a_ref[...], w_ref[...], preferred_element_type=jnp.float32)


def _outproj(a, w, x, tm=512):
    n, d = x.shape
    k = a.shape[1]
    return pl.pallas_call(
        _outproj_kernel,
        grid=(n // tm,),
        in_specs=[pl.BlockSpec((tm, k), lambda i: (i, 0)),
                  pl.BlockSpec((k, d), lambda i: (0, 0)),
                  pl.BlockSpec((tm, d), lambda i: (i, 0))],
        out_specs=pl.BlockSpec((tm, d), lambda i: (i, 0)),
        out_shape=jax.ShapeDtypeStruct((n, d), jnp.float32),
        compiler_params=_params("parallel"),
        name="outproj",
    )(a, w, x)


def _ffn_kernel(x_ref, g_ref, wu_ref, wd_ref, *rest, cast_steps):
    n_cast = len(cast_steps)
    cast_src, o_ref, cast_dst, hn_ref = rest[:n_cast], rest[n_cast], rest[n_cast + 1:-1], rest[-1]
    step = pl.program_id(0) * pl.num_programs(1) + pl.program_id(1)
    for src, dst, steps in zip(cast_src, cast_dst, cast_steps):
        @pl.when(step < steps)
        def _(src=src, dst=dst):
            slab = src[...] if src.shape == dst.shape else src[...].T
            dst[...] = slab.astype(dst.dtype)

    @pl.when(pl.program_id(1) == 0)
    def _():
        x = x_ref[...]
        ms = jnp.mean(x * x, axis=-1, keepdims=True)
        hn_ref[...] = (x * lax.rsqrt(ms + EPS) * g_ref[...]).astype(hn_ref.dtype)
        o_ref[...] = x

    acts = []
    for s in range(wu_ref.shape[1] // FFN_SUB):
        u = jnp.dot(hn_ref[...], wu_ref[:, s * FFN_SUB:(s + 1) * FFN_SUB], preferred_element_type=jnp.float32)
        acts.append(jnp.square(jnp.maximum(u, 0.0)).astype(jnp.bfloat16))
    o_ref[...] += jnp.dot(jnp.concatenate(acts, axis=1), wd_ref[...], preferred_element_type=jnp.float32)


def _ffn(x, g, w_up, w_down, cast=(), tm=1024, tf=512):
    n, d = x.shape
    tm = min(tm, n)
    f = w_up.shape[1]
    ni, nj = n // tm, f // tf
    in_specs, out_specs, out_shapes, steps = [], [], [], []
    for a, layer, transposed in cast:
        rows, cols = a.shape[1:]
        if transposed:
            tiles = rows // LANES
            width = LANES * (-(-tiles // (ni * nj)))
            used = tiles * LANES // width
            assert used * width == tiles * LANES
            last = used - 1
            in_specs.append(pl.BlockSpec((None, width, cols),
                                         lambda i, j, layer=layer, last=last: (layer, jnp.minimum(i * nj + j, last), 0)))
            out_specs.append(pl.BlockSpec((cols, width), lambda i, j, last=last: (0, jnp.minimum(i * nj + j, last))))
            out_shapes.append(jax.ShapeDtypeStruct((cols, tiles * LANES), jnp.bfloat16))
        else:
            used = ni * nj
            in_specs.append(pl.BlockSpec((None, rows // used, cols), lambda i, j, layer=layer: (layer, i * nj + j, 0)))
            out_specs.append(pl.BlockSpec((rows // used, cols), lambda i, j: (i * nj + j, 0)))
            out_shapes.append(jax.ShapeDtypeStruct((rows, cols), jnp.bfloat16))
        steps.append(used)
    out = pl.pallas_call(
        functools.partial(_ffn_kernel, cast_steps=tuple(steps)),
        grid=(ni, nj),
        in_specs=[pl.BlockSpec((tm, d), lambda i, j: (i, 0)),
                  pl.BlockSpec((1, d), lambda i, j: (0, 0)),
                  pl.BlockSpec((d, tf), lambda i, j: (0, j)),
                  pl.BlockSpec((tf, d), lambda i, j: (j, 0))] + in_specs,
        out_specs=[pl.BlockSpec((tm, d), lambda i, j: (i, 0))] + out_specs,
        out_shape=[jax.ShapeDtypeStruct((n, d), jnp.float32)] + out_shapes,
        scratch_shapes=[pltpu.VMEM((tm, d), jnp.bfloat16)],
        compiler_params=_params("arbitrary", "arbitrary"),
        name="ffn",
    )(x, g.reshape(1, d), w_up, w_down, *[a for a, _, _ in cast])
    return out[0], out[1:]


def _attention_layer(h, batch, seq, g_mix, w, w_tail_t, q_gain, k_gain, w_out):
    bf = jnp.bfloat16
    n, d = h.shape
    n_qk = (ATT_HEADS + ATT_KV_HEADS) * ATT_HEAD_DIM
    n_v = ATT_KV_HEADS * ATT_HEAD_DIM
    n_qi = IDX_HEADS * IDX_HEAD_DIM
    tm = min(512, seq)

    c, s1, s2, half = _rope_tables(seq, ATT_HEAD_DIM, 1)
    ci, s1i, s2i, half_i = _rope_tables(seq, IDX_HEAD_DIM, LANES // IDX_HEAD_DIM)
    lane = jnp.arange(LANES)
    is_k = (lane < IDX_HEAD_DIM)[None, :]
    is_w = jnp.logical_and(lane >= IDX_HEAD_DIM, lane < IDX_HEAD_DIM + IDX_HEADS)[None, :]
    w_scale = IDX_HEADS ** -0.5 * IDX_HEAD_DIM ** -0.5
    tab = jnp.stack([c, s1, s2])
    tab_i = jnp.stack([ci, s1i, s2i])
    tab_t = jnp.stack([jnp.where(is_k, ci, jnp.where(is_w, w_scale, 0.0)),
                       jnp.where(is_k, s1i, 0.0), jnp.where(is_k, s2i, 0.0)])
    gains = jnp.stack([q_gain, k_gain]).reshape(2, 1, ATT_HEAD_DIM)
    n_main = n_qk + n_v + n_qi
    w_tail = jnp.pad(w_tail_t.T.astype(bf), ((0, 0), (0, LANES - w_tail_t.shape[0])))

    row = lambda width: pl.BlockSpec((tm, width), lambda i: (i, 0))
    whole = lambda a: pl.BlockSpec(a.shape, lambda i: (0,) * a.ndim, pipeline_mode=pl.Buffered(1))
    tbl = pl.BlockSpec((3, tm, LANES), lambda i: (0, i % (seq // tm), 0))
    qk, v, qi, kw = pl.pallas_call(
        functools.partial(_att_proj_kernel, half=half, half_i=half_i),
        grid=(n // tm,),
        in_specs=[row(d), whole(g_mix.reshape(1, d)),
                  pl.BlockSpec((d, n_main), lambda i: (0, 0), pipeline_mode=pl.Buffered(1)), whole(w_tail),
                  whole(gains), tbl, tbl, tbl],
        out_specs=[row(n_qk), row(n_v), row(n_qi), row(LANES)],
        out_shape=[jax.ShapeDtypeStruct((n, n_qk), bf), jax.ShapeDtypeStruct((n, n_v), bf),
                   jax.ShapeDtypeStruct((n, n_qi), bf), jax.ShapeDtypeStruct((n, LANES), jnp.float32)],
        scratch_shapes=[pltpu.VMEM((tm, d), bf)],
        compiler_params=_params("parallel"),
        name="att_proj",
    )(h, g_mix.reshape(1, d), w, w_tail, gains, tab, tab_i, tab_t)

    o = _attention(qk, v, qi, kw, batch, seq)
    return _outproj(o, w_out.astype(bf), h)


def _mlstm_layer(h, batch, seq, g_mix, w, w_tail_t, b_gate, h_gain, w_out):
    bf = jnp.bfloat16
    n, d = h.shape
    n_qk, n_v = 2 * ML_HEADS * ML_QK_DIM, ML_HEADS * ML_V_DIM
    n_g = 2 * ML_HEADS
    tm = min(512, seq)
    g2 = g_mix.reshape(1, d)
    wg_t = w_tail_t.astype(bf)
    assert (n_qk + n_v) % d == 0

    row = lambda width: pl.BlockSpec((tm, width), lambda i: (i, 0))
    whole = lambda a: pl.BlockSpec(a.shape, lambda i: (0,) * a.ndim, pipeline_mode=pl.Buffered(1))
    wcols = lambda width, blk: pl.BlockSpec((d, width), lambda i: (0, blk), pipeline_mode=pl.Buffered(1))
    qk, v, gates_t = pl.pallas_call(
        _ml_proj_kernel,
        grid=(n // tm,),
        in_specs=[row(d), whole(g2), wcols(n_qk + n_v, 0), whole(wg_t), pl.BlockSpec((n_g, 1), lambda i: (0, 0))],
        out_specs=[row(n_qk), row(n_v), pl.BlockSpec((n_g, tm), lambda i: (0, i))],
        out_shape=[jax.ShapeDtypeStruct((n, n_qk), bf), jax.ShapeDtypeStruct((n, n_v), bf),
                   jax.ShapeDtypeStruct((n_g, n), jnp.float32)],
        scratch_shapes=[pltpu.VMEM((tm, d), bf)],
        compiler_params=_params("parallel"),
        name="ml_proj",
    )(h, g2, w, wg_t, b_gate.reshape(n_g, 1))
    og = pl.pallas_call(
        _norm_proj_kernel,
        grid=(n // tm,),
        in_specs=[row(d), whole(g2), wcols(d, (n_qk + n_v) // d)],
        out_specs=row(d),
        out_shape=jax.ShapeDtypeStruct((n, d), jnp.float32),
        scratch_shapes=[pltpu.VMEM((tm, d), bf)],
        compiler_params=_params("parallel"),
        name="ml_o_proj",
    )(h, g2, w)
    y = _mlstm(qk, v, og, gates_t, h_gain, batch, seq)
    return _outproj(y, w_out.astype(bf), h)


def kernel(x, norm_mix, norm_ffn, att_w_in, att_q_gain, att_k_gain, att_w_out, ml_w_in, ml_b_gate, ml_h_gain,
           ml_w_out, ffn_w_up, ffn_w_down):
    batch, seq, d = x.shape
    bf = jnp.bfloat16
    h = x.reshape(batch * seq, d)
    depth = norm_mix.shape[0]
    att_w_in_t, ml_w_in_t = jnp.swapaxes(att_w_in, 1, 2), jnp.swapaxes(ml_w_in, 1, 2)
    tail_t = lambda w_t, idx: w_t[idx, w_t.shape[1] // LANES * LANES:, :]
    ready = {"w_up": ffn_w_up[0].astype(bf), "w_down": ffn_w_down[0].astype(bf),
             "w_in": att_w_in[0].astype(bf), "w_out": att_w_out[0].astype(bf)}
    for i in range(depth):
        j = i // 2
        if i % 2 == 0:
            h = _attention_layer(h, batch, seq, norm_mix[i], ready["w_in"], tail_t(att_w_in_t, j), att_q_gain[j],
                                 att_k_gain[j], ready["w_out"])
        else:
            h = _mlstm_layer(h, batch, seq, norm_mix[i], ready["w_in"], tail_t(ml_w_in_t, j), ml_b_gate[j],
                             ml_h_gain[j], ready["w_out"])
        nxt = i + 1
        cast = []
        if nxt < depth:
            w_in_t, w_out = (att_w_in_t, att_w_out) if nxt % 2 == 0 else (ml_w_in_t, ml_w_out)
            cast = [(ffn_w_up, nxt, False), (ffn_w_down, nxt, False), (w_in_t, nxt // 2, True),
                    (w_out, nxt // 2, False)]
        h, done = _ffn(h, norm_ffn[i], ready["w_up"], ready["w_down"], cast)
        if done:
            ready = dict(zip(("w_up", "w_down", "w_in", "w_out"), done))
    return h.reshape(batch, seq, d)
```

```python
import functools

import jax
import jax.numpy as jnp
from jax import lax
from jax.experimental import pallas as pl
from jax.experimental.pallas import tpu as pltpu

D_MODEL = 2048
ATT_HEADS = 16
ATT_KV_HEADS = 4
ATT_HEAD_DIM = 128
IDX_HEADS = 16
IDX_HEAD_DIM = 64
TOPK_MAX = 256
ML_HEADS = 8
ML_V_DIM = 256
ML_QK_DIM = 128
GATE_SOFTCAP = 15.0
D_FF = 4 * D_MODEL
ROPE_THETA = 500000.0
ROT_FRAC = 4
EPS = 1e-6

LANES = 128
VMEM_LIMIT = 56 * 1024 * 1024
INT_MIN = -(2 ** 31)
NEG_BIG = -1e30

LOG2_E = 1.4426950408889634

ATT_Q_BLOCK = 256
ATT_KEY_CHUNK = 256
SEARCH_UNROLL = 4
SEARCH_TRIPS = 32 // SEARCH_UNROLL
ML_CHUNK = 128
ML_HEADS_PER_STEP = 4
ML_CHUNK_UNROLL = 4
PROJ_SUB = 512
FFN_SUB = 256


def _params(*sem):
    return pltpu.CompilerParams(dimension_semantics=sem, vmem_limit_bytes=VMEM_LIMIT)


def _store_rmsnorm(x_ref, g_ref, hn_ref):
    x = x_ref[...]
    ms = jnp.mean(x * x, axis=-1, keepdims=True)
    hn_ref[...] = (x * lax.rsqrt(ms + EPS) * g_ref[...]).astype(hn_ref.dtype)

def _rope_tables(seq, head_dim, heads_per_vreg):
    rot = head_dim // ROT_FRAC
    half = rot // 2
    inv_freq = ROPE_THETA ** (-2.0 * jnp.arange(half, dtype=jnp.float32) / rot)
    ang = jnp.arange(seq).astype(jnp.float32)[:, None] * inv_freq[None, :]
    cos, sin = jnp.cos(ang), jnp.sin(ang)
    ones = jnp.ones((seq, head_dim - rot), jnp.float32)
    zeros_h = jnp.zeros((seq, half), jnp.float32)
    zeros_r = jnp.zeros((seq, head_dim - rot), jnp.float32)
    c = jnp.concatenate([cos, cos, ones], axis=1)
    s1 = jnp.concatenate([-sin, zeros_h, zeros_r], axis=1)
    s2 = jnp.concatenate([zeros_h, sin, zeros_r], axis=1)
    rep = lambda t: jnp.tile(t, (1, heads_per_vreg))
    return rep(c), rep(s1), rep(s2), half


def _rope(y, c, s1, s2, half):
    return y * c + pltpu.roll(y, LANES - half, 1) * s1 + pltpu.roll(y, half, 1) * s2


def _att_proj_kernel(x_ref, g_ref, w_ref, wt_ref, gain_ref, tab_ref, tabi_ref, tabt_ref,
                     qk_ref, v_ref, qi_ref, kw_ref, hn_ref, *, half, half_i):
    _store_rmsnorm(x_ref, g_ref, hn_ref)
    n_qk, n_v, n_qi = qk_ref.shape[1], v_ref.shape[1], qi_ref.shape[1]
    n_q = ATT_HEADS * ATT_HEAD_DIM

    def cols(start, width):
        return jnp.dot(hn_ref[...], w_ref[:, start:start + width], preferred_element_type=jnp.float32)

    for s in range(n_qk // PROJ_SUB):
        acc = cols(s * PROJ_SUB, PROJ_SUB)
        gain = gain_ref[0 if s * PROJ_SUB < n_q else 1]
        for grp in range(PROJ_SUB // LANES):
            xg = acc[:, grp * LANES:(grp + 1) * LANES]
            ms = jnp.mean(xg * xg, axis=-1, keepdims=True)
            y = _rope(xg * lax.rsqrt(ms + EPS) * gain, tab_ref[0], tab_ref[1], tab_ref[2], half)
            qk_ref[:, s * PROJ_SUB + grp * LANES:s * PROJ_SUB + (grp + 1) * LANES] = y.astype(qk_ref.dtype)

    v_ref[...] = cols(n_qk, n_v).astype(v_ref.dtype)

    for s in range(n_qi // PROJ_SUB):
        acc = cols(n_qk + n_v + s * PROJ_SUB, PROJ_SUB)
        for grp in range(PROJ_SUB // LANES):
            y = _rope(acc[:, grp * LANES:(grp + 1) * LANES], tabi_ref[0], tabi_ref[1], tabi_ref[2], half_i)
            qi_ref[:, s * PROJ_SUB + grp * LANES:s * PROJ_SUB + (grp + 1) * LANES] = y.astype(qi_ref.dtype)

    tail = jnp.dot(hn_ref[...], wt_ref[...], preferred_element_type=jnp.float32)
    kw_ref[...] = _rope(tail, tabt_ref[0], tabt_ref[1], tabt_ref[2], half_i)


def _ml_proj_kernel(x_ref, g_ref, w_ref, wg_ref, b_ref, qk_ref, v_ref, gates_ref, hn_ref):
    _store_rmsnorm(x_ref, g_ref, hn_ref)
    n_qk = qk_ref.shape[1]
    for s in range((n_qk + v_ref.shape[1]) // PROJ_SUB):
        acc = jnp.dot(hn_ref[...], w_ref[:, s * PROJ_SUB:(s + 1) * PROJ_SUB], preferred_element_type=jnp.float32)
        start = s * PROJ_SUB
        if start < n_qk // 2:
            qk_ref[:, start:start + PROJ_SUB] = acc.astype(qk_ref.dtype)
        elif start < n_qk:
            qk_ref[:, start:start + PROJ_SUB] = (acc * ML_QK_DIM ** -0.5).astype(qk_ref.dtype)
        else:
            v_ref[:, start - n_qk:start - n_qk + PROJ_SUB] = acc.astype(v_ref.dtype)

    g = lax.dot_general(wg_ref[...], hn_ref[...], (((1,), (1,)), ((), ())),
                        preferred_element_type=jnp.float32)
    g = g + b_ref[...]
    g = GATE_SOFTCAP * jnp.tanh(g / GATE_SOFTCAP)
    logf = jnp.minimum(g, 0.0) - jnp.log1p(jnp.exp(-jnp.abs(g)))
    is_forget = lax.broadcasted_iota(jnp.int32, g.shape, 0) >= ML_HEADS
    gates_ref[...] = jnp.where(is_forget, logf, g)


def _norm_proj_kernel(x_ref, g_ref, w_ref, o_ref, hn_ref):
    _store_rmsnorm(x_ref, g_ref, hn_ref)
    for s in range(o_ref.shape[1] // PROJ_SUB):
        cols = slice(s * PROJ_SUB, (s + 1) * PROJ_SUB)
        o_ref[:, cols] = jnp.dot(hn_ref[...], w_ref[:, cols], preferred_element_type=jnp.float32).astype(o_ref.dtype)


def _key_to_f32(key):
    return pltpu.bitcast(jnp.where(key < 0, key ^ jnp.int32(0x7FFFFFFF), key), jnp.float32)


def _select_topk(score_ref, bias_ref, *, topk, n_chunks, side_work):
    _, tq, ck = score_ref.shape
    kf = jnp.float32(topk)

    def count(pred, thr):
        part = jnp.zeros((tq, LANES), jnp.float32)
        for c in range(n_chunks):
            for j in range(ck // LANES):
                part = part + jnp.where(pred(score_ref[c, :, j * LANES:(j + 1) * LANES], thr), 1.0, 0.0)
        return jnp.sum(part, axis=1, keepdims=True)

    key_ninf = INT_MIN + 0x7FFFFF
    key_pinf = 0x7F800000

    def search(it, tau):
        cand = tau + lax.shift_left(jnp.int32(1), jnp.int32(31) - it)
        in_range = jnp.logical_and(cand > tau, cand <= key_pinf)
        cnt = count(jnp.greater_equal, _key_to_f32(cand))
        return jnp.where(jnp.logical_and(in_range, cnt >= kf), cand, tau)

    def trip(t, tau):
        for u in range(SEARCH_UNROLL):
            tau = search(t * SEARCH_UNROLL + u, tau)
        side_work(t)
        return tau

    tau = lax.fori_loop(0, SEARCH_TRIPS, trip, jnp.full((tq, 1), key_ninf, jnp.int32))
    thr = _key_to_f32(tau)
    finite = lambda s, th: jnp.logical_and(s >= th, s > -jnp.inf)
    n_ge = count(finite, thr)

    @pl.when(jnp.max(n_ge) <= kf)
    def _():
        for c in range(n_chunks):
            bias_ref[c] = jnp.where(finite(score_ref[c], thr), 0.0, NEG_BIG)

    @pl.when(jnp.max(n_ge) > kf)
    def _():
        need = kf - count(jnp.greater, thr)
        r = lax.broadcasted_iota(jnp.int32, (ck, ck), 0)
        col = lax.broadcasted_iota(jnp.int32, (ck, ck), 1)
        before = jnp.where(r < col, 1.0, 0.0).astype(jnp.bfloat16)
        seen = jnp.zeros((tq, 1), jnp.float32)
        for c in range(n_chunks):
            s = score_ref[c]
            eq = jnp.logical_and(s == thr, s > -jnp.inf)
            e = jnp.where(eq, 1.0, 0.0)
            prefix = jnp.dot(e.astype(jnp.bfloat16), before, preferred_element_type=jnp.float32) + seen
            seen = seen + jnp.sum(e, axis=1, keepdims=True)
            keep = jnp.logical_or(s > thr, jnp.logical_and(eq, prefix < need))
            bias_ref[c] = jnp.where(keep, 0.0, NEG_BIG)


def _for_each_chunk(n, body):
    def pair(p, carry):
        body(2 * p)
        body(2 * p + 1)
        return carry

    lax.fori_loop(0, n // 2, pair, 0)
    pl.when(n % 2 == 1)(lambda: body(n - 1))


def _attention_kernel(q_ref, k_ref, v_ref, qi_ref, wq_ref, kk_ref, x_ref, wo_ref, h_ref, *scratch, topk):
    o_ref, proj_ref = scratch[-2:]
    i = pl.program_id(1)
    last = pl.num_programs(1) - 1
    sub = wo_ref.shape[2]
    assert wo_ref.shape[0] == SEARCH_TRIPS

    @pl.when(jnp.logical_and(pl.program_id(0) == 0, i == 0))
    def _():
        o_ref[...] = jnp.zeros(o_ref.shape, o_ref.dtype)

    def project(t):
        proj_ref[t] = jnp.dot(o_ref[...], wo_ref[t], preferred_element_type=jnp.float32)

    def finish_previous():
        for t in range(wo_ref.shape[0]):
            h_ref[:, t * sub:(t + 1) * sub] = x_ref[:, t * sub:(t + 1) * sub] + proj_ref[t]

    @pl.when(i < last)
    def _():
        _attention_block(q_ref, k_ref, v_ref, qi_ref, wq_ref, kk_ref, *scratch[:-1], topk=topk, q0=i * q_ref.shape[0],
                         project=project, finish_previous=finish_previous)

    @pl.when(i == last)
    def _():
        for t in range(wo_ref.shape[0]):
            project(t)
        finish_previous()


def _attention_block(q_ref, k_ref, v_ref, qi_ref, wq_ref, kk_ref,
                     qs_ref, wb_ref, score_ref, bias_ref, q4_ref, s_ref, m_ref, acc_ref, o_ref, *,
                     topk, q0, project, finish_previous):
    tq = q_ref.shape[0]
    ck = ATT_KEY_CHUNK
    half = tq // 2
    nt = (((1,), (1,)), ((), ()))
    n_chunks = (q0 + tq) // ck

    for r in range(2):
        rows = slice(r * half, (r + 1) * half)
        for h in range(IDX_HEADS):
            qs_ref[r, h * half:(h + 1) * half, :] = (
                qi_ref[rows, h * IDX_HEAD_DIM:(h + 1) * IDX_HEAD_DIM].astype(jnp.bfloat16))
            wb_ref[r * IDX_HEADS + h] = jnp.broadcast_to(
                wq_ref[rows, IDX_HEAD_DIM + h:IDX_HEAD_DIM + h + 1], (half, LANES))

    def score_chunk(c):
        off = pl.multiple_of(c * ck, ck)
        ki = kk_ref[pl.ds(off, ck), :].astype(jnp.bfloat16)[:, :IDX_HEAD_DIM]
        spos = off + lax.broadcasted_iota(jnp.int32, (half, LANES), 1)
        for r in range(2):
            d = lax.dot_general(qs_ref[r], ki, nt, preferred_element_type=jnp.float32)
            tpos = q0 + r * half + lax.broadcasted_iota(jnp.int32, (half, LANES), 0)
            for j in range(ck // LANES):
                acc = jnp.zeros((half, LANES), jnp.float32)
                for h in range(IDX_HEADS):
                    dh = d[h * half:(h + 1) * half, j * LANES:(j + 1) * LANES]
                    acc = acc + jnp.maximum(dh, 0.0) * wb_ref[r * IDX_HEADS + h]
                score_ref[c, r * half:(r + 1) * half, j * LANES:(j + 1) * LANES] = (
                    jnp.where(spos + j * LANES <= tpos, acc, -jnp.inf))

    _for_each_chunk(n_chunks, score_chunk)

    for n in range(1, score_ref.shape[0] + 1):
        pl.when(n_chunks == n)(functools.partial(_select_topk, score_ref, bias_ref, topk=topk, n_chunks=n,
                                                 side_work=project))
    finish_previous()

    exp2_scale = ATT_HEAD_DIM ** -0.5 * LOG2_E
    rep = ATT_HEADS // ATT_KV_HEADS
    pair = s_ref.shape[0]
    head_cols = lambda h: slice(h * ATT_HEAD_DIM, (h + 1) * ATT_HEAD_DIM)
    for g0 in range(0, ATT_KV_HEADS, pair):
        for u in range(pair):
            for r in range(rep):
                q4_ref[u, r * tq:(r + 1) * tq, :] = q_ref[:, head_cols((g0 + u) * rep + r)]
        m_ref[...] = jnp.full(m_ref.shape, NEG_BIG, jnp.float32)

        def logits_chunk(c):
            off = pl.multiple_of(c * ck, ck)
            for u in range(pair):
                s = lax.dot_general(q4_ref[u], k_ref[pl.ds(off, ck), head_cols(g0 + u)], nt,
                                    preferred_element_type=jnp.float32)
                s = ((s.reshape(rep, tq, ck) + bias_ref[c][None]) * exp2_scale).reshape(rep * tq, ck)
                s_ref[u, c] = s
                m = m_ref[u]
                for j in range(ck // LANES):
                    m = jnp.maximum(m, s[:, j * LANES:(j + 1) * LANES])
                m_ref[u] = m

        _for_each_chunk(n_chunks, logits_chunk)
        for u in range(pair):
            m_ref[u] = jnp.broadcast_to(jnp.max(m_ref[u], axis=1, keepdims=True), m_ref.shape[1:])
        acc_ref[...] = jnp.zeros(acc_ref.shape, jnp.float32)
        ones = jnp.ones((ck, LANES), jnp.bfloat16)

        def pv_chunk(c):
            off = pl.multiple_of(c * ck, ck)
            for u in range(pair):
                p = jnp.exp2(s_ref[u, c] - jnp.concatenate([m_ref[u]] * (ck // LANES), axis=1))
                v1 = jnp.concatenate([v_ref[pl.ds(off, ck), head_cols(g0 + u)], ones], axis=1)
                acc_ref[u] += jnp.dot(p.astype(jnp.bfloat16), v1, preferred_element_type=jnp.float32)

        _for_each_chunk(n_chunks, pv_chunk)
        for u in range(pair):
            o4 = acc_ref[u, :, :ATT_HEAD_DIM] / acc_ref[u, :, ATT_HEAD_DIM:]
            for r in range(rep):
                o_ref[:, head_cols((g0 + u) * rep + r)] = o4[r * tq:(r + 1) * tq].astype(o_ref.dtype)


def _attention(qk, v, qi, kw, x, w_out, batch, seq):
    n, d = x.shape
    tq, ck = ATT_Q_BLOCK, ATT_KEY_CHUNK
    nq = seq // tq
    rep = ATT_HEADS // ATT_KV_HEADS
    pair = 2
    topk = min(TOPK_MAX, seq // 4)
    q_w, kv_w = ATT_HEADS * ATT_HEAD_DIM, ATT_KV_HEADS * ATT_HEAD_DIM
    qi_w = IDX_HEADS * IDX_HEAD_DIM
    sub = d // SEARCH_TRIPS
    wo = w_out.reshape(q_w, SEARCH_TRIPS, sub).transpose(1, 0, 2)
    cur = lambda b, i: (b * nq + jnp.minimum(i, nq - 1), 0)
    prev = lambda b, i: (b * nq + jnp.maximum(i - 1, 0), 0)
    once = dict(pipeline_mode=pl.Buffered(1))
    return pl.pallas_call(
        functools.partial(_attention_kernel, topk=topk),
        grid=(batch, nq + 1),
        in_specs=[pl.BlockSpec((tq, q_w), cur),
                  pl.BlockSpec((seq, kv_w), lambda b, i: (b, q_w // kv_w), **once),
                  pl.BlockSpec((seq, kv_w), lambda b, i: (b, 0), **once),
                  pl.BlockSpec((tq, qi_w), cur),
                  pl.BlockSpec((tq, LANES), cur),
                  pl.BlockSpec((seq, LANES), lambda b, i: (b, 0), **once),
                  pl.BlockSpec((tq, d), prev),
                  pl.BlockSpec(wo.shape, lambda b, i: (0, 0, 0), **once)],
        out_specs=pl.BlockSpec((tq, d), prev),
        out_shape=jax.ShapeDtypeStruct((n, d), jnp.float32),
        scratch_shapes=[pltpu.VMEM((2, IDX_HEADS * tq // 2, IDX_HEAD_DIM), jnp.bfloat16),
                        pltpu.VMEM((2 * IDX_HEADS, tq // 2, LANES), jnp.float32),
                        pltpu.VMEM((seq // ck, tq, ck), jnp.float32),
                        pltpu.VMEM((seq // ck, tq, ck), jnp.float32),
                        pltpu.VMEM((pair, rep * tq, ATT_HEAD_DIM), jnp.bfloat16),
                        pltpu.VMEM((pair, seq // ck, rep * tq, ck), jnp.float32),
                        pltpu.VMEM((pair, rep * tq, LANES), jnp.float32),
                        pltpu.VMEM((pair, rep * tq, ATT_HEAD_DIM + LANES), jnp.float32),
                        pltpu.VMEM((tq, q_w), jnp.bfloat16),
                        pltpu.VMEM((SEARCH_TRIPS, tq, sub), jnp.float32)],
        compiler_params=_params("arbitrary", "arbitrary"),
        name="dsa_attention",
    )(qk, qk, v, qi, kw, kw, x, wo)


def _mlstm_kernel(q_ref, k_ref, v_ref, og_ref, ig_ref, lf_ref, hg_ref, o_ref):
    L = ML_CHUNK
    nc = q_ref.shape[0] // L
    dk, dv = ML_QK_DIM, ML_V_DIM
    heads = q_ref.shape[1] // dk
    row = lax.broadcasted_iota(jnp.int32, (L, L), 0)
    col = lax.broadcasted_iota(jnp.int32, (L, L), 1)
    tri = col <= row
    eye = col == row
    gain = hg_ref[...]

    def to_col(x_row):
        return jnp.sum(jnp.where(eye, jnp.broadcast_to(x_row, (L, L)), 0.0), axis=1, keepdims=True)

    def chunk(c, carry):
        return tuple(head_chunk(hd, c, carry[hd]) for hd in range(heads))

    def head_chunk(hd, c, state):
        C, n_row, m = state
        off = pl.multiple_of(c * L, L)
        qc = q_ref[pl.ds(off, L), hd * dk:(hd + 1) * dk]
        kc = k_ref[pl.ds(off, L), hd * dk:(hd + 1) * dk]
        vc = v_ref[pl.ds(off, L), hd * dv:(hd + 1) * dv]
        ig_r = ig_ref[hd, c]
        lf_r = lf_ref[hd, c]

        lf_b = jnp.broadcast_to(lf_r, (L, L))
        b_col = jnp.sum(jnp.where(tri, lf_b, 0.0), axis=1, keepdims=True)
        lf_col = to_col(lf_r)
        b_row = jnp.sum(jnp.where(row <= col, jnp.broadcast_to(lf_col, (L, L)), 0.0), axis=0, keepdims=True)
        a = jnp.sum(lf_r, axis=1, keepdims=True)

        g_row = a - b_row + ig_r
        m_loc = jnp.max(g_row, axis=1, keepdims=True)
        w_row = jnp.exp(g_row - m_loc)
        w_col = to_col(w_row)
        c_loc = lax.dot_general(kc, (w_col * vc.astype(jnp.float32)).astype(jnp.bfloat16),
                                (((0,), (0,)), ((), ())), preferred_element_type=jnp.float32)
        n_loc = jnp.dot(jnp.broadcast_to(w_row, (8, L)).astype(jnp.bfloat16), kc,
                        preferred_element_type=jnp.float32)[0:1]

        dm = jnp.where(tri, b_col - b_row + ig_r, -jnp.inf)
        inter = b_col + m
        m_t = jnp.maximum(inter, jnp.max(dm, axis=1, keepdims=True))
        s = lax.dot_general(qc, kc, (((1,), (1,)), ((), ())), preferred_element_type=jnp.float32)
        s = s * jnp.exp(dm - m_t)
        s_inter = jnp.exp(inter - m_t)
        qf = qc.astype(jnp.float32)
        num = (jnp.dot(s.astype(jnp.bfloat16), vc, preferred_element_type=jnp.float32)
               + s_inter * jnp.dot(qc, C.astype(jnp.bfloat16), preferred_element_type=jnp.float32))
        den = jnp.sum(s, axis=1, keepdims=True) + s_inter * jnp.sum(qf * n_row, axis=1, keepdims=True)
        h = num / jnp.maximum(jnp.abs(den), jnp.exp(-m_t))

        ms = jnp.mean(h * h, axis=-1, keepdims=True)
        hn = h * lax.rsqrt(ms + EPS) * gain
        og = og_ref[pl.ds(off, L), hd * dv:(hd + 1) * dv]
        o_ref[pl.ds(off, L), hd * dv:(hd + 1) * dv] = (jax.nn.sigmoid(og) * hn).astype(o_ref.dtype)

        m_new = jnp.maximum(a + m, m_loc)
        s_old = jnp.exp(a + m - m_new)
        s_new = jnp.exp(m_loc - m_new)
        return (s_old * C + s_new * c_loc, s_old * n_row + s_new * n_loc, m_new)

    init = (jnp.zeros((dk, dv), jnp.float32), jnp.zeros((1, dk), jnp.float32), jnp.zeros((1, 1), jnp.float32))
    lax.fori_loop(0, nc, chunk, (init,) * heads, unroll=ML_CHUNK_UNROLL)


def _mlstm(qk, v, og, gates_t, h_gain, batch, seq):
    n = qk.shape[0]
    L = ML_CHUNK
    nc = seq // L
    hb = ML_HEADS_PER_STEP
    nhb = ML_HEADS // hb
    g4 = gates_t.reshape(2 * ML_HEADS, batch * nc, 1, L)
    return pl.pallas_call(
        _mlstm_kernel,
        grid=(batch, nhb),
        in_specs=[pl.BlockSpec((seq, hb * ML_QK_DIM), lambda b, h: (b, h)),
                  pl.BlockSpec((seq, hb * ML_QK_DIM), lambda b, h: (b, nhb + h)),
                  pl.BlockSpec((seq, hb * ML_V_DIM), lambda b, h: (b, h)),
                  pl.BlockSpec((seq, hb * ML_V_DIM), lambda b, h: (b, h)),
                  pl.BlockSpec((hb, nc, 1, L), lambda b, h: (h, b, 0, 0)),
                  pl.BlockSpec((hb, nc, 1, L), lambda b, h: (nhb + h, b, 0, 0)),
                  pl.BlockSpec((1, ML_V_DIM), lambda b, h: (0, 0))],
        out_specs=pl.BlockSpec((seq, hb * ML_V_DIM), lambda b, h: (b, h)),
        out_shape=jax.ShapeDtypeStruct((n, ML_HEADS * ML_V_DIM), jnp.bfloat16),
        compiler_params=_params("parallel", "parallel"),
        name="mlstm",
    )(qk, qk, v, og, g4, g4, h_gain.reshape(1, ML_V_DIM))


def _outproj_kernel(a_ref, w_ref, x_ref, o_ref):
    o_ref[...] = x_ref[...] + jnp.dot(a_ref[...], w_ref[...], preferred_element_type=jnp.float32)


def _outproj(a, w, x, tm=512):
    n, d = x.shape
    k = a.shape[1]
    return pl.pallas_call(
        _outproj_kernel,
        grid=(n // tm,),
        in_specs=[pl.BlockSpec((tm, k), lambda i: (i, 0)),
                  pl.BlockSpec((k, d), lambda i: (0, 0)),
                  pl.BlockSpec((tm, d), lambda i: (i, 0))],
        out_specs=pl.BlockSpec((tm, d), lambda i: (i, 0)),
        out_shape=jax.ShapeDtypeStruct((n, d), jnp.float32),
        compiler_params=_params("parallel"),
        name="outproj",
    )(a, w, x)


def _ffn_kernel(x_ref, g_ref, wu_ref, wd_ref, *rest, cast_steps):
    n_cast = len(cast_steps)
    cast_src, o_ref, cast_dst, hn_ref = rest[:n_cast], rest[n_cast], rest[n_cast + 1:-1], rest[-1]
    step = pl.program_id(0) * pl.num_programs(1) + pl.program_id(1)
    for src, dst, steps in zip(cast_src, cast_dst, cast_steps):
        @pl.when(step < steps)
        def _(src=src, dst=dst):
            slab = src[...] if src.shape == dst.shape else src[...].T
            dst[...] = slab.astype(dst.dtype)

    @pl.when(pl.program_id(1) == 0)
    def _():
        x = x_ref[...]
        ms = jnp.mean(x * x, axis=-1, keepdims=True)
        hn_ref[...] = (x * lax.rsqrt(ms + EPS) * g_ref[...]).astype(hn_ref.dtype)
        o_ref[...] = x

    acts = []
    for s in range(wu_ref.shape[1] // FFN_SUB):
        u = jnp.dot(hn_ref[...], wu_ref[:, s * FFN_SUB:(s + 1) * FFN_SUB], preferred_element_type=jnp.float32)
        acts.append(jnp.square(jnp.maximum(u, 0.0)).astype(jnp.bfloat16))
    o_ref[...] += jnp.dot(jnp.concatenate(acts, axis=1), wd_ref[...], preferred_element_type=jnp.float32)


def _ffn(x, g, w_up, w_down, cast=(), tm=1024, tf=512):
    n, d = x.shape
    tm = min(tm, n)
    f = w_up.shape[1]
    ni, nj = n // tm, f // tf
    in_specs, out_specs, out_shapes, steps = [], [], [], []
    for a, layer, transposed in cast:
        rows, cols = a.shape[1:]
        if transposed:
            tiles = rows // LANES
            width = LANES * (-(-tiles // (ni * nj)))
            used = tiles * LANES // width
            assert used * width == tiles * LANES
            last = used - 1
            in_specs.append(pl.BlockSpec((None, width, cols),
                                         lambda i, j, layer=layer, last=last: (layer, jnp.minimum(i * nj + j, last), 0)))
            out_specs.append(pl.BlockSpec((cols, width), lambda i, j, last=last: (0, jnp.minimum(i * nj + j, last))))
            out_shapes.append(jax.ShapeDtypeStruct((cols, tiles * LANES), jnp.bfloat16))
        else:
            used = ni * nj
            in_specs.append(pl.BlockSpec((None, rows // used, cols), lambda i, j, layer=layer: (layer, i * nj + j, 0)))
            out_specs.append(pl.BlockSpec((rows // used, cols), lambda i, j: (i * nj + j, 0)))
            out_shapes.append(jax.ShapeDtypeStruct((rows, cols), jnp.bfloat16))
        steps.append(used)
    out = pl.pallas_call(
        functools.partial(_ffn_kernel, cast_steps=tuple(steps)),
        grid=(ni, nj),
        in_specs=[pl.BlockSpec((tm, d), lambda i, j: (i, 0)),
                  pl.BlockSpec((1, d), lambda i, j: (0, 0)),
                  pl.BlockSpec((d, tf), lambda i, j: (0, j)),
                  pl.BlockSpec((tf, d), lambda i, j: (j, 0))] + in_specs,
        out_specs=[pl.BlockSpec((tm, d), lambda i, j: (i, 0))] + out_specs,
        out_shape=[jax.ShapeDtypeStruct((n, d), jnp.float32)] + out_shapes,
        scratch_shapes=[pltpu.VMEM((tm, d), jnp.bfloat16)],
        compiler_params=_params("arbitrary", "arbitrary"),
        name="ffn",
    )(x, g.reshape(1, d), w_up, w_down, *[a for a, _, _ in cast])
    return out[0], out[1:]


def _attention_layer(h, batch, seq, g_mix, w, w_tail_t, q_gain, k_gain, w_out):
    bf = jnp.bfloat16
    n, d = h.shape
    n_qk = (ATT_HEADS + ATT_KV_HEADS) * ATT_HEAD_DIM
    n_v = ATT_KV_HEADS * ATT_HEAD_DIM
    n_qi = IDX_HEADS * IDX_HEAD_DIM
    tm = min(512, seq)

    c, s1, s2, half = _rope_tables(seq, ATT_HEAD_DIM, 1)
    ci, s1i, s2i, half_i = _rope_tables(seq, IDX_HEAD_DIM, LANES // IDX_HEAD_DIM)
    lane = jnp.arange(LANES)
    is_k = (lane < IDX_HEAD_DIM)[None, :]
    is_w = jnp.logical_and(lane >= IDX_HEAD_DIM, lane < IDX_HEAD_DIM + IDX_HEADS)[None, :]
    w_scale = IDX_HEADS ** -0.5 * IDX_HEAD_DIM ** -0.5
    tab = jnp.stack([c, s1, s2])
    tab_i = jnp.stack([ci, s1i, s2i])
    tab_t = jnp.stack([jnp.where(is_k, ci, jnp.where(is_w, w_scale, 0.0)),
                       jnp.where(is_k, s1i, 0.0), jnp.where(is_k, s2i, 0.0)])
    gains = jnp.stack([q_gain, k_gain]).reshape(2, 1, ATT_HEAD_DIM)
    n_main = n_qk + n_v + n_qi
    w_tail = jnp.pad(w_tail_t.T.astype(bf), ((0, 0), (0, LANES - w_tail_t.shape[0])))

    row = lambda width: pl.BlockSpec((tm, width), lambda i: (i, 0))
    whole = lambda a: pl.BlockSpec(a.shape, lambda i: (0,) * a.ndim, pipeline_mode=pl.Buffered(1))
    tbl = pl.BlockSpec((3, tm, LANES), lambda i: (0, i % (seq // tm), 0))
    qk, v, qi, kw = pl.pallas_call(
        functools.partial(_att_proj_kernel, half=half, half_i=half_i),
        grid=(n // tm,),
        in_specs=[row(d), whole(g_mix.reshape(1, d)),
                  pl.BlockSpec((d, n_main), lambda i: (0, 0), pipeline_mode=pl.Buffered(1)), whole(w_tail),
                  whole(gains), tbl, tbl, tbl],
        out_specs=[row(n_qk), row(n_v), row(n_qi), row(LANES)],
        out_shape=[jax.ShapeDtypeStruct((n, n_qk), bf), jax.ShapeDtypeStruct((n, n_v), bf),
                   jax.ShapeDtypeStruct((n, n_qi), bf), jax.ShapeDtypeStruct((n, LANES), jnp.float32)],
        scratch_shapes=[pltpu.VMEM((tm, d), bf)],
        compiler_params=_params("parallel"),
        name="att_proj",
    )(h, g_mix.reshape(1, d), w, w_tail, gains, tab, tab_i, tab_t)

    return _attention(qk, v, qi, kw, h, w_out.astype(bf), batch, seq)


def _mlstm_layer(h, batch, seq, g_mix, w, w_tail_t, b_gate, h_gain, w_out):
    bf = jnp.bfloat16
    n, d = h.shape
    n_qk, n_v = 2 * ML_HEADS * ML_QK_DIM, ML_HEADS * ML_V_DIM
    n_g = 2 * ML_HEADS
    tm = min(512, seq)
    g2 = g_mix.reshape(1, d)
    wg_t = w_tail_t.astype(bf)
    assert (n_qk + n_v) % d == 0

    row = lambda width: pl.BlockSpec((tm, width), lambda i: (i, 0))
    whole = lambda a: pl.BlockSpec(a.shape, lambda i: (0,) * a.ndim, pipeline_mode=pl.Buffered(1))
    wcols = lambda width, blk: pl.BlockSpec((d, width), lambda i: (0, blk), pipeline_mode=pl.Buffered(1))
    qk, v, gates_t = pl.pallas_call(
        _ml_proj_kernel,
        grid=(n // tm,),
        in_specs=[row(d), whole(g2), wcols(n_qk + n_v, 0), whole(wg_t), pl.BlockSpec((n_g, 1), lambda i: (0, 0))],
        out_specs=[row(n_qk), row(n_v), pl.BlockSpec((n_g, tm), lambda i: (0, i))],
        out_shape=[jax.ShapeDtypeStruct((n, n_qk), bf), jax.ShapeDtypeStruct((n, n_v), bf),
                   jax.ShapeDtypeStruct((n_g, n), jnp.float32)],
        scratch_shapes=[pltpu.VMEM((tm, d), bf)],
        compiler_params=_params("parallel"),
        name="ml_proj",
    )(h, g2, w, wg_t, b_gate.reshape(n_g, 1))
    og = pl.pallas_call(
        _norm_proj_kernel,
        grid=(n // tm,),
        in_specs=[row(d), whole(g2), wcols(d, (n_qk + n_v) // d)],
        out_specs=row(d),
        out_shape=jax.ShapeDtypeStruct((n, d), jnp.float32),
        scratch_shapes=[pltpu.VMEM((tm, d), bf)],
        compiler_params=_params("parallel"),
        name="ml_o_proj",
    )(h, g2, w)
    y = _mlstm(qk, v, og, gates_t, h_gain, batch, seq)
    return _outproj(y, w_out.astype(bf), h)


def kernel(x, norm_mix, norm_ffn, att_w_in, att_q_gain, att_k_gain, att_w_out, ml_w_in, ml_b_gate, ml_h_gain,
           ml_w_out, ffn_w_up, ffn_w_down):
    batch, seq, d = x.shape
    bf = jnp.bfloat16
    h = x.reshape(batch * seq, d)
    depth = norm_mix.shape[0]
    att_w_in_t, ml_w_in_t = jnp.swapaxes(att_w_in, 1, 2), jnp.swapaxes(ml_w_in, 1, 2)
    tail_t = lambda w_t, idx: w_t[idx, w_t.shape[1] // LANES * LANES:, :]
    ready = {"w_up": ffn_w_up[0].astype(bf), "w_down": ffn_w_down[0].astype(bf),
             "w_in": att_w_in[0].astype(bf), "w_out": att_w_out[0].astype(bf)}
    for i in range(depth):
        j = i // 2
        if i % 2 == 0:
            h = _attention_layer(h, batch, seq, norm_mix[i], ready["w_in"], tail_t(att_w_in_t, j), att_q_gain[j],
                                 att_k_gain[j], ready["w_out"])
        else:
            h = _mlstm_layer(h, batch, seq, norm_mix[i], ready["w_in"], tail_t(ml_w_in_t, j), ml_b_gate[j],
                             ml_h_gain[j], ready["w_out"])
        nxt = i + 1
        cast = []
        if nxt < depth:
            w_in_t, w_out = (att_w_in_t, att_w_out) if nxt % 2 == 0 else (ml_w_in_t, ml_w_out)
            cast = [(ffn_w_up, nxt, False), (ffn_w_down, nxt, False), (w_in_t, nxt // 2, True),
                    (w_out, nxt // 2, False)]
        h, done = _ffn(h, norm_ffn[i], ready["w_up"], ready["w_down"], cast)
        if done:
            ready = dict(zip(("w_up", "w_down", "w_in", "w_out"), done))
    return h.reshape(batch, seq, d)
```

```python
import functools

import jax
import jax.numpy as jnp
from jax import lax
from jax.experimental import pallas as pl
from jax.experimental.pallas import tpu as pltpu

D_MODEL = 2048
ATT_HEADS = 16
ATT_KV_HEADS = 4
ATT_HEAD_DIM = 128
IDX_HEADS = 16
IDX_HEAD_DIM = 64
TOPK_MAX = 256
ML_HEADS = 8
ML_V_DIM = 256
ML_QK_DIM = 128
GATE_SOFTCAP = 15.0
D_FF = 4 * D_MODEL
ROPE_THETA = 500000.0
ROT_FRAC = 4
EPS = 1e-6

LANES = 128
VMEM_LIMIT = 60 * 1024 * 1024
INT_MIN = -(2 ** 31)
NEG_BIG = -1e30

LOG2_E = 1.4426950408889634

ATT_Q_BLOCK = 256
ATT_KEY_CHUNK = 256
SEARCH_UNROLL = 4
SEARCH_TRIPS = 32 // SEARCH_UNROLL
ML_CHUNK = 128
ML_HEADS_PER_STEP = 4
ML_CHUNK_UNROLL = 4
PROJ_SUB = 512
FFN_SUB = 256


def _params(*sem):
    return pltpu.CompilerParams(dimension_semantics=sem, vmem_limit_bytes=VMEM_LIMIT)


def _store_rmsnorm(x_ref, g_ref, hn_ref):
    x = x_ref[...]
    ms = jnp.mean(x * x, axis=-1, keepdims=True)
    hn_ref[...] = (x * lax.rsqrt(ms + EPS) * g_ref[...]).astype(hn_ref.dtype)

def _rope_tables(seq, head_dim, heads_per_vreg):
    rot = head_dim // ROT_FRAC
    half = rot // 2
    inv_freq = ROPE_THETA ** (-2.0 * jnp.arange(half, dtype=jnp.float32) / rot)
    ang = jnp.arange(seq).astype(jnp.float32)[:, None] * inv_freq[None, :]
    cos, sin = jnp.cos(ang), jnp.sin(ang)
    ones = jnp.ones((seq, head_dim - rot), jnp.float32)
    zeros_h = jnp.zeros((seq, half), jnp.float32)
    zeros_r = jnp.zeros((seq, head_dim - rot), jnp.float32)
    c = jnp.concatenate([cos, cos, ones], axis=1)
    s1 = jnp.concatenate([-sin, zeros_h, zeros_r], axis=1)
    s2 = jnp.concatenate([zeros_h, sin, zeros_r], axis=1)
    rep = lambda t: jnp.tile(t, (1, heads_per_vreg))
    return rep(c), rep(s1), rep(s2), half


def _rope(y, c, s1, s2, half):
    return y * c + pltpu.roll(y, LANES - half, 1) * s1 + pltpu.roll(y, half, 1) * s2


def _att_proj_kernel(x_ref, g_ref, w_ref, wt_ref, gain_ref, tab_ref, tabi_ref, tabt_ref,
                     qk_ref, v_ref, qi_ref, kw_ref, hn_ref, *, half, half_i):
    _store_rmsnorm(x_ref, g_ref, hn_ref)
    n_qk, n_v, n_qi = qk_ref.shape[1], v_ref.shape[1], qi_ref.shape[1]
    n_q = ATT_HEADS * ATT_HEAD_DIM

    def cols(start, width):
        return jnp.dot(hn_ref[...], w_ref[:, start:start + width], preferred_element_type=jnp.float32)

    for s in range(n_qk // PROJ_SUB):
        acc = cols(s * PROJ_SUB, PROJ_SUB)
        gain = gain_ref[0 if s * PROJ_SUB < n_q else 1]
        for grp in range(PROJ_SUB // LANES):
            xg = acc[:, grp * LANES:(grp + 1) * LANES]
            ms = jnp.mean(xg * xg, axis=-1, keepdims=True)
            y = _rope(xg * lax.rsqrt(ms + EPS) * gain, tab_ref[0], tab_ref[1], tab_ref[2], half)
            qk_ref[:, s * PROJ_SUB + grp * LANES:s * PROJ_SUB + (grp + 1) * LANES] = y.astype(qk_ref.dtype)

    v_ref[...] = cols(n_qk, n_v).astype(v_ref.dtype)

    for s in range(n_qi // PROJ_SUB):
        acc = cols(n_qk + n_v + s * PROJ_SUB, PROJ_SUB)
        for grp in range(PROJ_SUB // LANES):
            y = _rope(acc[:, grp * LANES:(grp + 1) * LANES], tabi_ref[0], tabi_ref[1], tabi_ref[2], half_i)
            qi_ref[:, s * PROJ_SUB + grp * LANES:s * PROJ_SUB + (grp + 1) * LANES] = y.astype(qi_ref.dtype)

    tail = jnp.dot(hn_ref[...], wt_ref[...], preferred_element_type=jnp.float32)
    kw_ref[...] = _rope(tail, tabt_ref[0], tabt_ref[1], tabt_ref[2], half_i)


def _ml_proj_kernel(x_ref, g_ref, w_ref, wg_ref, b_ref, qk_ref, v_ref, gates_ref, hn_ref):
    _store_rmsnorm(x_ref, g_ref, hn_ref)
    n_qk = qk_ref.shape[1]
    for s in range((n_qk + v_ref.shape[1]) // PROJ_SUB):
        acc = jnp.dot(hn_ref[...], w_ref[:, s * PROJ_SUB:(s + 1) * PROJ_SUB], preferred_element_type=jnp.float32)
        start = s * PROJ_SUB
        if start < n_qk // 2:
            qk_ref[:, start:start + PROJ_SUB] = acc.astype(qk_ref.dtype)
        elif start < n_qk:
            qk_ref[:, start:start + PROJ_SUB] = (acc * ML_QK_DIM ** -0.5).astype(qk_ref.dtype)
        else:
            v_ref[:, start - n_qk:start - n_qk + PROJ_SUB] = acc.astype(v_ref.dtype)

    g = lax.dot_general(wg_ref[...], hn_ref[...], (((1,), (1,)), ((), ())),
                        preferred_element_type=jnp.float32)
    g = g + b_ref[...]
    g = GATE_SOFTCAP * jnp.tanh(g / GATE_SOFTCAP)
    logf = jnp.minimum(g, 0.0) - jnp.log1p(jnp.exp(-jnp.abs(g)))
    is_forget = lax.broadcasted_iota(jnp.int32, g.shape, 0) >= ML_HEADS
    gates_ref[...] = jnp.where(is_forget, logf, g)


def _norm_proj_kernel(x_ref, g_ref, w_ref, o_ref, hn_ref):
    _store_rmsnorm(x_ref, g_ref, hn_ref)
    for s in range(o_ref.shape[1] // PROJ_SUB):
        cols = slice(s * PROJ_SUB, (s + 1) * PROJ_SUB)
        o_ref[:, cols] = jnp.dot(hn_ref[...], w_ref[:, cols], preferred_element_type=jnp.float32).astype(o_ref.dtype)


def _key_to_f32(key):
    return pltpu.bitcast(jnp.where(key < 0, key ^ jnp.int32(0x7FFFFFFF), key), jnp.float32)


def _select_topk(score_ref, bias_ref, *, topk, n_chunks, side_work):
    _, tq, ck = score_ref.shape
    kf = jnp.float32(topk)

    def count(pred, thr):
        part = jnp.zeros((tq, LANES), jnp.float32)
        for c in range(n_chunks):
            for j in range(ck // LANES):
                part = part + jnp.where(pred(score_ref[c, :, j * LANES:(j + 1) * LANES], thr), 1.0, 0.0)
        return jnp.sum(part, axis=1, keepdims=True)

    key_ninf = INT_MIN + 0x7FFFFF
    key_pinf = 0x7F800000

    def search(it, tau):
        cand = tau + lax.shift_left(jnp.int32(1), jnp.int32(31) - it)
        in_range = jnp.logical_and(cand > tau, cand <= key_pinf)
        cnt = count(jnp.greater_equal, _key_to_f32(cand))
        return jnp.where(jnp.logical_and(in_range, cnt >= kf), cand, tau)

    def trip(t, tau):
        for u in range(SEARCH_UNROLL):
            tau = search(t * SEARCH_UNROLL + u, tau)
        side_work(t)
        return tau

    tau = lax.fori_loop(0, SEARCH_TRIPS, trip, jnp.full((tq, 1), key_ninf, jnp.int32))
    thr = _key_to_f32(tau)
    finite = lambda s, th: jnp.logical_and(s >= th, s > -jnp.inf)
    n_ge = count(finite, thr)

    @pl.when(jnp.max(n_ge) <= kf)
    def _():
        for c in range(n_chunks):
            bias_ref[c] = jnp.where(finite(score_ref[c], thr), 0.0, NEG_BIG)

    @pl.when(jnp.max(n_ge) > kf)
    def _():
        need = kf - count(jnp.greater, thr)
        r = lax.broadcasted_iota(jnp.int32, (ck, ck), 0)
        col = lax.broadcasted_iota(jnp.int32, (ck, ck), 1)
        before = jnp.where(r < col, 1.0, 0.0).astype(jnp.bfloat16)
        seen = jnp.zeros((tq, 1), jnp.float32)
        for c in range(n_chunks):
            s = score_ref[c]
            eq = jnp.logical_and(s == thr, s > -jnp.inf)
            e = jnp.where(eq, 1.0, 0.0)
            prefix = jnp.dot(e.astype(jnp.bfloat16), before, preferred_element_type=jnp.float32) + seen
            seen = seen + jnp.sum(e, axis=1, keepdims=True)
            keep = jnp.logical_or(s > thr, jnp.logical_and(eq, prefix < need))
            bias_ref[c] = jnp.where(keep, 0.0, NEG_BIG)


def _for_each_chunk(n, body):
    def pair(p, carry):
        body(2 * p)
        body(2 * p + 1)
        return carry

    lax.fori_loop(0, n // 2, pair, 0)
    pl.when(n % 2 == 1)(lambda: body(n - 1))


def _attention_kernel(q_ref, k_ref, v_ref, qi_ref, wq_ref, kk_ref, x_ref, wo_ref, h_ref, *scratch, topk):
    o_ref, proj_ref = scratch[-2:]
    i = pl.program_id(1)
    last = pl.num_programs(1) - 1
    sub = wo_ref.shape[2]
    assert wo_ref.shape[0] == SEARCH_TRIPS

    @pl.when(jnp.logical_and(pl.program_id(0) == 0, i == 0))
    def _():
        o_ref[...] = jnp.zeros(o_ref.shape, o_ref.dtype)

    def project(t):
        proj_ref[t] = jnp.dot(o_ref[...], wo_ref[t], preferred_element_type=jnp.float32)

    def finish_previous():
        for t in range(wo_ref.shape[0]):
            h_ref[:, t * sub:(t + 1) * sub] = x_ref[:, t * sub:(t + 1) * sub] + proj_ref[t]

    @pl.when(i < last)
    def _():
        _attention_block(q_ref, k_ref, v_ref, qi_ref, wq_ref, kk_ref, *scratch[:-1], topk=topk, q0=i * q_ref.shape[0],
                         project=project, finish_previous=finish_previous)

    @pl.when(i == last)
    def _():
        for t in range(wo_ref.shape[0]):
            project(t)
        finish_previous()


def _attention_block(q_ref, k_ref, v_ref, qi_ref, wq_ref, kk_ref,
                     qs_ref, wb_ref, score_ref, bias_ref, q4_ref, s_ref, m_ref, acc_ref, o_ref, *,
                     topk, q0, project, finish_previous):
    tq = q_ref.shape[0]
    ck = ATT_KEY_CHUNK
    half = tq // 2
    nt = (((1,), (1,)), ((), ()))
    n_chunks = (q0 + tq) // ck

    for r in range(2):
        rows = slice(r * half, (r + 1) * half)
        for h in range(IDX_HEADS):
            qs_ref[r, h * half:(h + 1) * half, :] = (
                qi_ref[rows, h * IDX_HEAD_DIM:(h + 1) * IDX_HEAD_DIM].astype(jnp.bfloat16))
            wb_ref[r * IDX_HEADS + h] = jnp.broadcast_to(
                wq_ref[rows, IDX_HEAD_DIM + h:IDX_HEAD_DIM + h + 1], (half, LANES))

    def score_chunk(c):
        off = pl.multiple_of(c * ck, ck)
        ki = kk_ref[pl.ds(off, ck), :].astype(jnp.bfloat16)[:, :IDX_HEAD_DIM]
        spos = off + lax.broadcasted_iota(jnp.int32, (half, LANES), 1)
        for r in range(2):
            d = lax.dot_general(qs_ref[r], ki, nt, preferred_element_type=jnp.float32)
            tpos = q0 + r * half + lax.broadcasted_iota(jnp.int32, (half, LANES), 0)
            for j in range(ck // LANES):
                acc = jnp.zeros((half, LANES), jnp.float32)
                for h in range(IDX_HEADS):
                    dh = d[h * half:(h + 1) * half, j * LANES:(j + 1) * LANES]
                    acc = acc + jnp.maximum(dh, 0.0) * wb_ref[r * IDX_HEADS + h]
                score_ref[c, r * half:(r + 1) * half, j * LANES:(j + 1) * LANES] = (
                    jnp.where(spos + j * LANES <= tpos, acc, -jnp.inf))

    _for_each_chunk(n_chunks, score_chunk)

    for n in range(1, score_ref.shape[0] + 1):
        pl.when(n_chunks == n)(functools.partial(_select_topk, score_ref, bias_ref, topk=topk, n_chunks=n,
                                                 side_work=project))
    finish_previous()

    exp2_scale = ATT_HEAD_DIM ** -0.5 * LOG2_E
    rep = ATT_HEADS // ATT_KV_HEADS
    pair = s_ref.shape[0]
    head_cols = lambda h: slice(h * ATT_HEAD_DIM, (h + 1) * ATT_HEAD_DIM)
    for g0 in range(0, ATT_KV_HEADS, pair):
        for u in range(pair):
            for r in range(rep):
                q4_ref[u, r * tq:(r + 1) * tq, :] = q_ref[:, head_cols((g0 + u) * rep + r)]
        m_ref[...] = jnp.full(m_ref.shape, NEG_BIG, jnp.float32)

        def logits_chunk(c):
            off = pl.multiple_of(c * ck, ck)
            for u in range(pair):
                s = lax.dot_general(q4_ref[u], k_ref[pl.ds(off, ck), head_cols(g0 + u)], nt,
                                    preferred_element_type=jnp.float32)
                s = ((s.reshape(rep, tq, ck) + bias_ref[c][None]) * exp2_scale).reshape(rep * tq, ck)
                s_ref[u, c] = s
                m = m_ref[u]
                for j in range(ck // LANES):
                    m = jnp.maximum(m, s[:, j * LANES:(j + 1) * LANES])
                m_ref[u] = m

        _for_each_chunk(n_chunks, logits_chunk)
        for u in range(pair):
            m_ref[u] = jnp.broadcast_to(jnp.max(m_ref[u], axis=1, keepdims=True), m_ref.shape[1:])
        acc_ref[...] = jnp.zeros(acc_ref.shape, jnp.float32)
        ones = jnp.ones((ck, LANES), jnp.bfloat16)

        def pv_chunk(c):
            off = pl.multiple_of(c * ck, ck)
            for u in range(pair):
                p = jnp.exp2(s_ref[u, c] - jnp.concatenate([m_ref[u]] * (ck // LANES), axis=1))
                v1 = jnp.concatenate([v_ref[pl.ds(off, ck), head_cols(g0 + u)], ones], axis=1)
                acc_ref[u] += jnp.dot(p.astype(jnp.bfloat16), v1, preferred_element_type=jnp.float32)

        _for_each_chunk(n_chunks, pv_chunk)
        for u in range(pair):
            o4 = acc_ref[u, :, :ATT_HEAD_DIM] / acc_ref[u, :, ATT_HEAD_DIM:]
            for r in range(rep):
                o_ref[:, head_cols((g0 + u) * rep + r)] = o4[r * tq:(r + 1) * tq].astype(o_ref.dtype)


def _attention(qk, v, qi, kw, x, w_out, batch, seq):
    n, d = x.shape
    tq, ck = ATT_Q_BLOCK, ATT_KEY_CHUNK
    nq = seq // tq
    rep = ATT_HEADS // ATT_KV_HEADS
    pair = 2
    topk = min(TOPK_MAX, seq // 4)
    q_w, kv_w = ATT_HEADS * ATT_HEAD_DIM, ATT_KV_HEADS * ATT_HEAD_DIM
    qi_w = IDX_HEADS * IDX_HEAD_DIM
    sub = d // SEARCH_TRIPS
    wo = w_out.reshape(q_w, SEARCH_TRIPS, sub).transpose(1, 0, 2)
    cur = lambda b, i: (b * nq + jnp.minimum(i, nq - 1), 0)
    prev = lambda b, i: (b * nq + jnp.maximum(i - 1, 0), 0)
    once = dict(pipeline_mode=pl.Buffered(1))
    return pl.pallas_call(
        functools.partial(_attention_kernel, topk=topk),
        grid=(batch, nq + 1),
        in_specs=[pl.BlockSpec((tq, q_w), cur),
                  pl.BlockSpec((seq, kv_w), lambda b, i: (b, q_w // kv_w), **once),
                  pl.BlockSpec((seq, kv_w), lambda b, i: (b, 0), **once),
                  pl.BlockSpec((tq, qi_w), cur),
                  pl.BlockSpec((tq, LANES), cur),
                  pl.BlockSpec((seq, LANES), lambda b, i: (b, 0), **once),
                  pl.BlockSpec((tq, d), prev),
                  pl.BlockSpec(wo.shape, lambda b, i: (0, 0, 0), **once)],
        out_specs=pl.BlockSpec((tq, d), prev),
        out_shape=jax.ShapeDtypeStruct((n, d), jnp.float32),
        scratch_shapes=[pltpu.VMEM((2, IDX_HEADS * tq // 2, IDX_HEAD_DIM), jnp.bfloat16),
                        pltpu.VMEM((2 * IDX_HEADS, tq // 2, LANES), jnp.float32),
                        pltpu.VMEM((seq // ck, tq, ck), jnp.float32),
                        pltpu.VMEM((seq // ck, tq, ck), jnp.float32),
                        pltpu.VMEM((pair, rep * tq, ATT_HEAD_DIM), jnp.bfloat16),
                        pltpu.VMEM((pair, seq // ck, rep * tq, ck), jnp.float32),
                        pltpu.VMEM((pair, rep * tq, LANES), jnp.float32),
                        pltpu.VMEM((pair, rep * tq, ATT_HEAD_DIM + LANES), jnp.float32),
                        pltpu.VMEM((tq, q_w), jnp.bfloat16),
                        pltpu.VMEM((SEARCH_TRIPS, tq, sub), jnp.float32)],
        compiler_params=_params("arbitrary", "arbitrary"),
        name="dsa_attention",
    )(qk, qk, v, qi, kw, kw, x, wo)


def _mlstm_kernel(q_ref, k_ref, v_ref, og_ref, ig_ref, lf_ref, hg_ref, o_ref):
    L = ML_CHUNK
    nc = q_ref.shape[0] // L
    dk, dv = ML_QK_DIM, ML_V_DIM
    heads = q_ref.shape[1] // dk
    row = lax.broadcasted_iota(jnp.int32, (L, L), 0)
    col = lax.broadcasted_iota(jnp.int32, (L, L), 1)
    tri = col <= row
    eye = col == row
    gain = hg_ref[...]

    def to_col(x_row):
        return jnp.sum(jnp.where(eye, jnp.broadcast_to(x_row, (L, L)), 0.0), axis=1, keepdims=True)

    def chunk(c, carry):
        return tuple(head_chunk(hd, c, carry[hd]) for hd in range(heads))

    def head_chunk(hd, c, state):
        C, n_row, m = state
        off = pl.multiple_of(c * L, L)
        qc = q_ref[pl.ds(off, L), hd * dk:(hd + 1) * dk]
        kc = k_ref[pl.ds(off, L), hd * dk:(hd + 1) * dk]
        vc = v_ref[pl.ds(off, L), hd * dv:(hd + 1) * dv]
        ig_r = ig_ref[hd, c]
        lf_r = lf_ref[hd, c]

        lf_b = jnp.broadcast_to(lf_r, (L, L))
        b_col = jnp.sum(jnp.where(tri, lf_b, 0.0), axis=1, keepdims=True)
        lf_col = to_col(lf_r)
        b_row = jnp.sum(jnp.where(row <= col, jnp.broadcast_to(lf_col, (L, L)), 0.0), axis=0, keepdims=True)
        a = jnp.sum(lf_r, axis=1, keepdims=True)

        g_row = a - b_row + ig_r
        m_loc = jnp.max(g_row, axis=1, keepdims=True)
        w_row = jnp.exp(g_row - m_loc)
        w_col = to_col(w_row)
        c_loc = lax.dot_general(kc, (w_col * vc.astype(jnp.float32)).astype(jnp.bfloat16),
                                (((0,), (0,)), ((), ())), preferred_element_type=jnp.float32)
        n_loc = jnp.dot(jnp.broadcast_to(w_row, (8, L)).astype(jnp.bfloat16), kc,
                        preferred_element_type=jnp.float32)[0:1]

        dm = jnp.where(tri, b_col - b_row + ig_r, -jnp.inf)
        inter = b_col + m
        m_t = jnp.maximum(inter, jnp.max(dm, axis=1, keepdims=True))
        s = lax.dot_general(qc, kc, (((1,), (1,)), ((), ())), preferred_element_type=jnp.float32)
        s = s * jnp.exp(dm - m_t)
        s_inter = jnp.exp(inter - m_t)
        qf = qc.astype(jnp.float32)
        num = (jnp.dot(s.astype(jnp.bfloat16), vc, preferred_element_type=jnp.float32)
               + s_inter * jnp.dot(qc, C.astype(jnp.bfloat16), preferred_element_type=jnp.float32))
        den = jnp.sum(s, axis=1, keepdims=True) + s_inter * jnp.sum(qf * n_row, axis=1, keepdims=True)
        h = num / jnp.maximum(jnp.abs(den), jnp.exp(-m_t))

        ms = jnp.mean(h * h, axis=-1, keepdims=True)
        hn = h * lax.rsqrt(ms + EPS) * gain
        og = og_ref[pl.ds(off, L), hd * dv:(hd + 1) * dv]
        o_ref[pl.ds(off, L), hd * dv:(hd + 1) * dv] = (jax.nn.sigmoid(og) * hn).astype(o_ref.dtype)

        m_new = jnp.maximum(a + m, m_loc)
        s_old = jnp.exp(a + m - m_new)
        s_new = jnp.exp(m_loc - m_new)
        return (s_old * C + s_new * c_loc, s_old * n_row + s_new * n_loc, m_new)

    init = (jnp.zeros((dk, dv), jnp.float32), jnp.zeros((1, dk), jnp.float32), jnp.zeros((1, 1), jnp.float32))
    lax.fori_loop(0, nc, chunk, (init,) * heads, unroll=ML_CHUNK_UNROLL)


def _mlstm(qk, v, og, gates_t, h_gain, batch, seq):
    n = qk.shape[0]
    L = ML_CHUNK
    nc = seq // L
    hb = ML_HEADS_PER_STEP
    nhb = ML_HEADS // hb
    g4 = gates_t.reshape(2 * ML_HEADS, batch * nc, 1, L)
    return pl.pallas_call(
        _mlstm_kernel,
        grid=(batch, nhb),
        in_specs=[pl.BlockSpec((seq, hb * ML_QK_DIM), lambda b, h: (b, h)),
                  pl.BlockSpec((seq, hb * ML_QK_DIM), lambda b, h: (b, nhb + h)),
                  pl.BlockSpec((seq, hb * ML_V_DIM), lambda b, h: (b, h)),
                  pl.BlockSpec((seq, hb * ML_V_DIM), lambda b, h: (b, h)),
                  pl.BlockSpec((hb, nc, 1, L), lambda b, h: (h, b, 0, 0)),
                  pl.BlockSpec((hb, nc, 1, L), lambda b, h: (nhb + h, b, 0, 0)),
                  pl.BlockSpec((1, ML_V_DIM), lambda b, h: (0, 0))],
        out_specs=pl.BlockSpec((seq, hb * ML_V_DIM), lambda b, h: (b, h)),
        out_shape=jax.ShapeDtypeStruct((n, ML_HEADS * ML_V_DIM), jnp.bfloat16),
        compiler_params=_params("parallel", "parallel"),
        name="mlstm",
    )(qk, qk, v, og, g4, g4, h_gain.reshape(1, ML_V_DIM))


def _outproj_kernel(a_ref, w_ref, x_ref, o_ref):
    o_ref[...] = x_ref[...] + jnp.dot(a_ref[...], w_ref[...], preferred_element_type=jnp.float32)


def _outproj(a, w, x, tm=512):
    n, d = x.shape
    k = a.shape[1]
    return pl.pallas_call(
        _outproj_kernel,
        grid=(n // tm,),
        in_specs=[pl.BlockSpec((tm, k), lambda i: (i, 0)),
                  pl.BlockSpec((k, d), lambda i: (0, 0)),
                  pl.BlockSpec((tm, d), lambda i: (i, 0))],
        out_specs=pl.BlockSpec((tm, d), lambda i: (i, 0)),
        out_shape=jax.ShapeDtypeStruct((n, d), jnp.float32),
        compiler_params=_params("parallel"),
        name="outproj",
    )(a, w, x)


def _ffn_step(first_chunk, x_ref, g_ref, wu, wd, o_ref, hn_ref):
    @pl.when(first_chunk)
    def _():
        x = x_ref[...]
        ms = jnp.mean(x * x, axis=-1, keepdims=True)
        hn_ref[...] = (x * lax.rsqrt(ms + EPS) * g_ref[...]).astype(hn_ref.dtype)
        o_ref[...] = x

    acts = []
    for s in range(wu.shape[1] // FFN_SUB):
        u = jnp.dot(hn_ref[...], wu[:, s * FFN_SUB:(s + 1) * FFN_SUB], preferred_element_type=jnp.float32)
        acts.append(jnp.square(jnp.maximum(u, 0.0)).astype(jnp.bfloat16))
    o_ref[...] += jnp.dot(jnp.concatenate(acts, axis=1), wd[...], preferred_element_type=jnp.float32)


def _ffn_first_tile_kernel(x_ref, g_ref, wu32_ref, wd32_ref, o_ref, wu16_ref, wd16_ref, hn_ref):
    wu16_ref[...] = wu32_ref[...].astype(wu16_ref.dtype)
    wd16_ref[...] = wd32_ref[...].astype(wd16_ref.dtype)
    _ffn_step(pl.program_id(0) == 0, x_ref, g_ref, wu16_ref, wd16_ref, o_ref, hn_ref)


def _ffn_kernel(x_ref, g_ref, wu_ref, wd_ref, *rest, cast_steps, first_done):
    n_cast = len(cast_steps)
    cast_src, o_ref, cast_dst, hn_ref = rest[:n_cast], rest[n_cast], rest[n_cast + 1:-1], rest[-1]
    i, j = pl.program_id(0), pl.program_id(1)
    step = i * pl.num_programs(1) + j
    for src, dst, steps in zip(cast_src, cast_dst, cast_steps):
        @pl.when(step < steps)
        def _(src=src, dst=dst):
            slab = src[...] if src.shape == dst.shape else src[...].T
            dst[...] = slab.astype(dst.dtype)

    if first_done:
        @pl.when(step == 0)
        def _():
            o_ref[...] = x_ref[...]

        @pl.when(i > 0)
        def _():
            _ffn_step(j == 0, x_ref, g_ref, wu_ref, wd_ref, o_ref, hn_ref)
    else:
        _ffn_step(j == 0, x_ref, g_ref, wu_ref, wd_ref, o_ref, hn_ref)


FFN_ROWS, FFN_CHUNK = 1024, 512


def _ffn_first_tile(x, g, w_up32, w_down32, layer):
    n, d = x.shape
    tm, tf = min(FFN_ROWS, n), FFN_CHUNK
    f = w_up32.shape[2]
    return pl.pallas_call(
        _ffn_first_tile_kernel,
        grid=(f // tf,),
        in_specs=[pl.BlockSpec((tm, d), lambda j: (0, 0), pipeline_mode=pl.Buffered(1)),
                  pl.BlockSpec((1, d), lambda j: (0, 0)),
                  pl.BlockSpec((None, d, tf), lambda j: (layer, 0, j)),
                  pl.BlockSpec((None, tf, d), lambda j: (layer, j, 0))],
        out_specs=[pl.BlockSpec((tm, d), lambda j: (0, 0)),
                   pl.BlockSpec((d, tf), lambda j: (0, j)),
                   pl.BlockSpec((tf, d), lambda j: (j, 0))],
        out_shape=[jax.ShapeDtypeStruct((tm, d), jnp.float32), jax.ShapeDtypeStruct((d, f), jnp.bfloat16),
                   jax.ShapeDtypeStruct((f, d), jnp.bfloat16)],
        scratch_shapes=[pltpu.VMEM((tm, d), jnp.bfloat16)],
        compiler_params=_params("arbitrary"),
        name="ffn_first_tile",
    )(x, g.reshape(1, d), w_up32, w_down32)


def _ffn(x, g, w_up, w_down, cast=(), first_done=False):
    n, d = x.shape
    tm, tf = min(FFN_ROWS, n), FFN_CHUNK
    f = w_up.shape[1]
    ni, nj = n // tm, f // tf
    chunk = (lambda i, j: jnp.where(i > 0, j, 0)) if first_done else (lambda i, j: j)
    in_specs, out_specs, out_shapes, steps = [], [], [], []
    for a, layer, transposed in cast:
        rows, cols = a.shape[1:]
        if transposed:
            tiles = rows // LANES
            width = LANES * (-(-tiles // (ni * nj)))
            used = tiles * LANES // width
            assert used * width == tiles * LANES
            last = used - 1
            in_specs.append(pl.BlockSpec((None, width, cols),
                                         lambda i, j, layer=layer, last=last: (layer, jnp.minimum(i * nj + j, last), 0)))
            out_specs.append(pl.BlockSpec((cols, width), lambda i, j, last=last: (0, jnp.minimum(i * nj + j, last))))
            out_shapes.append(jax.ShapeDtypeStruct((cols, tiles * LANES), jnp.bfloat16))
        else:
            used = ni * nj
            in_specs.append(pl.BlockSpec((None, rows // used, cols), lambda i, j, layer=layer: (layer, i * nj + j, 0)))
            out_specs.append(pl.BlockSpec((rows // used, cols), lambda i, j: (i * nj + j, 0)))
            out_shapes.append(jax.ShapeDtypeStruct((rows, cols), jnp.bfloat16))
        steps.append(used)
    out = pl.pallas_call(
        functools.partial(_ffn_kernel, cast_steps=tuple(steps), first_done=first_done),
        grid=(ni, nj),
        in_specs=[pl.BlockSpec((tm, d), lambda i, j: (i, 0)),
                  pl.BlockSpec((1, d), lambda i, j: (0, 0)),
                  pl.BlockSpec((d, tf), lambda i, j: (0, chunk(i, j))),
                  pl.BlockSpec((tf, d), lambda i, j: (chunk(i, j), 0))] + in_specs,
        out_specs=[pl.BlockSpec((tm, d), lambda i, j: (i, 0))] + out_specs,
        out_shape=[jax.ShapeDtypeStruct((n, d), jnp.float32)] + out_shapes,
        scratch_shapes=[pltpu.VMEM((tm, d), jnp.bfloat16)],
        compiler_params=_params("arbitrary", "arbitrary"),
        name="ffn",
    )(x, g.reshape(1, d), w_up, w_down, *[a for a, _, _ in cast])
    return out[0], out[1:]


def _attention_layer(h, batch, seq, g_mix, w, w_tail_t, q_gain, k_gain, w_out):
    bf = jnp.bfloat16
    n, d = h.shape
    n_qk = (ATT_HEADS + ATT_KV_HEADS) * ATT_HEAD_DIM
    n_v = ATT_KV_HEADS * ATT_HEAD_DIM
    n_qi = IDX_HEADS * IDX_HEAD_DIM
    tm = min(512, seq)

    c, s1, s2, half = _rope_tables(seq, ATT_HEAD_DIM, 1)
    ci, s1i, s2i, half_i = _rope_tables(seq, IDX_HEAD_DIM, LANES // IDX_HEAD_DIM)
    lane = jnp.arange(LANES)
    is_k = (lane < IDX_HEAD_DIM)[None, :]
    is_w = jnp.logical_and(lane >= IDX_HEAD_DIM, lane < IDX_HEAD_DIM + IDX_HEADS)[None, :]
    w_scale = IDX_HEADS ** -0.5 * IDX_HEAD_DIM ** -0.5
    tab = jnp.stack([c, s1, s2])
    tab_i = jnp.stack([ci, s1i, s2i])
    tab_t = jnp.stack([jnp.where(is_k, ci, jnp.where(is_w, w_scale, 0.0)),
                       jnp.where(is_k, s1i, 0.0), jnp.where(is_k, s2i, 0.0)])
    gains = jnp.stack([q_gain, k_gain]).reshape(2, 1, ATT_HEAD_DIM)
    n_main = n_qk + n_v + n_qi
    w_tail = jnp.pad(w_tail_t.T.astype(bf), ((0, 0), (0, LANES - w_tail_t.shape[0])))

    row = lambda width: pl.BlockSpec((tm, width), lambda i: (i, 0))
    whole = lambda a: pl.BlockSpec(a.shape, lambda i: (0,) * a.ndim, pipeline_mode=pl.Buffered(1))
    tbl = pl.BlockSpec((3, tm, LANES), lambda i: (0, i % (seq // tm), 0))
    qk, v, qi, kw = pl.pallas_call(
        functools.partial(_att_proj_kernel, half=half, half_i=half_i),
        grid=(n // tm,),
        in_specs=[row(d), whole(g_mix.reshape(1, d)),
                  pl.BlockSpec((d, n_main), lambda i: (0, 0), pipeline_mode=pl.Buffered(1)), whole(w_tail),
                  whole(gains), tbl, tbl, tbl],
        out_specs=[row(n_qk), row(n_v), row(n_qi), row(LANES)],
        out_shape=[jax.ShapeDtypeStruct((n, n_qk), bf), jax.ShapeDtypeStruct((n, n_v), bf),
                   jax.ShapeDtypeStruct((n, n_qi), bf), jax.ShapeDtypeStruct((n, LANES), jnp.float32)],
        scratch_shapes=[pltpu.VMEM((tm, d), bf)],
        compiler_params=_params("parallel"),
        name="att_proj",
    )(h, g_mix.reshape(1, d), w, w_tail, gains, tab, tab_i, tab_t)

    return _attention(qk, v, qi, kw, h, w_out.astype(bf), batch, seq)


def _mlstm_layer(h, batch, seq, g_mix, w, w_tail_t, b_gate, h_gain, w_out):
    bf = jnp.bfloat16
    n, d = h.shape
    n_qk, n_v = 2 * ML_HEADS * ML_QK_DIM, ML_HEADS * ML_V_DIM
    n_g = 2 * ML_HEADS
    tm = min(512, seq)
    g2 = g_mix.reshape(1, d)
    wg_t = w_tail_t.astype(bf)
    assert (n_qk + n_v) % d == 0

    row = lambda width: pl.BlockSpec((tm, width), lambda i: (i, 0))
    whole = lambda a: pl.BlockSpec(a.shape, lambda i: (0,) * a.ndim, pipeline_mode=pl.Buffered(1))
    wcols = lambda width, blk: pl.BlockSpec((d, width), lambda i: (0, blk), pipeline_mode=pl.Buffered(1))
    qk, v, gates_t = pl.pallas_call(
        _ml_proj_kernel,
        grid=(n // tm,),
        in_specs=[row(d), whole(g2), wcols(n_qk + n_v, 0), whole(wg_t), pl.BlockSpec((n_g, 1), lambda i: (0, 0))],
        out_specs=[row(n_qk), row(n_v), pl.BlockSpec((n_g, tm), lambda i: (0, i))],
        out_shape=[jax.ShapeDtypeStruct((n, n_qk), bf), jax.ShapeDtypeStruct((n, n_v), bf),
                   jax.ShapeDtypeStruct((n_g, n), jnp.float32)],
        scratch_shapes=[pltpu.VMEM((tm, d), bf)],
        compiler_params=_params("parallel"),
        name="ml_proj",
    )(h, g2, w, wg_t, b_gate.reshape(n_g, 1))
    og = pl.pallas_call(
        _norm_proj_kernel,
        grid=(n // tm,),
        in_specs=[row(d), whole(g2), wcols(d, (n_qk + n_v) // d)],
        out_specs=row(d),
        out_shape=jax.ShapeDtypeStruct((n, d), jnp.float32),
        scratch_shapes=[pltpu.VMEM((tm, d), bf)],
        compiler_params=_params("parallel"),
        name="ml_o_proj",
    )(h, g2, w)
    y = _mlstm(qk, v, og, gates_t, h_gain, batch, seq)
    return _outproj(y, w_out.astype(bf), h)


def kernel(x, norm_mix, norm_ffn, att_w_in, att_q_gain, att_k_gain, att_w_out, ml_w_in, ml_b_gate, ml_h_gain,
           ml_w_out, ffn_w_up, ffn_w_down):
    batch, seq, d = x.shape
    bf = jnp.bfloat16
    h = x.reshape(batch * seq, d)
    depth = norm_mix.shape[0]
    att_w_in_t, ml_w_in_t = jnp.swapaxes(att_w_in, 1, 2), jnp.swapaxes(ml_w_in, 1, 2)
    tail_t = lambda w_t, idx: w_t[idx, w_t.shape[1] // LANES * LANES:, :]
    ready = {"w_in": att_w_in[0].astype(bf), "w_out": att_w_out[0].astype(bf)}
    for i in range(depth):
        j = i // 2
        if i % 2 == 0:
            h = _attention_layer(h, batch, seq, norm_mix[i], ready["w_in"], tail_t(att_w_in_t, j), att_q_gain[j],
                                 att_k_gain[j], ready["w_out"])
        else:
            h = _mlstm_layer(h, batch, seq, norm_mix[i], ready["w_in"], tail_t(ml_w_in_t, j), ml_b_gate[j],
                             ml_h_gain[j], ready["w_out"])
        nxt = i + 1
        cast = []
        if nxt < depth:
            w_in_t, w_out = (att_w_in_t, att_w_out) if nxt % 2 == 0 else (ml_w_in_t, ml_w_out)
            cast = [(ffn_w_up, nxt, False), (ffn_w_down, nxt, False), (w_in_t, nxt // 2, True),
                    (w_out, nxt // 2, False)]
        if "w_up" in ready:
            h, done = _ffn(h, norm_ffn[i], ready["w_up"], ready["w_down"], cast)
        else:
            first, w_up, w_down = _ffn_first_tile(h, norm_ffn[i], ffn_w_up, ffn_w_down, i)
            h = lax.dynamic_update_slice(h, first, (0, 0))
            h, done = _ffn(h, norm_ffn[i], w_up, w_down, cast, first_done=True)
        if done:
            ready = dict(zip(("w_up", "w_down", "w_in", "w_out"), done))
    return h.reshape(batch, seq, d)
```

```python
import functools

import jax
import jax.numpy as jnp
from jax import lax
from jax.experimental import pallas as pl
from jax.experimental.pallas import tpu as pltpu

D_MODEL = 2048
ATT_HEADS = 16
ATT_KV_HEADS = 4
ATT_HEAD_DIM = 128
IDX_HEADS = 16
IDX_HEAD_DIM = 64
TOPK_MAX = 256
ML_HEADS = 8
ML_V_DIM = 256
ML_QK_DIM = 128
GATE_SOFTCAP = 15.0
D_FF = 4 * D_MODEL
ROPE_THETA = 500000.0
ROT_FRAC = 4
EPS = 1e-6

LANES = 128
SUBLANES = 8
VMEM_LIMIT = 60 * 1024 * 1024
INT_MIN = -(2 ** 31)
NEG_BIG = -1e30

LOG2_E = 1.4426950408889634

ATT_Q_BLOCK = 256
ATT_KEY_CHUNK = 256
SEARCH_UNROLL = 4
SEARCH_TRIPS = 32 // SEARCH_UNROLL
ML_CHUNK = 256
ML_HEADS_PER_STEP = 4
ML_CHUNK_UNROLL = 2
PROJ_ROWS = 512
PROJ_SUB = 512
FFN_SUB = 256


def _params(*sem):
    return pltpu.CompilerParams(dimension_semantics=sem, vmem_limit_bytes=VMEM_LIMIT)


def _store_rmsnorm(x_ref, g_ref, hn_ref):
    x = x_ref[...]
    ms = jnp.mean(x * x, axis=-1, keepdims=True)
    hn_ref[...] = (x * lax.rsqrt(ms + EPS) * g_ref[...]).astype(hn_ref.dtype)

def _rope_tables(seq, head_dim, heads_per_vreg):
    rot = head_dim // ROT_FRAC
    half = rot // 2
    inv_freq = ROPE_THETA ** (-2.0 * jnp.arange(half, dtype=jnp.float32) / rot)
    ang = jnp.arange(seq).astype(jnp.float32)[:, None] * inv_freq[None, :]
    cos, sin = jnp.cos(ang), jnp.sin(ang)
    ones = jnp.ones((seq, head_dim - rot), jnp.float32)
    zeros_h = jnp.zeros((seq, half), jnp.float32)
    zeros_r = jnp.zeros((seq, head_dim - rot), jnp.float32)
    c = jnp.concatenate([cos, cos, ones], axis=1)
    s1 = jnp.concatenate([-sin, zeros_h, zeros_r], axis=1)
    s2 = jnp.concatenate([zeros_h, sin, zeros_r], axis=1)
    rep = lambda t: jnp.tile(t, (1, heads_per_vreg))
    return rep(c), rep(s1), rep(s2), half


def _rope(y, c, s1, s2, half):
    return y * c + pltpu.roll(y, LANES - half, 1) * s1 + pltpu.roll(y, half, 1) * s2


def _att_proj_kernel(x_ref, g_ref, w_ref, wt_ref, gain_ref, tab_ref, tabi_ref, tabt_ref,
                     qk_ref, v_ref, qi_ref, kw_ref, hn_ref, *, half, half_i):
    _store_rmsnorm(x_ref, g_ref, hn_ref)
    n_qk, n_v, n_qi = qk_ref.shape[1], v_ref.shape[1], qi_ref.shape[1]
    n_q = ATT_HEADS * ATT_HEAD_DIM

    def cols(start, width):
        return jnp.dot(hn_ref[...], w_ref[:, start:start + width], preferred_element_type=jnp.float32)

    for s in range(n_qk // PROJ_SUB):
        acc = cols(s * PROJ_SUB, PROJ_SUB)
        gain = gain_ref[0 if s * PROJ_SUB < n_q else 1]
        for grp in range(PROJ_SUB // LANES):
            xg = acc[:, grp * LANES:(grp + 1) * LANES]
            ms = jnp.mean(xg * xg, axis=-1, keepdims=True)
            y = _rope(xg * lax.rsqrt(ms + EPS) * gain, tab_ref[0], tab_ref[1], tab_ref[2], half)
            qk_ref[:, s * PROJ_SUB + grp * LANES:s * PROJ_SUB + (grp + 1) * LANES] = y.astype(qk_ref.dtype)

    v_ref[...] = cols(n_qk, n_v).astype(v_ref.dtype)

    for s in range(n_qi // PROJ_SUB):
        acc = cols(n_qk + n_v + s * PROJ_SUB, PROJ_SUB)
        for grp in range(PROJ_SUB // LANES):
            y = _rope(acc[:, grp * LANES:(grp + 1) * LANES], tabi_ref[0], tabi_ref[1], tabi_ref[2], half_i)
            qi_ref[:, s * PROJ_SUB + grp * LANES:s * PROJ_SUB + (grp + 1) * LANES] = y.astype(qi_ref.dtype)

    tail = jnp.dot(hn_ref[...], wt_ref[...], preferred_element_type=jnp.float32)
    kw_ref[...] = _rope(tail, tabt_ref[0], tabt_ref[1], tabt_ref[2], half_i)


def _ml_proj_kernel(x_ref, g_ref, w_ref, wg_ref, b_ref, qk_ref, v_ref, og_ref, gates_ref, hn_ref):
    _store_rmsnorm(x_ref, g_ref, hn_ref)
    n_qk, n_v = qk_ref.shape[1], v_ref.shape[1]
    for s in range((n_qk + n_v + og_ref.shape[1]) // PROJ_SUB):
        acc = jnp.dot(hn_ref[...], w_ref[:, s * PROJ_SUB:(s + 1) * PROJ_SUB], preferred_element_type=jnp.float32)
        start = s * PROJ_SUB
        if start < n_qk // 2:
            qk_ref[:, start:start + PROJ_SUB] = acc.astype(qk_ref.dtype)
        elif start < n_qk:
            qk_ref[:, start:start + PROJ_SUB] = (acc * ML_QK_DIM ** -0.5).astype(qk_ref.dtype)
        elif start < n_qk + n_v:
            v_ref[:, start - n_qk:start - n_qk + PROJ_SUB] = acc.astype(v_ref.dtype)
        else:
            og_ref[:, start - n_qk - n_v:start - n_qk - n_v + PROJ_SUB] = acc

    g = lax.dot_general(wg_ref[...], hn_ref[...], (((1,), (1,)), ((), ())),
                        preferred_element_type=jnp.float32)
    g = g + b_ref[...]
    g = GATE_SOFTCAP * jnp.tanh(g / GATE_SOFTCAP)
    logf = jnp.minimum(g, 0.0) - jnp.log1p(jnp.exp(-jnp.abs(g)))
    is_forget = lax.broadcasted_iota(jnp.int32, g.shape, 0) >= ML_HEADS
    gates_ref[...] = jnp.where(is_forget, logf, g)


def _key_to_f32(key):
    return pltpu.bitcast(jnp.where(key < 0, key ^ jnp.int32(0x7FFFFFFF), key), jnp.float32)


def _select_topk(score_ref, bias_ref, *, topk, n_chunks, side_work):
    _, tq, ck = score_ref.shape
    kf = jnp.float32(topk)

    def count(pred, thr):
        part = jnp.zeros((tq, LANES), jnp.float32)
        for c in range(n_chunks):
            for j in range(ck // LANES):
                part = part + jnp.where(pred(score_ref[c, :, j * LANES:(j + 1) * LANES], thr), 1.0, 0.0)
        return jnp.sum(part, axis=1, keepdims=True)

    key_ninf = INT_MIN + 0x7FFFFF
    key_pinf = 0x7F800000

    def search(it, tau):
        cand = tau + lax.shift_left(jnp.int32(1), jnp.int32(31) - it)
        in_range = jnp.logical_and(cand > tau, cand <= key_pinf)
        cnt = count(jnp.greater_equal, _key_to_f32(cand))
        return jnp.where(jnp.logical_and(in_range, cnt >= kf), cand, tau)

    def trip(t, tau):
        for u in range(SEARCH_UNROLL):
            tau = search(t * SEARCH_UNROLL + u, tau)
        side_work(t)
        return tau

    tau = lax.fori_loop(0, SEARCH_TRIPS, trip, jnp.full((tq, 1), key_ninf, jnp.int32))
    thr = _key_to_f32(tau)
    finite = lambda s, th: jnp.logical_and(s >= th, s > -jnp.inf)
    n_ge = count(finite, thr)

    @pl.when(jnp.max(n_ge) <= kf)
    def _():
        for c in range(n_chunks):
            bias_ref[c] = jnp.where(finite(score_ref[c], thr), 0.0, NEG_BIG)

    @pl.when(jnp.max(n_ge) > kf)
    def _():
        need = kf - count(jnp.greater, thr)
        r = lax.broadcasted_iota(jnp.int32, (ck, ck), 0)
        col = lax.broadcasted_iota(jnp.int32, (ck, ck), 1)
        before = jnp.where(r < col, 1.0, 0.0).astype(jnp.bfloat16)
        seen = jnp.zeros((tq, 1), jnp.float32)
        for c in range(n_chunks):
            s = score_ref[c]
            eq = jnp.logical_and(s == thr, s > -jnp.inf)
            e = jnp.where(eq, 1.0, 0.0)
            prefix = jnp.dot(e.astype(jnp.bfloat16), before, preferred_element_type=jnp.float32) + seen
            seen = seen + jnp.sum(e, axis=1, keepdims=True)
            keep = jnp.logical_or(s > thr, jnp.logical_and(eq, prefix < need))
            bias_ref[c] = jnp.where(keep, 0.0, NEG_BIG)


def _for_each_chunk(n, body):
    def pair(p, carry):
        body(2 * p)
        body(2 * p + 1)
        return carry

    lax.fori_loop(0, n // 2, pair, 0)
    pl.when(n % 2 == 1)(lambda: body(n - 1))


def _attention_kernel(q_ref, k_ref, v_ref, qi_ref, wq_ref, kk_ref, x_ref, wo_ref, h_ref, *scratch, topk):
    o_ref, proj_ref = scratch[-2:]
    i = pl.program_id(1)
    last = pl.num_programs(1) - 1
    sub = wo_ref.shape[2]
    assert wo_ref.shape[0] == SEARCH_TRIPS

    @pl.when(jnp.logical_and(pl.program_id(0) == 0, i == 0))
    def _():
        o_ref[...] = jnp.zeros(o_ref.shape, o_ref.dtype)

    def project(t):
        proj_ref[t] = jnp.dot(o_ref[...], wo_ref[t], preferred_element_type=jnp.float32)

    def finish_previous():
        for t in range(wo_ref.shape[0]):
            h_ref[:, t * sub:(t + 1) * sub] = x_ref[:, t * sub:(t + 1) * sub] + proj_ref[t]

    @pl.when(i < last)
    def _():
        _attention_block(q_ref, k_ref, v_ref, qi_ref, wq_ref, kk_ref, *scratch[:-1], topk=topk, q0=i * q_ref.shape[0],
                         project=project, finish_previous=finish_previous)

    @pl.when(i == last)
    def _():
        for t in range(wo_ref.shape[0]):
            project(t)
        finish_previous()


def _attention_block(q_ref, k_ref, v_ref, qi_ref, wq_ref, kk_ref,
                     qs_ref, wb_ref, score_ref, bias_ref, q4_ref, s_ref, m_ref, acc_ref, o_ref, *,
                     topk, q0, project, finish_previous):
    tq = q_ref.shape[0]
    ck = ATT_KEY_CHUNK
    half = tq // 2
    nt = (((1,), (1,)), ((), ()))
    n_chunks = (q0 + tq) // ck

    for r in range(2):
        rows = slice(r * half, (r + 1) * half)
        for h in range(IDX_HEADS):
            qs_ref[r, h * half:(h + 1) * half, :] = (
                qi_ref[rows, h * IDX_HEAD_DIM:(h + 1) * IDX_HEAD_DIM].astype(jnp.bfloat16))
            wb_ref[r * IDX_HEADS + h] = jnp.broadcast_to(
                wq_ref[rows, IDX_HEAD_DIM + h:IDX_HEAD_DIM + h + 1], (half, LANES))

    def score_chunk(c):
        off = pl.multiple_of(c * ck, ck)
        ki = kk_ref[pl.ds(off, ck), :].astype(jnp.bfloat16)[:, :IDX_HEAD_DIM]
        spos = off + lax.broadcasted_iota(jnp.int32, (half, LANES), 1)
        for r in range(2):
            d = lax.dot_general(qs_ref[r], ki, nt, preferred_element_type=jnp.float32)
            tpos = q0 + r * half + lax.broadcasted_iota(jnp.int32, (half, LANES), 0)
            for j in range(ck // LANES):
                acc = jnp.zeros((half, LANES), jnp.float32)
                for h in range(IDX_HEADS):
                    dh = d[h * half:(h + 1) * half, j * LANES:(j + 1) * LANES]
                    acc = acc + jnp.maximum(dh, 0.0) * wb_ref[r * IDX_HEADS + h]
                score_ref[c, r * half:(r + 1) * half, j * LANES:(j + 1) * LANES] = (
                    jnp.where(spos + j * LANES <= tpos, acc, -jnp.inf))

    _for_each_chunk(n_chunks, score_chunk)

    for n in range(1, score_ref.shape[0] + 1):
        pl.when(n_chunks == n)(functools.partial(_select_topk, score_ref, bias_ref, topk=topk, n_chunks=n,
                                                 side_work=project))
    finish_previous()

    exp2_scale = ATT_HEAD_DIM ** -0.5 * LOG2_E
    rep = ATT_HEADS // ATT_KV_HEADS
    pair = s_ref.shape[0]
    head_cols = lambda h: slice(h * ATT_HEAD_DIM, (h + 1) * ATT_HEAD_DIM)
    for g0 in range(0, ATT_KV_HEADS, pair):
        for u in range(pair):
            for r in range(rep):
                q4_ref[u, r * tq:(r + 1) * tq, :] = q_ref[:, head_cols((g0 + u) * rep + r)]
        m_ref[...] = jnp.full(m_ref.shape, NEG_BIG, jnp.float32)

        def logits_chunk(c):
            off = pl.multiple_of(c * ck, ck)
            for u in range(pair):
                s = lax.dot_general(q4_ref[u], k_ref[pl.ds(off, ck), head_cols(g0 + u)], nt,
                                    preferred_element_type=jnp.float32)
                s = ((s.reshape(rep, tq, ck) + bias_ref[c][None]) * exp2_scale).reshape(rep * tq, ck)
                s_ref[u, c] = s
                m = m_ref[u]
                for j in range(ck // LANES):
                    m = jnp.maximum(m, s[:, j * LANES:(j + 1) * LANES])
                m_ref[u] = m

        _for_each_chunk(n_chunks, logits_chunk)
        for u in range(pair):
            m_ref[u] = jnp.broadcast_to(jnp.max(m_ref[u], axis=1, keepdims=True), m_ref.shape[1:])
        acc_ref[...] = jnp.zeros(acc_ref.shape, jnp.float32)
        ones = jnp.ones((ck, LANES), jnp.bfloat16)

        def pv_chunk(c):
            off = pl.multiple_of(c * ck, ck)
            for u in range(pair):
                p = jnp.exp2(s_ref[u, c] - jnp.concatenate([m_ref[u]] * (ck // LANES), axis=1))
                v1 = jnp.concatenate([v_ref[pl.ds(off, ck), head_cols(g0 + u)], ones], axis=1)
                acc_ref[u] += jnp.dot(p.astype(jnp.bfloat16), v1, preferred_element_type=jnp.float32)

        _for_each_chunk(n_chunks, pv_chunk)
        for u in range(pair):
            o4 = acc_ref[u, :, :ATT_HEAD_DIM] / acc_ref[u, :, ATT_HEAD_DIM:]
            for r in range(rep):
                o_ref[:, head_cols((g0 + u) * rep + r)] = o4[r * tq:(r + 1) * tq].astype(o_ref.dtype)


def _attention(qk, v, qi, kw, x, w_out, batch, seq):
    n, d = x.shape
    tq, ck = ATT_Q_BLOCK, ATT_KEY_CHUNK
    nq = seq // tq
    rep = ATT_HEADS // ATT_KV_HEADS
    pair = 2
    topk = min(TOPK_MAX, seq // 4)
    q_w, kv_w = ATT_HEADS * ATT_HEAD_DIM, ATT_KV_HEADS * ATT_HEAD_DIM
    qi_w = IDX_HEADS * IDX_HEAD_DIM
    sub = d // SEARCH_TRIPS
    wo = w_out.reshape(q_w, SEARCH_TRIPS, sub).transpose(1, 0, 2)
    cur = lambda b, i: (b * nq + jnp.minimum(i, nq - 1), 0)
    prev = lambda b, i: (b * nq + jnp.maximum(i - 1, 0), 0)
    once = dict(pipeline_mode=pl.Buffered(1))
    return pl.pallas_call(
        functools.partial(_attention_kernel, topk=topk),
        grid=(batch, nq + 1),
        in_specs=[pl.BlockSpec((tq, q_w), cur),
                  pl.BlockSpec((seq, kv_w), lambda b, i: (b, q_w // kv_w), **once),
                  pl.BlockSpec((seq, kv_w), lambda b, i: (b, 0), **once),
                  pl.BlockSpec((tq, qi_w), cur),
                  pl.BlockSpec((tq, LANES), cur),
                  pl.BlockSpec((seq, LANES), lambda b, i: (b, 0), **once),
                  pl.BlockSpec((tq, d), prev),
                  pl.BlockSpec(wo.shape, lambda b, i: (0, 0, 0), **once)],
        out_specs=pl.BlockSpec((tq, d), prev),
        out_shape=jax.ShapeDtypeStruct((n, d), jnp.float32),
        scratch_shapes=[pltpu.VMEM((2, IDX_HEADS * tq // 2, IDX_HEAD_DIM), jnp.bfloat16),
                        pltpu.VMEM((2 * IDX_HEADS, tq // 2, LANES), jnp.float32),
                        pltpu.VMEM((seq // ck, tq, ck), jnp.float32),
                        pltpu.VMEM((seq // ck, tq, ck), jnp.float32),
                        pltpu.VMEM((pair, rep * tq, ATT_HEAD_DIM), jnp.bfloat16),
                        pltpu.VMEM((pair, seq // ck, rep * tq, ck), jnp.float32),
                        pltpu.VMEM((pair, rep * tq, LANES), jnp.float32),
                        pltpu.VMEM((pair, rep * tq, ATT_HEAD_DIM + LANES), jnp.float32),
                        pltpu.VMEM((tq, q_w), jnp.bfloat16),
                        pltpu.VMEM((SEARCH_TRIPS, tq, sub), jnp.float32)],
        compiler_params=_params("arbitrary", "arbitrary"),
        name="dsa_attention",
    )(qk, qk, v, qi, kw, kw, x, wo)


def _mlstm_kernel(q_ref, k_ref, v_ref, og_ref, ig_ref, lf_ref, hg_ref, o_ref):
    L = ML_CHUNK
    nc = q_ref.shape[0] // L
    dk, dv = ML_QK_DIM, ML_V_DIM
    heads = q_ref.shape[1] // dk
    row = lax.broadcasted_iota(jnp.int32, (L, L), 0)
    col = lax.broadcasted_iota(jnp.int32, (L, L), 1)
    tri = col <= row
    eye = col == row
    gain = hg_ref[...]

    def to_col(x_row):
        return jnp.sum(jnp.where(eye, jnp.broadcast_to(x_row, (L, L)), 0.0), axis=1, keepdims=True)

    def chunk(c, carry):
        return tuple(head_chunk(hd, c, carry[hd]) for hd in range(heads))

    def head_chunk(hd, c, state):
        C, n_row, m = state
        off = pl.multiple_of(c * L, L)
        qc = q_ref[pl.ds(off, L), hd * dk:(hd + 1) * dk]
        kc = k_ref[pl.ds(off, L), hd * dk:(hd + 1) * dk]
        vc = v_ref[pl.ds(off, L), hd * dv:(hd + 1) * dv]
        ig_r = ig_ref[hd, c]
        lf_r = lf_ref[hd, c]

        lf_b = jnp.broadcast_to(lf_r, (L, L))
        b_col = jnp.sum(jnp.where(tri, lf_b, 0.0), axis=1, keepdims=True)
        lf_col = to_col(lf_r)
        b_row = jnp.sum(jnp.where(row <= col, jnp.broadcast_to(lf_col, (L, L)), 0.0), axis=0, keepdims=True)
        a = jnp.sum(lf_r, axis=1, keepdims=True)

        g_row = a - b_row + ig_r
        m_loc = jnp.max(g_row, axis=1, keepdims=True)
        w_row = jnp.exp(g_row - m_loc)
        w_col = to_col(w_row)
        c_loc = lax.dot_general(kc, (w_col * vc.astype(jnp.float32)).astype(jnp.bfloat16),
                                (((0,), (0,)), ((), ())), preferred_element_type=jnp.float32)
        n_loc = jnp.dot(jnp.broadcast_to(w_row, (SUBLANES, L)).astype(jnp.bfloat16), kc,
                        preferred_element_type=jnp.float32)[0:1]

        dm = jnp.where(tri, b_col - b_row + ig_r, -jnp.inf)
        inter = b_col + m
        m_t = jnp.maximum(inter, jnp.max(dm, axis=1, keepdims=True))
        s = lax.dot_general(qc, kc, (((1,), (1,)), ((), ())), preferred_element_type=jnp.float32)
        s = s * jnp.exp(dm - m_t)
        s_inter = jnp.exp(inter - m_t)
        qf = qc.astype(jnp.float32)
        num = (jnp.dot(s.astype(jnp.bfloat16), vc, preferred_element_type=jnp.float32)
               + s_inter * jnp.dot(qc, C.astype(jnp.bfloat16), preferred_element_type=jnp.float32))
        den = jnp.sum(s, axis=1, keepdims=True) + s_inter * jnp.sum(qf * n_row, axis=1, keepdims=True)
        h = num / jnp.maximum(jnp.abs(den), jnp.exp(-m_t))

        ms = jnp.mean(h * h, axis=-1, keepdims=True)
        hn = h * lax.rsqrt(ms + EPS) * gain
        og = og_ref[pl.ds(off, L), hd * dv:(hd + 1) * dv]
        o_ref[pl.ds(off, L), hd * dv:(hd + 1) * dv] = (jax.nn.sigmoid(og) * hn).astype(o_ref.dtype)

        m_new = jnp.maximum(a + m, m_loc)
        s_old = jnp.exp(a + m - m_new)
        s_new = jnp.exp(m_loc - m_new)
        return (s_old * C + s_new * c_loc, s_old * n_row + s_new * n_loc, m_new)

    init = (jnp.zeros((dk, dv), jnp.float32), jnp.zeros((1, dk), jnp.float32), jnp.zeros((1, 1), jnp.float32))
    lax.fori_loop(0, nc, chunk, (init,) * heads, unroll=ML_CHUNK_UNROLL)


def _mlstm(qk, v, og, gates_t, h_gain, batch, seq):
    n = qk.shape[0]
    L = ML_CHUNK
    nc = seq // L
    hb = ML_HEADS_PER_STEP
    nhb = ML_HEADS // hb
    g4 = gates_t.reshape(2 * ML_HEADS, batch * nc, 1, L)
    return pl.pallas_call(
        _mlstm_kernel,
        grid=(batch, nhb),
        in_specs=[pl.BlockSpec((seq, hb * ML_QK_DIM), lambda b, h: (b, h)),
                  pl.BlockSpec((seq, hb * ML_QK_DIM), lambda b, h: (b, nhb + h)),
                  pl.BlockSpec((seq, hb * ML_V_DIM), lambda b, h: (b, h)),
                  pl.BlockSpec((seq, hb * ML_V_DIM), lambda b, h: (b, h)),
                  pl.BlockSpec((hb, nc, 1, L), lambda b, h: (h, b, 0, 0)),
                  pl.BlockSpec((hb, nc, 1, L), lambda b, h: (nhb + h, b, 0, 0)),
                  pl.BlockSpec((1, ML_V_DIM), lambda b, h: (0, 0))],
        out_specs=pl.BlockSpec((seq, hb * ML_V_DIM), lambda b, h: (b, h)),
        out_shape=jax.ShapeDtypeStruct((n, ML_HEADS * ML_V_DIM), jnp.bfloat16),
        compiler_params=_params("parallel", "parallel"),
        name="mlstm",
    )(qk, qk, v, og, g4, g4, h_gain.reshape(1, ML_V_DIM))


def _outproj_kernel(a_ref, w_ref, x_ref, o_ref):
    o_ref[...] = x_ref[...] + jnp.dot(a_ref[...], w_ref[...], preferred_element_type=jnp.float32)


def _outproj(a, w, x, tm=512):
    n, d = x.shape
    k = a.shape[1]
    return pl.pallas_call(
        _outproj_kernel,
        grid=(n // tm,),
        in_specs=[pl.BlockSpec((tm, k), lambda i: (i, 0)),
                  pl.BlockSpec((k, d), lambda i: (0, 0)),
                  pl.BlockSpec((tm, d), lambda i: (i, 0))],
        out_specs=pl.BlockSpec((tm, d), lambda i: (i, 0)),
        out_shape=jax.ShapeDtypeStruct((n, d), jnp.float32),
        compiler_params=_params("parallel"),
        name="outproj",
    )(a, w, x)


def _ffn_step(first_chunk, x_ref, g_ref, wu, wd, o_ref, hn_ref):
    @pl.when(first_chunk)
    def _():
        x = x_ref[...]
        ms = jnp.mean(x * x, axis=-1, keepdims=True)
        hn_ref[...] = (x * lax.rsqrt(ms + EPS) * g_ref[...]).astype(hn_ref.dtype)
        o_ref[...] = x

    acts = []
    for s in range(wu.shape[1] // FFN_SUB):
        u = jnp.dot(hn_ref[...], wu[:, s * FFN_SUB:(s + 1) * FFN_SUB], preferred_element_type=jnp.float32)
        acts.append(jnp.square(jnp.maximum(u, 0.0)).astype(jnp.bfloat16))
    o_ref[...] += jnp.dot(jnp.concatenate(acts, axis=1), wd[...], preferred_element_type=jnp.float32)


def _ffn_first_tile_kernel(x_ref, g_ref, wu32_ref, wd32_ref, o_ref, wu16_ref, wd16_ref, hn_ref):
    wu16_ref[...] = wu32_ref[...].astype(wu16_ref.dtype)
    wd16_ref[...] = wd32_ref[...].astype(wd16_ref.dtype)
    _ffn_step(pl.program_id(0) == 0, x_ref, g_ref, wu16_ref, wd16_ref, o_ref, hn_ref)


def _ffn_kernel(x_ref, g_ref, wu_ref, wd_ref, *rest, cast_steps, first_done):
    n_cast = len(cast_steps)
    cast_src, o_ref, cast_dst, hn_ref = rest[:n_cast], rest[n_cast], rest[n_cast + 1:-1], rest[-1]
    i, j = pl.program_id(0), pl.program_id(1)
    step = i * pl.num_programs(1) + j
    for src, dst, steps in zip(cast_src, cast_dst, cast_steps):
        @pl.when(step < steps)
        def _(src=src, dst=dst):
            slab = src[...] if src.shape == dst.shape else src[...].T
            dst[...] = slab.astype(dst.dtype)

    if first_done:
        @pl.when(step == 0)
        def _():
            o_ref[...] = x_ref[...]

        @pl.when(i > 0)
        def _():
            _ffn_step(j == 0, x_ref, g_ref, wu_ref, wd_ref, o_ref, hn_ref)
    else:
        _ffn_step(j == 0, x_ref, g_ref, wu_ref, wd_ref, o_ref, hn_ref)


FFN_ROWS, FFN_CHUNK = 1024, 512


def _ffn_first_tile(x, g, w_up32, w_down32, layer):
    n, d = x.shape
    tm, tf = min(FFN_ROWS, n), FFN_CHUNK
    f = w_up32.shape[2]
    return pl.pallas_call(
        _ffn_first_tile_kernel,
        grid=(f // tf,),
        in_specs=[pl.BlockSpec((tm, d), lambda j: (0, 0), pipeline_mode=pl.Buffered(1)),
                  pl.BlockSpec((1, d), lambda j: (0, 0)),
                  pl.BlockSpec((None, d, tf), lambda j: (layer, 0, j)),
                  pl.BlockSpec((None, tf, d), lambda j: (layer, j, 0))],
        out_specs=[pl.BlockSpec((tm, d), lambda j: (0, 0)),
                   pl.BlockSpec((d, tf), lambda j: (0, j)),
                   pl.BlockSpec((tf, d), lambda j: (j, 0))],
        out_shape=[jax.ShapeDtypeStruct((tm, d), jnp.float32), jax.ShapeDtypeStruct((d, f), jnp.bfloat16),
                   jax.ShapeDtypeStruct((f, d), jnp.bfloat16)],
        scratch_shapes=[pltpu.VMEM((tm, d), jnp.bfloat16)],
        compiler_params=_params("arbitrary"),
        name="ffn_first_tile",
    )(x, g.reshape(1, d), w_up32, w_down32)


def _ffn(x, g, w_up, w_down, cast=(), first_done=False):
    n, d = x.shape
    tm, tf = min(FFN_ROWS, n), FFN_CHUNK
    f = w_up.shape[1]
    ni, nj = n // tm, f // tf
    chunk = (lambda i, j: jnp.where(i > 0, j, 0)) if first_done else (lambda i, j: j)
    in_specs, out_specs, out_shapes, steps = [], [], [], []
    for a, layer, transposed in cast:
        rows, cols = a.shape[1:]
        if transposed:
            tiles = rows // LANES
            width = LANES * (-(-tiles // (ni * nj)))
            used = tiles * LANES // width
            assert used * width == tiles * LANES
            last = used - 1
            in_specs.append(pl.BlockSpec((None, width, cols),
                                         lambda i, j, layer=layer, last=last: (layer, jnp.minimum(i * nj + j, last), 0)))
            out_specs.append(pl.BlockSpec((cols, width), lambda i, j, last=last: (0, jnp.minimum(i * nj + j, last))))
            out_shapes.append(jax.ShapeDtypeStruct((cols, tiles * LANES), jnp.bfloat16))
        else:
            used = ni * nj
            in_specs.append(pl.BlockSpec((None, rows // used, cols), lambda i, j, layer=layer: (layer, i * nj + j, 0)))
            out_specs.append(pl.BlockSpec((rows // used, cols), lambda i, j: (i * nj + j, 0)))
            out_shapes.append(jax.ShapeDtypeStruct((rows, cols), jnp.bfloat16))
        steps.append(used)
    out = pl.pallas_call(
        functools.partial(_ffn_kernel, cast_steps=tuple(steps), first_done=first_done),
        grid=(ni, nj),
        in_specs=[pl.BlockSpec((tm, d), lambda i, j: (i, 0)),
                  pl.BlockSpec((1, d), lambda i, j: (0, 0)),
                  pl.BlockSpec((d, tf), lambda i, j: (0, chunk(i, j))),
                  pl.BlockSpec((tf, d), lambda i, j: (chunk(i, j), 0))] + in_specs,
        out_specs=[pl.BlockSpec((tm, d), lambda i, j: (i, 0))] + out_specs,
        out_shape=[jax.ShapeDtypeStruct((n, d), jnp.float32)] + out_shapes,
        scratch_shapes=[pltpu.VMEM((tm, d), jnp.bfloat16)],
        compiler_params=_params("arbitrary", "arbitrary"),
        name="ffn",
    )(x, g.reshape(1, d), w_up, w_down, *[a for a, _, _ in cast])
    return out[0], out[1:]


def _attention_layer(h, batch, seq, g_mix, w, w_tail_t, q_gain, k_gain, w_out):
    bf = jnp.bfloat16
    n, d = h.shape
    n_qk = (ATT_HEADS + ATT_KV_HEADS) * ATT_HEAD_DIM
    n_v = ATT_KV_HEADS * ATT_HEAD_DIM
    n_qi = IDX_HEADS * IDX_HEAD_DIM
    tm = min(PROJ_ROWS, seq)

    c, s1, s2, half = _rope_tables(seq, ATT_HEAD_DIM, 1)
    ci, s1i, s2i, half_i = _rope_tables(seq, IDX_HEAD_DIM, LANES // IDX_HEAD_DIM)
    lane = jnp.arange(LANES)
    is_k = (lane < IDX_HEAD_DIM)[None, :]
    is_w = jnp.logical_and(lane >= IDX_HEAD_DIM, lane < IDX_HEAD_DIM + IDX_HEADS)[None, :]
    w_scale = IDX_HEADS ** -0.5 * IDX_HEAD_DIM ** -0.5
    tab = jnp.stack([c, s1, s2])
    tab_i = jnp.stack([ci, s1i, s2i])
    tab_t = jnp.stack([jnp.where(is_k, ci, jnp.where(is_w, w_scale, 0.0)),
                       jnp.where(is_k, s1i, 0.0), jnp.where(is_k, s2i, 0.0)])
    gains = jnp.stack([q_gain, k_gain]).reshape(2, 1, ATT_HEAD_DIM)
    n_main = n_qk + n_v + n_qi
    w_tail = jnp.pad(w_tail_t.T.astype(bf), ((0, 0), (0, LANES - w_tail_t.shape[0])))

    row = lambda width: pl.BlockSpec((tm, width), lambda i: (i, 0))
    whole = lambda a: pl.BlockSpec(a.shape, lambda i: (0,) * a.ndim, pipeline_mode=pl.Buffered(1))
    tbl = pl.BlockSpec((3, tm, LANES), lambda i: (0, i % (seq // tm), 0))
    qk, v, qi, kw = pl.pallas_call(
        functools.partial(_att_proj_kernel, half=half, half_i=half_i),
        grid=(n // tm,),
        in_specs=[row(d), whole(g_mix.reshape(1, d)),
                  pl.BlockSpec((d, n_main), lambda i: (0, 0), pipeline_mode=pl.Buffered(1)), whole(w_tail),
                  whole(gains), tbl, tbl, tbl],
        out_specs=[row(n_qk), row(n_v), row(n_qi), row(LANES)],
        out_shape=[jax.ShapeDtypeStruct((n, n_qk), bf), jax.ShapeDtypeStruct((n, n_v), bf),
                   jax.ShapeDtypeStruct((n, n_qi), bf), jax.ShapeDtypeStruct((n, LANES), jnp.float32)],
        scratch_shapes=[pltpu.VMEM((tm, d), bf)],
        compiler_params=_params("parallel"),
        name="att_proj",
    )(h, g_mix.reshape(1, d), w, w_tail, gains, tab, tab_i, tab_t)

    return _attention(qk, v, qi, kw, h, w_out.astype(bf), batch, seq)


def _mlstm_layer(h, batch, seq, g_mix, w, w_tail_t, b_gate, h_gain, w_out):
    bf = jnp.bfloat16
    n, d = h.shape
    n_qk, n_v = 2 * ML_HEADS * ML_QK_DIM, ML_HEADS * ML_V_DIM
    n_g = 2 * ML_HEADS
    tm = min(PROJ_ROWS, seq)
    g2 = g_mix.reshape(1, d)
    wg_t = w_tail_t.astype(bf)
    n_main = n_qk + n_v + d

    row = lambda width: pl.BlockSpec((tm, width), lambda i: (i, 0))
    whole = lambda a: pl.BlockSpec(a.shape, lambda i: (0,) * a.ndim, pipeline_mode=pl.Buffered(1))
    qk, v, og, gates_t = pl.pallas_call(
        _ml_proj_kernel,
        grid=(n // tm,),
        in_specs=[row(d), whole(g2), pl.BlockSpec((d, n_main), lambda i: (0, 0), pipeline_mode=pl.Buffered(1)),
                  whole(wg_t), pl.BlockSpec((n_g, 1), lambda i: (0, 0))],
        out_specs=[row(n_qk), row(n_v), row(d), pl.BlockSpec((n_g, tm), lambda i: (0, i))],
        out_shape=[jax.ShapeDtypeStruct((n, n_qk), bf), jax.ShapeDtypeStruct((n, n_v), bf),
                   jax.ShapeDtypeStruct((n, d), jnp.float32), jax.ShapeDtypeStruct((n_g, n), jnp.float32)],
        scratch_shapes=[pltpu.VMEM((tm, d), bf)],
        compiler_params=_params("parallel"),
        name="ml_proj",
    )(h, g2, w, wg_t, b_gate.reshape(n_g, 1))
    y = _mlstm(qk, v, og, gates_t, h_gain, batch, seq)
    return _outproj(y, w_out.astype(bf), h)


def kernel(x, norm_mix, norm_ffn, att_w_in, att_q_gain, att_k_gain, att_w_out, ml_w_in, ml_b_gate, ml_h_gain,
           ml_w_out, ffn_w_up, ffn_w_down):
    batch, seq, d = x.shape
    bf = jnp.bfloat16
    h = x.reshape(batch * seq, d)
    depth = norm_mix.shape[0]
    att_w_in_t, ml_w_in_t = jnp.swapaxes(att_w_in, 1, 2), jnp.swapaxes(ml_w_in, 1, 2)
    tail_t = lambda w_t, idx: w_t[idx, w_t.shape[1] // LANES * LANES:, :]
    ready = {"w_in": att_w_in[0].astype(bf), "w_out": att_w_out[0].astype(bf)}
    for i in range(depth):
        j = i // 2
        if i % 2 == 0:
            h = _attention_layer(h, batch, seq, norm_mix[i], ready["w_in"], tail_t(att_w_in_t, j), att_q_gain[j],
                                 att_k_gain[j], ready["w_out"])
        else:
            h = _mlstm_layer(h, batch, seq, norm_mix[i], ready["w_in"], tail_t(ml_w_in_t, j), ml_b_gate[j],
                             ml_h_gain[j], ready["w_out"])
        nxt = i + 1
        cast = []
        if nxt < depth:
            w_in_t, w_out = (att_w_in_t, att_w_out) if nxt % 2 == 0 else (ml_w_in_t, ml_w_out)
            cast = [(ffn_w_up, nxt, False), (ffn_w_down, nxt, False), (w_in_t, nxt // 2, True),
                    (w_out, nxt // 2, False)]
        if "w_up" in ready:
            h, done = _ffn(h, norm_ffn[i], ready["w_up"], ready["w_down"], cast)
        else:
            first, w_up, w_down = _ffn_first_tile(h, norm_ffn[i], ffn_w_up, ffn_w_down, i)
            h = lax.dynamic_update_slice(h, first, (0, 0))
            h, done = _ffn(h, norm_ffn[i], w_up, w_down, cast, first_done=True)
        if done:
            ready = dict(zip(("w_up", "w_down", "w_in", "w_out"), done))
    return h.reshape(batch, seq, d)
```

```python
import functools

import jax
import jax.numpy as jnp
from jax import lax
from jax.experimental import pallas as pl
from jax.experimental.pallas import tpu as pltpu

D_MODEL = 2048
ATT_HEADS = 16
ATT_KV_HEADS = 4
ATT_HEAD_DIM = 128
IDX_HEADS = 16
IDX_HEAD_DIM = 64
TOPK_MAX = 256
ML_HEADS = 8
ML_V_DIM = 256
ML_QK_DIM = 128
GATE_SOFTCAP = 15.0
D_FF = 4 * D_MODEL
ROPE_THETA = 500000.0
ROT_FRAC = 4
EPS = 1e-6

LANES = 128
SUBLANES = 8
BF16_ROWS = 16
VMEM_LIMIT = 60 * 1024 * 1024
INT_MIN = -(2 ** 31)
NEG_BIG = -1e30

LOG2_E = 1.4426950408889634

ATT_Q_BLOCK = 256
ATT_KEY_CHUNK = 256
SEARCH_UNROLL = 4
SEARCH_TRIPS = 32 // SEARCH_UNROLL
ML_CHUNK = 256
ML_HEADS_PER_STEP = 4
ML_CHUNK_UNROLL = 2
PROJ_ROWS = 512
PROJ_SUB = 512
FFN_SUB = 256


def _params(*sem):
    return pltpu.CompilerParams(dimension_semantics=sem, vmem_limit_bytes=VMEM_LIMIT)


def _store_rmsnorm(x_ref, g_ref, hn_ref):
    x = x_ref[...]
    ms = jnp.mean(x * x, axis=-1, keepdims=True)
    hn_ref[...] = (x * lax.rsqrt(ms + EPS) * g_ref[...]).astype(hn_ref.dtype)

def _rope_tables(seq, head_dim, heads_per_vreg):
    rot = head_dim // ROT_FRAC
    half = rot // 2
    inv_freq = ROPE_THETA ** (-2.0 * jnp.arange(half, dtype=jnp.float32) / rot)
    ang = jnp.arange(seq).astype(jnp.float32)[:, None] * inv_freq[None, :]
    cos, sin = jnp.cos(ang), jnp.sin(ang)
    ones = jnp.ones((seq, head_dim - rot), jnp.float32)
    zeros_h = jnp.zeros((seq, half), jnp.float32)
    zeros_r = jnp.zeros((seq, head_dim - rot), jnp.float32)
    c = jnp.concatenate([cos, cos, ones], axis=1)
    s1 = jnp.concatenate([-sin, zeros_h, zeros_r], axis=1)
    s2 = jnp.concatenate([zeros_h, sin, zeros_r], axis=1)
    rep = lambda t: jnp.tile(t, (1, heads_per_vreg))
    return rep(c), rep(s1), rep(s2), half


def _rope(y, c, s1, s2, half):
    return y * c + pltpu.roll(y, LANES - half, 1) * s1 + pltpu.roll(y, half, 1) * s2


def _att_proj_kernel(x_ref, g_ref, w_ref, wt_ref, gain_ref, tab_ref, tabi_ref, tabt_ref,
                     qk_ref, v_ref, qi_ref, kw_ref, hn_ref, *, half, half_i):
    _store_rmsnorm(x_ref, g_ref, hn_ref)
    n_qk, n_v, n_qi = qk_ref.shape[1], v_ref.shape[1], qi_ref.shape[1]
    n_q = ATT_HEADS * ATT_HEAD_DIM

    def cols(start, width):
        return lax.dot_general(hn_ref[...], w_ref[start:start + width, :], (((1,), (1,)), ((), ())),
                               preferred_element_type=jnp.float32)

    for s in range(n_qk // PROJ_SUB):
        acc = cols(s * PROJ_SUB, PROJ_SUB)
        gain = gain_ref[0 if s * PROJ_SUB < n_q else 1]
        for grp in range(PROJ_SUB // LANES):
            xg = acc[:, grp * LANES:(grp + 1) * LANES]
            ms = jnp.mean(xg * xg, axis=-1, keepdims=True)
            y = _rope(xg * lax.rsqrt(ms + EPS) * gain, tab_ref[0], tab_ref[1], tab_ref[2], half)
            qk_ref[:, s * PROJ_SUB + grp * LANES:s * PROJ_SUB + (grp + 1) * LANES] = y.astype(qk_ref.dtype)

    v_ref[...] = cols(n_qk, n_v).astype(v_ref.dtype)

    for s in range(n_qi // PROJ_SUB):
        acc = cols(n_qk + n_v + s * PROJ_SUB, PROJ_SUB)
        for grp in range(PROJ_SUB // LANES):
            y = _rope(acc[:, grp * LANES:(grp + 1) * LANES], tabi_ref[0], tabi_ref[1], tabi_ref[2], half_i)
            qi_ref[:, s * PROJ_SUB + grp * LANES:s * PROJ_SUB + (grp + 1) * LANES] = y.astype(qi_ref.dtype)

    tail = jnp.dot(hn_ref[...], wt_ref[...], preferred_element_type=jnp.float32)
    kw_ref[...] = _rope(tail, tabt_ref[0], tabt_ref[1], tabt_ref[2], half_i)


def _ml_proj_kernel(x_ref, g_ref, w_ref, wg_ref, b_ref, qk_ref, v_ref, og_ref, gates_ref, hn_ref):
    _store_rmsnorm(x_ref, g_ref, hn_ref)
    n_qk, n_v = qk_ref.shape[1], v_ref.shape[1]
    for s in range((n_qk + n_v + og_ref.shape[1]) // PROJ_SUB):
        acc = lax.dot_general(hn_ref[...], w_ref[s * PROJ_SUB:(s + 1) * PROJ_SUB, :], (((1,), (1,)), ((), ())),
                              preferred_element_type=jnp.float32)
        start = s * PROJ_SUB
        if start < n_qk // 2:
            qk_ref[:, start:start + PROJ_SUB] = acc.astype(qk_ref.dtype)
        elif start < n_qk:
            qk_ref[:, start:start + PROJ_SUB] = (acc * ML_QK_DIM ** -0.5).astype(qk_ref.dtype)
        elif start < n_qk + n_v:
            v_ref[:, start - n_qk:start - n_qk + PROJ_SUB] = acc.astype(v_ref.dtype)
        else:
            og_ref[:, start - n_qk - n_v:start - n_qk - n_v + PROJ_SUB] = acc

    g = lax.dot_general(wg_ref[...], hn_ref[...], (((1,), (1,)), ((), ())),
                        preferred_element_type=jnp.float32)
    g = g + b_ref[...]
    g = GATE_SOFTCAP * jnp.tanh(g / GATE_SOFTCAP)
    logf = jnp.minimum(g, 0.0) - jnp.log1p(jnp.exp(-jnp.abs(g)))
    is_forget = lax.broadcasted_iota(jnp.int32, g.shape, 0) >= ML_HEADS
    gates_ref[...] = jnp.where(is_forget, logf, g)


def _key_to_f32(key):
    return pltpu.bitcast(jnp.where(key < 0, key ^ jnp.int32(0x7FFFFFFF), key), jnp.float32)


def _select_topk(score_ref, bias_ref, *, topk, n_chunks, side_work):
    _, tq, ck = score_ref.shape
    kf = jnp.float32(topk)

    def count(pred, thr):
        part = jnp.zeros((tq, LANES), jnp.float32)
        for c in range(n_chunks):
            for j in range(ck // LANES):
                part = part + jnp.where(pred(score_ref[c, :, j * LANES:(j + 1) * LANES], thr), 1.0, 0.0)
        return jnp.sum(part, axis=1, keepdims=True)

    key_ninf = INT_MIN + 0x7FFFFF
    key_pinf = 0x7F800000

    def search(it, tau):
        cand = tau + lax.shift_left(jnp.int32(1), jnp.int32(31) - it)
        in_range = jnp.logical_and(cand > tau, cand <= key_pinf)
        cnt = count(jnp.greater_equal, _key_to_f32(cand))
        return jnp.where(jnp.logical_and(in_range, cnt >= kf), cand, tau)

    def trip(t, tau):
        for u in range(SEARCH_UNROLL):
            tau = search(t * SEARCH_UNROLL + u, tau)
        side_work(t)
        return tau

    tau = lax.fori_loop(0, SEARCH_TRIPS, trip, jnp.full((tq, 1), key_ninf, jnp.int32))
    thr = _key_to_f32(tau)
    finite = lambda s, th: jnp.logical_and(s >= th, s > -jnp.inf)
    n_ge = count(finite, thr)

    @pl.when(jnp.max(n_ge) <= kf)
    def _():
        for c in range(n_chunks):
            bias_ref[c] = jnp.where(finite(score_ref[c], thr), 0.0, NEG_BIG)

    @pl.when(jnp.max(n_ge) > kf)
    def _():
        need = kf - count(jnp.greater, thr)
        r = lax.broadcasted_iota(jnp.int32, (ck, ck), 0)
        col = lax.broadcasted_iota(jnp.int32, (ck, ck), 1)
        before = jnp.where(r < col, 1.0, 0.0).astype(jnp.bfloat16)
        seen = jnp.zeros((tq, 1), jnp.float32)
        for c in range(n_chunks):
            s = score_ref[c]
            eq = jnp.logical_and(s == thr, s > -jnp.inf)
            e = jnp.where(eq, 1.0, 0.0)
            prefix = jnp.dot(e.astype(jnp.bfloat16), before, preferred_element_type=jnp.float32) + seen
            seen = seen + jnp.sum(e, axis=1, keepdims=True)
            keep = jnp.logical_or(s > thr, jnp.logical_and(eq, prefix < need))
            bias_ref[c] = jnp.where(keep, 0.0, NEG_BIG)


def _for_each_chunk(n, body):
    def pair(p, carry):
        body(2 * p)
        body(2 * p + 1)
        return carry

    lax.fori_loop(0, n // 2, pair, 0)
    pl.when(n % 2 == 1)(lambda: body(n - 1))


def _attention_kernel(q_ref, k_ref, v_ref, qi_ref, wq_ref, kk_ref, x_ref, wo_ref, h_ref, *scratch, topk):
    o_ref, proj_ref = scratch[-2:]
    i = pl.program_id(1)
    last = pl.num_programs(1) - 1
    sub = wo_ref.shape[2]
    assert wo_ref.shape[0] == SEARCH_TRIPS

    @pl.when(jnp.logical_and(pl.program_id(0) == 0, i == 0))
    def _():
        o_ref[...] = jnp.zeros(o_ref.shape, o_ref.dtype)

    def project(t):
        proj_ref[t] = jnp.dot(o_ref[...], wo_ref[t], preferred_element_type=jnp.float32)

    def finish_previous():
        for t in range(wo_ref.shape[0]):
            h_ref[:, t * sub:(t + 1) * sub] = x_ref[:, t * sub:(t + 1) * sub] + proj_ref[t]

    @pl.when(i < last)
    def _():
        _attention_block(q_ref, k_ref, v_ref, qi_ref, wq_ref, kk_ref, *scratch[:-1], topk=topk, q0=i * q_ref.shape[0],
                         project=project, finish_previous=finish_previous)

    @pl.when(i == last)
    def _():
        for t in range(wo_ref.shape[0]):
            project(t)
        finish_previous()


def _attention_block(q_ref, k_ref, v_ref, qi_ref, wq_ref, kk_ref,
                     qs_ref, wb_ref, score_ref, bias_ref, q4_ref, s_ref, m_ref, acc_ref, o_ref, *,
                     topk, q0, project, finish_previous):
    tq = q_ref.shape[0]
    ck = ATT_KEY_CHUNK
    half = tq // 2
    nt = (((1,), (1,)), ((), ()))
    n_chunks = (q0 + tq) // ck

    for r in range(2):
        rows = slice(r * half, (r + 1) * half)
        for h in range(IDX_HEADS):
            qs_ref[r, h * half:(h + 1) * half, :] = (
                qi_ref[rows, h * IDX_HEAD_DIM:(h + 1) * IDX_HEAD_DIM].astype(jnp.bfloat16))
            wb_ref[r * IDX_HEADS + h] = jnp.broadcast_to(
                wq_ref[rows, IDX_HEAD_DIM + h:IDX_HEAD_DIM + h + 1], (half, LANES))

    def score_chunk(c):
        off = pl.multiple_of(c * ck, ck)
        ki = kk_ref[pl.ds(off, ck), :].astype(jnp.bfloat16)[:, :IDX_HEAD_DIM]
        spos = off + lax.broadcasted_iota(jnp.int32, (half, LANES), 1)
        for r in range(2):
            d = lax.dot_general(qs_ref[r], ki, nt, preferred_element_type=jnp.float32)
            tpos = q0 + r * half + lax.broadcasted_iota(jnp.int32, (half, LANES), 0)
            for j in range(ck // LANES):
                acc = jnp.zeros((half, LANES), jnp.float32)
                for h in range(IDX_HEADS):
                    dh = d[h * half:(h + 1) * half, j * LANES:(j + 1) * LANES]
                    acc = acc + jnp.maximum(dh, 0.0) * wb_ref[r * IDX_HEADS + h]
                score_ref[c, r * half:(r + 1) * half, j * LANES:(j + 1) * LANES] = (
                    jnp.where(spos + j * LANES <= tpos, acc, -jnp.inf))

    _for_each_chunk(n_chunks, score_chunk)

    for n in range(1, score_ref.shape[0] + 1):
        pl.when(n_chunks == n)(functools.partial(_select_topk, score_ref, bias_ref, topk=topk, n_chunks=n,
                                                 side_work=project))
    finish_previous()

    exp2_scale = ATT_HEAD_DIM ** -0.5 * LOG2_E
    rep = ATT_HEADS // ATT_KV_HEADS
    pair = s_ref.shape[0]
    head_cols = lambda h: slice(h * ATT_HEAD_DIM, (h + 1) * ATT_HEAD_DIM)
    for g0 in range(0, ATT_KV_HEADS, pair):
        for u in range(pair):
            for r in range(rep):
                q4_ref[u, r * tq:(r + 1) * tq, :] = q_ref[:, head_cols((g0 + u) * rep + r)]
        m_ref[...] = jnp.full(m_ref.shape, NEG_BIG, jnp.float32)

        def logits_chunk(c):
            off = pl.multiple_of(c * ck, ck)
            for u in range(pair):
                s = lax.dot_general(q4_ref[u], k_ref[pl.ds(off, ck), head_cols(g0 + u)], nt,
                                    preferred_element_type=jnp.float32)
                s = ((s.reshape(rep, tq, ck) + bias_ref[c][None]) * exp2_scale).reshape(rep * tq, ck)
                s_ref[u, c] = s
                m = m_ref[u]
                for j in range(ck // LANES):
                    m = jnp.maximum(m, s[:, j * LANES:(j + 1) * LANES])
                m_ref[u] = m

        _for_each_chunk(n_chunks, logits_chunk)
        for u in range(pair):
            m_ref[u] = jnp.broadcast_to(jnp.max(m_ref[u], axis=1, keepdims=True), m_ref.shape[1:])
        acc_ref[...] = jnp.zeros(acc_ref.shape, jnp.float32)
        ones = jnp.ones((ck, LANES), jnp.bfloat16)

        def pv_chunk(c):
            off = pl.multiple_of(c * ck, ck)
            for u in range(pair):
                p = jnp.exp2(s_ref[u, c] - jnp.concatenate([m_ref[u]] * (ck // LANES), axis=1))
                v1 = jnp.concatenate([v_ref[pl.ds(off, ck), head_cols(g0 + u)], ones], axis=1)
                acc_ref[u] += jnp.dot(p.astype(jnp.bfloat16), v1, preferred_element_type=jnp.float32)

        _for_each_chunk(n_chunks, pv_chunk)
        for u in range(pair):
            o4 = acc_ref[u, :, :ATT_HEAD_DIM] / acc_ref[u, :, ATT_HEAD_DIM:]
            for r in range(rep):
                o_ref[:, head_cols((g0 + u) * rep + r)] = o4[r * tq:(r + 1) * tq].astype(o_ref.dtype)


def _attention(qk, v, qi, kw, x, w_out, batch, seq):
    n, d = x.shape
    tq, ck = ATT_Q_BLOCK, ATT_KEY_CHUNK
    nq = seq // tq
    rep = ATT_HEADS // ATT_KV_HEADS
    pair = 2
    topk = min(TOPK_MAX, seq // 4)
    q_w, kv_w = ATT_HEADS * ATT_HEAD_DIM, ATT_KV_HEADS * ATT_HEAD_DIM
    qi_w = IDX_HEADS * IDX_HEAD_DIM
    sub = d // SEARCH_TRIPS
    wo = w_out.reshape(q_w, SEARCH_TRIPS, sub).transpose(1, 0, 2)
    cur = lambda b, i: (b * nq + jnp.minimum(i, nq - 1), 0)
    prev = lambda b, i: (b * nq + jnp.maximum(i - 1, 0), 0)
    once = dict(pipeline_mode=pl.Buffered(1))
    return pl.pallas_call(
        functools.partial(_attention_kernel, topk=topk),
        grid=(batch, nq + 1),
        in_specs=[pl.BlockSpec((tq, q_w), cur),
                  pl.BlockSpec((seq, kv_w), lambda b, i: (b, q_w // kv_w), **once),
                  pl.BlockSpec((seq, kv_w), lambda b, i: (b, 0), **once),
                  pl.BlockSpec((tq, qi_w), cur),
                  pl.BlockSpec((tq, LANES), cur),
                  pl.BlockSpec((seq, LANES), lambda b, i: (b, 0), **once),
                  pl.BlockSpec((tq, d), prev),
                  pl.BlockSpec(wo.shape, lambda b, i: (0, 0, 0), **once)],
        out_specs=pl.BlockSpec((tq, d), prev),
        out_shape=jax.ShapeDtypeStruct((n, d), jnp.float32),
        scratch_shapes=[pltpu.VMEM((2, IDX_HEADS * tq // 2, IDX_HEAD_DIM), jnp.bfloat16),
                        pltpu.VMEM((2 * IDX_HEADS, tq // 2, LANES), jnp.float32),
                        pltpu.VMEM((seq // ck, tq, ck), jnp.float32),
                        pltpu.VMEM((seq // ck, tq, ck), jnp.float32),
                        pltpu.VMEM((pair, rep * tq, ATT_HEAD_DIM), jnp.bfloat16),
                        pltpu.VMEM((pair, seq // ck, rep * tq, ck), jnp.float32),
                        pltpu.VMEM((pair, rep * tq, LANES), jnp.float32),
                        pltpu.VMEM((pair, rep * tq, ATT_HEAD_DIM + LANES), jnp.float32),
                        pltpu.VMEM((tq, q_w), jnp.bfloat16),
                        pltpu.VMEM((SEARCH_TRIPS, tq, sub), jnp.float32)],
        compiler_params=_params("arbitrary", "arbitrary"),
        name="dsa_attention",
    )(qk, qk, v, qi, kw, kw, x, wo)


def _mlstm_kernel(q_ref, k_ref, v_ref, og_ref, ig_ref, lf_ref, hg_ref, o_ref):
    L = ML_CHUNK
    nc = q_ref.shape[0] // L
    dk, dv = ML_QK_DIM, ML_V_DIM
    heads = q_ref.shape[1] // dk
    row = lax.broadcasted_iota(jnp.int32, (L, L), 0)
    col = lax.broadcasted_iota(jnp.int32, (L, L), 1)
    tri = col <= row
    eye = col == row
    gain = hg_ref[...]

    def to_col(x_row):
        return jnp.sum(jnp.where(eye, jnp.broadcast_to(x_row, (L, L)), 0.0), axis=1, keepdims=True)

    def chunk(c, carry):
        return tuple(head_chunk(hd, c, carry[hd]) for hd in range(heads))

    def head_chunk(hd, c, state):
        C, n_row, m = state
        off = pl.multiple_of(c * L, L)
        qc = q_ref[pl.ds(off, L), hd * dk:(hd + 1) * dk]
        kc = k_ref[pl.ds(off, L), hd * dk:(hd + 1) * dk]
        vc = v_ref[pl.ds(off, L), hd * dv:(hd + 1) * dv]
        ig_r = ig_ref[hd, c]
        lf_r = lf_ref[hd, c]

        lf_b = jnp.broadcast_to(lf_r, (L, L))
        b_col = jnp.sum(jnp.where(tri, lf_b, 0.0), axis=1, keepdims=True)
        lf_col = to_col(lf_r)
        b_row = jnp.sum(jnp.where(row <= col, jnp.broadcast_to(lf_col, (L, L)), 0.0), axis=0, keepdims=True)
        a = jnp.sum(lf_r, axis=1, keepdims=True)

        g_row = a - b_row + ig_r
        m_loc = jnp.max(g_row, axis=1, keepdims=True)
        w_row = jnp.exp(g_row - m_loc)
        w_col = to_col(w_row)
        c_loc = lax.dot_general(kc, (w_col * vc.astype(jnp.float32)).astype(jnp.bfloat16),
                                (((0,), (0,)), ((), ())), preferred_element_type=jnp.float32)
        n_loc = jnp.dot(jnp.broadcast_to(w_row, (SUBLANES, L)).astype(jnp.bfloat16), kc,
                        preferred_element_type=jnp.float32)[0:1]

        dm = jnp.where(tri, b_col - b_row + ig_r, -jnp.inf)
        inter = b_col + m
        m_t = jnp.maximum(inter, jnp.max(dm, axis=1, keepdims=True))
        s = lax.dot_general(qc, kc, (((1,), (1,)), ((), ())), preferred_element_type=jnp.float32)
        s = s * jnp.exp(dm - m_t)
        s_inter = jnp.exp(inter - m_t)
        qf = qc.astype(jnp.float32)
        num = (jnp.dot(s.astype(jnp.bfloat16), vc, preferred_element_type=jnp.float32)
               + s_inter * jnp.dot(qc, C.astype(jnp.bfloat16), preferred_element_type=jnp.float32))
        den = jnp.sum(s, axis=1, keepdims=True) + s_inter * jnp.sum(qf * n_row, axis=1, keepdims=True)
        h = num / jnp.maximum(jnp.abs(den), jnp.exp(-m_t))

        ms = jnp.mean(h * h, axis=-1, keepdims=True)
        hn = h * lax.rsqrt(ms + EPS) * gain
        og = og_ref[pl.ds(off, L), hd * dv:(hd + 1) * dv]
        o_ref[pl.ds(off, L), hd * dv:(hd + 1) * dv] = (jax.nn.sigmoid(og) * hn).astype(o_ref.dtype)

        m_new = jnp.maximum(a + m, m_loc)
        s_old = jnp.exp(a + m - m_new)
        s_new = jnp.exp(m_loc - m_new)
        return (s_old * C + s_new * c_loc, s_old * n_row + s_new * n_loc, m_new)

    init = (jnp.zeros((dk, dv), jnp.float32), jnp.zeros((1, dk), jnp.float32), jnp.zeros((1, 1), jnp.float32))
    lax.fori_loop(0, nc, chunk, (init,) * heads, unroll=ML_CHUNK_UNROLL)


def _mlstm(qk, v, og, gates_t, h_gain, batch, seq):
    n = qk.shape[0]
    L = ML_CHUNK
    nc = seq // L
    hb = ML_HEADS_PER_STEP
    nhb = ML_HEADS // hb
    g4 = gates_t.reshape(2 * ML_HEADS, batch * nc, 1, L)
    return pl.pallas_call(
        _mlstm_kernel,
        grid=(batch, nhb),
        in_specs=[pl.BlockSpec((seq, hb * ML_QK_DIM), lambda b, h: (b, h)),
                  pl.BlockSpec((seq, hb * ML_QK_DIM), lambda b, h: (b, nhb + h)),
                  pl.BlockSpec((seq, hb * ML_V_DIM), lambda b, h: (b, h)),
                  pl.BlockSpec((seq, hb * ML_V_DIM), lambda b, h: (b, h)),
                  pl.BlockSpec((hb, nc, 1, L), lambda b, h: (h, b, 0, 0)),
                  pl.BlockSpec((hb, nc, 1, L), lambda b, h: (nhb + h, b, 0, 0)),
                  pl.BlockSpec((1, ML_V_DIM), lambda b, h: (0, 0))],
        out_specs=pl.BlockSpec((seq, hb * ML_V_DIM), lambda b, h: (b, h)),
        out_shape=jax.ShapeDtypeStruct((n, ML_HEADS * ML_V_DIM), jnp.bfloat16),
        compiler_params=_params("parallel", "parallel"),
        name="mlstm",
    )(qk, qk, v, og, g4, g4, h_gain.reshape(1, ML_V_DIM))


def _outproj_kernel(a_ref, w_ref, x_ref, o_ref):
    o_ref[...] = x_ref[...] + jnp.dot(a_ref[...], w_ref[...], preferred_element_type=jnp.float32)


def _outproj(a, w, x, tm=512):
    n, d = x.shape
    k = a.shape[1]
    return pl.pallas_call(
        _outproj_kernel,
        grid=(n // tm,),
        in_specs=[pl.BlockSpec((tm, k), lambda i: (i, 0)),
                  pl.BlockSpec((k, d), lambda i: (0, 0)),
                  pl.BlockSpec((tm, d), lambda i: (i, 0))],
        out_specs=pl.BlockSpec((tm, d), lambda i: (i, 0)),
        out_shape=jax.ShapeDtypeStruct((n, d), jnp.float32),
        compiler_params=_params("parallel"),
        name="outproj",
    )(a, w, x)


def _ffn_step(first_chunk, x_ref, g_ref, wu, wd, o_ref, hn_ref):
    @pl.when(first_chunk)
    def _():
        x = x_ref[...]
        ms = jnp.mean(x * x, axis=-1, keepdims=True)
        hn_ref[...] = (x * lax.rsqrt(ms + EPS) * g_ref[...]).astype(hn_ref.dtype)
        o_ref[...] = x

    acts = []
    for s in range(wu.shape[1] // FFN_SUB):
        u = jnp.dot(hn_ref[...], wu[:, s * FFN_SUB:(s + 1) * FFN_SUB], preferred_element_type=jnp.float32)
        acts.append(jnp.square(jnp.maximum(u, 0.0)).astype(jnp.bfloat16))
    o_ref[...] += jnp.dot(jnp.concatenate(acts, axis=1), wd[...], preferred_element_type=jnp.float32)


def _ffn_first_tile_kernel(x_ref, g_ref, wu32_ref, wd32_ref, o_ref, wu16_ref, wd16_ref, hn_ref):
    wu16_ref[...] = wu32_ref[...].astype(wu16_ref.dtype)
    wd16_ref[...] = wd32_ref[...].astype(wd16_ref.dtype)
    _ffn_step(pl.program_id(0) == 0, x_ref, g_ref, wu16_ref, wd16_ref, o_ref, hn_ref)


def _ffn_kernel(x_ref, g_ref, wu_ref, wd_ref, *rest, n_cast, first_done):
    cast_src, o_ref, cast_dst, hn_ref = rest[:n_cast], rest[n_cast], rest[n_cast + 1:-1], rest[-1]
    i, j = pl.program_id(0), pl.program_id(1)
    step = i * pl.num_programs(1) + j
    for src, dst in zip(cast_src, cast_dst):
        dst[...] = src[...].astype(dst.dtype)

    if first_done:
        @pl.when(step == 0)
        def _():
            o_ref[...] = x_ref[...]

        @pl.when(i > 0)
        def _():
            _ffn_step(j == 0, x_ref, g_ref, wu_ref, wd_ref, o_ref, hn_ref)
    else:
        _ffn_step(j == 0, x_ref, g_ref, wu_ref, wd_ref, o_ref, hn_ref)


FFN_ROWS, FFN_CHUNK = 1024, 512


def _ffn_first_tile(x, g, w_up32, w_down32, layer):
    n, d = x.shape
    tm, tf = min(FFN_ROWS, n), FFN_CHUNK
    f = w_up32.shape[2]
    return pl.pallas_call(
        _ffn_first_tile_kernel,
        grid=(f // tf,),
        in_specs=[pl.BlockSpec((tm, d), lambda j: (0, 0), pipeline_mode=pl.Buffered(1)),
                  pl.BlockSpec((1, d), lambda j: (0, 0)),
                  pl.BlockSpec((None, d, tf), lambda j: (layer, 0, j)),
                  pl.BlockSpec((None, tf, d), lambda j: (layer, j, 0))],
        out_specs=[pl.BlockSpec((tm, d), lambda j: (0, 0)),
                   pl.BlockSpec((d, tf), lambda j: (0, j)),
                   pl.BlockSpec((tf, d), lambda j: (j, 0))],
        out_shape=[jax.ShapeDtypeStruct((tm, d), jnp.float32), jax.ShapeDtypeStruct((d, f), jnp.bfloat16),
                   jax.ShapeDtypeStruct((f, d), jnp.bfloat16)],
        scratch_shapes=[pltpu.VMEM((tm, d), jnp.bfloat16)],
        compiler_params=_params("arbitrary"),
        name="ffn_first_tile",
    )(x, g.reshape(1, d), w_up32, w_down32)


def _ffn(x, g, w_up, w_down, cast=(), first_done=False):
    n, d = x.shape
    tm, tf = min(FFN_ROWS, n), FFN_CHUNK
    f = w_up.shape[1]
    ni, nj = n // tm, f // tf
    chunk = (lambda i, j: jnp.where(i > 0, j, 0)) if first_done else (lambda i, j: j)
    in_specs, out_specs, out_shapes = [], [], []
    for a, layer in cast:
        rows, cols = a.shape[1:]
        slab = rows // (ni * nj) // BF16_ROWS * BF16_ROWS
        in_specs.append(pl.BlockSpec((None, slab, cols), lambda i, j, layer=layer: (layer, i * nj + j, 0)))
        out_specs.append(pl.BlockSpec((slab, cols), lambda i, j: (i * nj + j, 0)))
        out_shapes.append(jax.ShapeDtypeStruct((slab * ni * nj, cols), jnp.bfloat16))
    out = pl.pallas_call(
        functools.partial(_ffn_kernel, n_cast=len(cast), first_done=first_done),
        grid=(ni, nj),
        in_specs=[pl.BlockSpec((tm, d), lambda i, j: (i, 0)),
                  pl.BlockSpec((1, d), lambda i, j: (0, 0)),
                  pl.BlockSpec((d, tf), lambda i, j: (0, chunk(i, j))),
                  pl.BlockSpec((tf, d), lambda i, j: (chunk(i, j), 0))] + in_specs,
        out_specs=[pl.BlockSpec((tm, d), lambda i, j: (i, 0))] + out_specs,
        out_shape=[jax.ShapeDtypeStruct((n, d), jnp.float32)] + out_shapes,
        scratch_shapes=[pltpu.VMEM((tm, d), jnp.bfloat16)],
        compiler_params=_params("arbitrary", "arbitrary"),
        name="ffn",
    )(x, g.reshape(1, d), w_up, w_down, *[a for a, _ in cast])
    return out[0], out[1:]


def _attention_layer(h, batch, seq, g_mix, w_t, w_tail_t, q_gain, k_gain, w_out):
    bf = jnp.bfloat16
    n, d = h.shape
    n_qk = (ATT_HEADS + ATT_KV_HEADS) * ATT_HEAD_DIM
    n_v = ATT_KV_HEADS * ATT_HEAD_DIM
    n_qi = IDX_HEADS * IDX_HEAD_DIM
    tm = min(PROJ_ROWS, seq)

    c, s1, s2, half = _rope_tables(seq, ATT_HEAD_DIM, 1)
    ci, s1i, s2i, half_i = _rope_tables(seq, IDX_HEAD_DIM, LANES // IDX_HEAD_DIM)
    lane = jnp.arange(LANES)
    is_k = (lane < IDX_HEAD_DIM)[None, :]
    is_w = jnp.logical_and(lane >= IDX_HEAD_DIM, lane < IDX_HEAD_DIM + IDX_HEADS)[None, :]
    w_scale = IDX_HEADS ** -0.5 * IDX_HEAD_DIM ** -0.5
    tab = jnp.stack([c, s1, s2])
    tab_i = jnp.stack([ci, s1i, s2i])
    tab_t = jnp.stack([jnp.where(is_k, ci, jnp.where(is_w, w_scale, 0.0)),
                       jnp.where(is_k, s1i, 0.0), jnp.where(is_k, s2i, 0.0)])
    gains = jnp.stack([q_gain, k_gain]).reshape(2, 1, ATT_HEAD_DIM)
    n_main = n_qk + n_v + n_qi
    w_tail = jnp.pad(w_tail_t.T.astype(bf), ((0, 0), (0, LANES - w_tail_t.shape[0])))

    row = lambda width: pl.BlockSpec((tm, width), lambda i: (i, 0))
    whole = lambda a: pl.BlockSpec(a.shape, lambda i: (0,) * a.ndim, pipeline_mode=pl.Buffered(1))
    tbl = pl.BlockSpec((3, tm, LANES), lambda i: (0, i % (seq // tm), 0))
    qk, v, qi, kw = pl.pallas_call(
        functools.partial(_att_proj_kernel, half=half, half_i=half_i),
        grid=(n // tm,),
        in_specs=[row(d), whole(g_mix.reshape(1, d)),
                  pl.BlockSpec((n_main, d), lambda i: (0, 0), pipeline_mode=pl.Buffered(1)), whole(w_tail),
                  whole(gains), tbl, tbl, tbl],
        out_specs=[row(n_qk), row(n_v), row(n_qi), row(LANES)],
        out_shape=[jax.ShapeDtypeStruct((n, n_qk), bf), jax.ShapeDtypeStruct((n, n_v), bf),
                   jax.ShapeDtypeStruct((n, n_qi), bf), jax.ShapeDtypeStruct((n, LANES), jnp.float32)],
        scratch_shapes=[pltpu.VMEM((tm, d), bf)],
        compiler_params=_params("parallel"),
        name="att_proj",
    )(h, g_mix.reshape(1, d), w_t, w_tail, gains, tab, tab_i, tab_t)

    return _attention(qk, v, qi, kw, h, w_out.astype(bf), batch, seq)


def _mlstm_layer(h, batch, seq, g_mix, w_t, w_tail_t, b_gate, h_gain, w_out):
    bf = jnp.bfloat16
    n, d = h.shape
    n_qk, n_v = 2 * ML_HEADS * ML_QK_DIM, ML_HEADS * ML_V_DIM
    n_g = 2 * ML_HEADS
    tm = min(PROJ_ROWS, seq)
    g2 = g_mix.reshape(1, d)
    wg_t = w_tail_t.astype(bf)
    n_main = n_qk + n_v + d

    row = lambda width: pl.BlockSpec((tm, width), lambda i: (i, 0))
    whole = lambda a: pl.BlockSpec(a.shape, lambda i: (0,) * a.ndim, pipeline_mode=pl.Buffered(1))
    qk, v, og, gates_t = pl.pallas_call(
        _ml_proj_kernel,
        grid=(n // tm,),
        in_specs=[row(d), whole(g2), pl.BlockSpec((n_main, d), lambda i: (0, 0), pipeline_mode=pl.Buffered(1)),
                  whole(wg_t), pl.BlockSpec((n_g, 1), lambda i: (0, 0))],
        out_specs=[row(n_qk), row(n_v), row(d), pl.BlockSpec((n_g, tm), lambda i: (0, i))],
        out_shape=[jax.ShapeDtypeStruct((n, n_qk), bf), jax.ShapeDtypeStruct((n, n_v), bf),
                   jax.ShapeDtypeStruct((n, d), jnp.float32), jax.ShapeDtypeStruct((n_g, n), jnp.float32)],
        scratch_shapes=[pltpu.VMEM((tm, d), bf)],
        compiler_params=_params("parallel"),
        name="ml_proj",
    )(h, g2, w_t, wg_t, b_gate.reshape(n_g, 1))
    y = _mlstm(qk, v, og, gates_t, h_gain, batch, seq)
    return _outproj(y, w_out.astype(bf), h)


def kernel(x, norm_mix, norm_ffn, att_w_in, att_q_gain, att_k_gain, att_w_out, ml_w_in, ml_b_gate, ml_h_gain,
           ml_w_out, ffn_w_up, ffn_w_down):
    batch, seq, d = x.shape
    bf = jnp.bfloat16
    h = x.reshape(batch * seq, d)
    depth = norm_mix.shape[0]
    att_w_in_t, ml_w_in_t = jnp.swapaxes(att_w_in, 1, 2), jnp.swapaxes(ml_w_in, 1, 2)
    tail_t = lambda w_t, idx: w_t[idx, w_t.shape[1] // LANES * LANES:, :]
    ready = {"w_in": att_w_in_t[0].astype(bf), "w_out": att_w_out[0].astype(bf)}
    for i in range(depth):
        j = i // 2
        if i % 2 == 0:
            h = _attention_layer(h, batch, seq, norm_mix[i], ready["w_in"], tail_t(att_w_in_t, j), att_q_gain[j],
                                 att_k_gain[j], ready["w_out"])
        else:
            h = _mlstm_layer(h, batch, seq, norm_mix[i], ready["w_in"], tail_t(ml_w_in_t, j), ml_b_gate[j],
                             ml_h_gain[j], ready["w_out"])
        nxt = i + 1
        cast = []
        if nxt < depth:
            w_in_t, w_out = (att_w_in_t, att_w_out) if nxt % 2 == 0 else (ml_w_in_t, ml_w_out)
            cast = [(ffn_w_up, nxt), (ffn_w_down, nxt), (w_in_t, nxt // 2), (w_out, nxt // 2)]
        if "w_up" in ready:
            h, done = _ffn(h, norm_ffn[i], ready["w_up"], ready["w_down"], cast)
        else:
            first, w_up, w_down = _ffn_first_tile(h, norm_ffn[i], ffn_w_up, ffn_w_down, i)
            h = lax.dynamic_update_slice(h, first, (0, 0))
            h, done = _ffn(h, norm_ffn[i], w_up, w_down, cast, first_done=True)
        if done:
            ready = dict(zip(("w_up", "w_down", "w_in", "w_out"), done))
    return h.reshape(batch, seq, d)
```

```python
import functools

import jax
import jax.numpy as jnp
from jax import lax
from jax.experimental import pallas as pl
from jax.experimental.pallas import tpu as pltpu

D_MODEL = 2048
ATT_HEADS = 16
ATT_KV_HEADS = 4
ATT_HEAD_DIM = 128
IDX_HEADS = 16
IDX_HEAD_DIM = 64
TOPK_MAX = 256
ML_HEADS = 8
ML_V_DIM = 256
ML_QK_DIM = 128
GATE_SOFTCAP = 15.0
D_FF = 4 * D_MODEL
ROPE_THETA = 500000.0
ROT_FRAC = 4
EPS = 1e-6

LANES = 128
SUBLANES = 8
BF16_ROWS = 16
VMEM_LIMIT = 60 * 1024 * 1024
INT_MIN = -(2 ** 31)
NEG_BIG = -1e30

LOG2_E = 1.4426950408889634

ATT_Q_BLOCK = 256
ATT_KEY_CHUNK = 256
SEARCH_UNROLL = 4
SEARCH_TRIPS = 32 // SEARCH_UNROLL
ML_CHUNK = 256
ML_HEADS_PER_STEP = 4
ML_CHUNK_UNROLL = 2
PROJ_ROWS = 512
PROJ_SUB = 512
FFN_SUB = 256


def _params(*sem):
    return pltpu.CompilerParams(dimension_semantics=sem, vmem_limit_bytes=VMEM_LIMIT)


def _store_rmsnorm(x_ref, g_ref, hn_ref):
    x = x_ref[...]
    ms = jnp.mean(x * x, axis=-1, keepdims=True)
    hn_ref[...] = (x * lax.rsqrt(ms + EPS) * g_ref[...]).astype(hn_ref.dtype)

def _rope_tables(seq, head_dim, heads_per_vreg):
    rot = head_dim // ROT_FRAC
    half = rot // 2
    inv_freq = ROPE_THETA ** (-2.0 * jnp.arange(half, dtype=jnp.float32) / rot)
    ang = jnp.arange(seq).astype(jnp.float32)[:, None] * inv_freq[None, :]
    cos, sin = jnp.cos(ang), jnp.sin(ang)
    ones = jnp.ones((seq, head_dim - rot), jnp.float32)
    zeros_h = jnp.zeros((seq, half), jnp.float32)
    zeros_r = jnp.zeros((seq, head_dim - rot), jnp.float32)
    c = jnp.concatenate([cos, cos, ones], axis=1)
    s1 = jnp.concatenate([-sin, zeros_h, zeros_r], axis=1)
    s2 = jnp.concatenate([zeros_h, sin, zeros_r], axis=1)
    rep = lambda t: jnp.tile(t, (1, heads_per_vreg))
    return rep(c), rep(s1), rep(s2), half


def _rope(y, c, s1, s2, half):
    return y * c + pltpu.roll(y, LANES - half, 1) * s1 + pltpu.roll(y, half, 1) * s2


def _att_proj_kernel(x_ref, g_ref, w_ref, wt_ref, gain_ref, tab_ref, tabi_ref, tabt_ref, wo_src_ref,
                     qk_ref, v_ref, qi_ref, kw_ref, wo_ref, hn_ref, *, half, half_i):
    _store_rmsnorm(x_ref, g_ref, hn_ref)
    n_qk, n_v, n_qi = qk_ref.shape[1], v_ref.shape[1], qi_ref.shape[1]
    n_q = ATT_HEADS * ATT_HEAD_DIM

    def cols(start, width):
        return lax.dot_general(hn_ref[...], w_ref[start:start + width, :], (((1,), (1,)), ((), ())),
                               preferred_element_type=jnp.float32)

    for s in range(n_qk // PROJ_SUB):
        acc = cols(s * PROJ_SUB, PROJ_SUB)
        gain = gain_ref[0 if s * PROJ_SUB < n_q else 1]
        for grp in range(PROJ_SUB // LANES):
            xg = acc[:, grp * LANES:(grp + 1) * LANES]
            ms = jnp.mean(xg * xg, axis=-1, keepdims=True)
            y = _rope(xg * lax.rsqrt(ms + EPS) * gain, tab_ref[0], tab_ref[1], tab_ref[2], half)
            qk_ref[:, s * PROJ_SUB + grp * LANES:s * PROJ_SUB + (grp + 1) * LANES] = y.astype(qk_ref.dtype)

    v_ref[...] = cols(n_qk, n_v).astype(v_ref.dtype)

    for s in range(n_qi // PROJ_SUB):
        acc = cols(n_qk + n_v + s * PROJ_SUB, PROJ_SUB)
        for grp in range(PROJ_SUB // LANES):
            y = _rope(acc[:, grp * LANES:(grp + 1) * LANES], tabi_ref[0], tabi_ref[1], tabi_ref[2], half_i)
            qi_ref[:, s * PROJ_SUB + grp * LANES:s * PROJ_SUB + (grp + 1) * LANES] = y.astype(qi_ref.dtype)

    tail = lax.dot_general(hn_ref[...], wt_ref[...], (((1,), (1,)), ((), ())), preferred_element_type=jnp.float32)
    kw_ref[...] = _rope(tail, tabt_ref[0], tabt_ref[1], tabt_ref[2], half_i)

    sub = wo_ref.shape[2]
    for t in range(wo_ref.shape[0]):
        wo_ref[t] = wo_src_ref[:, t * sub:(t + 1) * sub].astype(wo_ref.dtype)


def _ml_proj_kernel(x_ref, g_ref, w_ref, wg_ref, b_ref, qk_ref, v_ref, og_ref, gates_ref, hn_ref):
    _store_rmsnorm(x_ref, g_ref, hn_ref)
    n_qk, n_v = qk_ref.shape[1], v_ref.shape[1]
    for s in range((n_qk + n_v + og_ref.shape[1]) // PROJ_SUB):
        acc = lax.dot_general(hn_ref[...], w_ref[s * PROJ_SUB:(s + 1) * PROJ_SUB, :], (((1,), (1,)), ((), ())),
                              preferred_element_type=jnp.float32)
        start = s * PROJ_SUB
        if start < n_qk // 2:
            qk_ref[:, start:start + PROJ_SUB] = acc.astype(qk_ref.dtype)
        elif start < n_qk:
            qk_ref[:, start:start + PROJ_SUB] = (acc * ML_QK_DIM ** -0.5).astype(qk_ref.dtype)
        elif start < n_qk + n_v:
            v_ref[:, start - n_qk:start - n_qk + PROJ_SUB] = acc.astype(v_ref.dtype)
        else:
            og_ref[:, start - n_qk - n_v:start - n_qk - n_v + PROJ_SUB] = acc

    g = lax.dot_general(wg_ref[...], hn_ref[...], (((1,), (1,)), ((), ())),
                        preferred_element_type=jnp.float32)
    g = g + b_ref[...]
    g = GATE_SOFTCAP * jnp.tanh(g / GATE_SOFTCAP)
    logf = jnp.minimum(g, 0.0) - jnp.log1p(jnp.exp(-jnp.abs(g)))
    is_forget = lax.broadcasted_iota(jnp.int32, g.shape, 0) >= ML_HEADS
    gates_ref[...] = jnp.where(is_forget, logf, g)


def _key_to_f32(key):
    return pltpu.bitcast(jnp.where(key < 0, key ^ jnp.int32(0x7FFFFFFF), key), jnp.float32)


def _select_topk(score_ref, bias_ref, *, topk, n_chunks, side_work):
    _, tq, ck = score_ref.shape
    kf = jnp.float32(topk)

    def count(pred, thr):
        part = jnp.zeros((tq, LANES), jnp.float32)
        for c in range(n_chunks):
            for j in range(ck // LANES):
                part = part + jnp.where(pred(score_ref[c, :, j * LANES:(j + 1) * LANES], thr), 1.0, 0.0)
        return jnp.sum(part, axis=1, keepdims=True)

    key_ninf = INT_MIN + 0x7FFFFF
    key_pinf = 0x7F800000

    def search(it, tau):
        cand = tau + lax.shift_left(jnp.int32(1), jnp.int32(31) - it)
        in_range = jnp.logical_and(cand > tau, cand <= key_pinf)
        cnt = count(jnp.greater_equal, _key_to_f32(cand))
        return jnp.where(jnp.logical_and(in_range, cnt >= kf), cand, tau)

    def trip(t, tau):
        for u in range(SEARCH_UNROLL):
            tau = search(t * SEARCH_UNROLL + u, tau)
        side_work(t)
        return tau

    tau = lax.fori_loop(0, SEARCH_TRIPS, trip, jnp.full((tq, 1), key_ninf, jnp.int32))
    thr = _key_to_f32(tau)
    finite = lambda s, th: jnp.logical_and(s >= th, s > -jnp.inf)
    n_ge = count(finite, thr)

    @pl.when(jnp.max(n_ge) <= kf)
    def _():
        for c in range(n_chunks):
            bias_ref[c] = jnp.where(finite(score_ref[c], thr), 0.0, NEG_BIG)

    @pl.when(jnp.max(n_ge) > kf)
    def _():
        need = kf - count(jnp.greater, thr)
        r = lax.broadcasted_iota(jnp.int32, (ck, ck), 0)
        col = lax.broadcasted_iota(jnp.int32, (ck, ck), 1)
        before = jnp.where(r < col, 1.0, 0.0).astype(jnp.bfloat16)
        seen = jnp.zeros((tq, 1), jnp.float32)
        for c in range(n_chunks):
            s = score_ref[c]
            eq = jnp.logical_and(s == thr, s > -jnp.inf)
            e = jnp.where(eq, 1.0, 0.0)
            prefix = jnp.dot(e.astype(jnp.bfloat16), before, preferred_element_type=jnp.float32) + seen
            seen = seen + jnp.sum(e, axis=1, keepdims=True)
            keep = jnp.logical_or(s > thr, jnp.logical_and(eq, prefix < need))
            bias_ref[c] = jnp.where(keep, 0.0, NEG_BIG)


def _for_each_chunk(n, body):
    def pair(p, carry):
        body(2 * p)
        body(2 * p + 1)
        return carry

    lax.fori_loop(0, n // 2, pair, 0)
    pl.when(n % 2 == 1)(lambda: body(n - 1))


def _attention_kernel(q_ref, k_ref, v_ref, qi_ref, wq_ref, kk_ref, x_ref, wo_ref, h_ref, *scratch, topk):
    o_ref, proj_ref = scratch[-2:]
    i = pl.program_id(1)
    last = pl.num_programs(1) - 1
    sub = wo_ref.shape[2]
    assert wo_ref.shape[0] == SEARCH_TRIPS

    @pl.when(jnp.logical_and(pl.program_id(0) == 0, i == 0))
    def _():
        o_ref[...] = jnp.zeros(o_ref.shape, o_ref.dtype)

    def project(t):
        proj_ref[t] = jnp.dot(o_ref[...], wo_ref[t], preferred_element_type=jnp.float32)

    def finish_previous():
        for t in range(wo_ref.shape[0]):
            h_ref[:, t * sub:(t + 1) * sub] = x_ref[:, t * sub:(t + 1) * sub] + proj_ref[t]

    @pl.when(i < last)
    def _():
        _attention_block(q_ref, k_ref, v_ref, qi_ref, wq_ref, kk_ref, *scratch[:-1], topk=topk, q0=i * q_ref.shape[0],
                         project=project, finish_previous=finish_previous)

    @pl.when(i == last)
    def _():
        for t in range(wo_ref.shape[0]):
            project(t)
        finish_previous()


def _attention_block(q_ref, k_ref, v_ref, qi_ref, wq_ref, kk_ref,
                     qs_ref, wb_ref, score_ref, bias_ref, q4_ref, s_ref, m_ref, acc_ref, o_ref, *,
                     topk, q0, project, finish_previous):
    tq = q_ref.shape[0]
    ck = ATT_KEY_CHUNK
    half = tq // 2
    nt = (((1,), (1,)), ((), ()))
    n_chunks = (q0 + tq) // ck

    for r in range(2):
        rows = slice(r * half, (r + 1) * half)
        for h in range(IDX_HEADS):
            qs_ref[r, h * half:(h + 1) * half, :] = (
                qi_ref[rows, h * IDX_HEAD_DIM:(h + 1) * IDX_HEAD_DIM].astype(jnp.bfloat16))
            wb_ref[r * IDX_HEADS + h] = jnp.broadcast_to(
                wq_ref[rows, IDX_HEAD_DIM + h:IDX_HEAD_DIM + h + 1], (half, LANES))

    def score_chunk(c):
        off = pl.multiple_of(c * ck, ck)
        ki = kk_ref[pl.ds(off, ck), :].astype(jnp.bfloat16)[:, :IDX_HEAD_DIM]
        spos = off + lax.broadcasted_iota(jnp.int32, (half, LANES), 1)
        for r in range(2):
            d = lax.dot_general(qs_ref[r], ki, nt, preferred_element_type=jnp.float32)
            tpos = q0 + r * half + lax.broadcasted_iota(jnp.int32, (half, LANES), 0)
            for j in range(ck // LANES):
                acc = jnp.zeros((half, LANES), jnp.float32)
                for h in range(IDX_HEADS):
                    dh = d[h * half:(h + 1) * half, j * LANES:(j + 1) * LANES]
                    acc = acc + jnp.maximum(dh, 0.0) * wb_ref[r * IDX_HEADS + h]
                score_ref[c, r * half:(r + 1) * half, j * LANES:(j + 1) * LANES] = (
                    jnp.where(spos + j * LANES <= tpos, acc, -jnp.inf))

    _for_each_chunk(n_chunks, score_chunk)

    for n in range(1, score_ref.shape[0] + 1):
        pl.when(n_chunks == n)(functools.partial(_select_topk, score_ref, bias_ref, topk=topk, n_chunks=n,
                                                 side_work=project))
    finish_previous()

    exp2_scale = ATT_HEAD_DIM ** -0.5 * LOG2_E
    rep = ATT_HEADS // ATT_KV_HEADS
    pair = s_ref.shape[0]
    head_cols = lambda h: slice(h * ATT_HEAD_DIM, (h + 1) * ATT_HEAD_DIM)
    for g0 in range(0, ATT_KV_HEADS, pair):
        for u in range(pair):
            for r in range(rep):
                q4_ref[u, r * tq:(r + 1) * tq, :] = q_ref[:, head_cols((g0 + u) * rep + r)]
        m_ref[...] = jnp.full(m_ref.shape, NEG_BIG, jnp.float32)

        def logits_chunk(c):
            off = pl.multiple_of(c * ck, ck)
            for u in range(pair):
                s = lax.dot_general(q4_ref[u], k_ref[pl.ds(off, ck), head_cols(g0 + u)], nt,
                                    preferred_element_type=jnp.float32)
                s = ((s.reshape(rep, tq, ck) + bias_ref[c][None]) * exp2_scale).reshape(rep * tq, ck)
                s_ref[u, c] = s
                m = m_ref[u]
                for j in range(ck // LANES):
                    m = jnp.maximum(m, s[:, j * LANES:(j + 1) * LANES])
                m_ref[u] = m

        _for_each_chunk(n_chunks, logits_chunk)
        for u in range(pair):
            m_ref[u] = jnp.broadcast_to(jnp.max(m_ref[u], axis=1, keepdims=True), m_ref.shape[1:])
        acc_ref[...] = jnp.zeros(acc_ref.shape, jnp.float32)
        ones = jnp.ones((ck, LANES), jnp.bfloat16)

        def pv_chunk(c):
            off = pl.multiple_of(c * ck, ck)
            for u in range(pair):
                p = jnp.exp2(s_ref[u, c] - jnp.concatenate([m_ref[u]] * (ck // LANES), axis=1))
                v1 = jnp.concatenate([v_ref[pl.ds(off, ck), head_cols(g0 + u)], ones], axis=1)
                acc_ref[u] += jnp.dot(p.astype(jnp.bfloat16), v1, preferred_element_type=jnp.float32)

        _for_each_chunk(n_chunks, pv_chunk)
        for u in range(pair):
            o4 = acc_ref[u, :, :ATT_HEAD_DIM] / acc_ref[u, :, ATT_HEAD_DIM:]
            for r in range(rep):
                o_ref[:, head_cols((g0 + u) * rep + r)] = o4[r * tq:(r + 1) * tq].astype(o_ref.dtype)


def _attention(qk, v, qi, kw, x, wo, batch, seq):
    n, d = x.shape
    tq, ck = ATT_Q_BLOCK, ATT_KEY_CHUNK
    nq = seq // tq
    rep = ATT_HEADS // ATT_KV_HEADS
    pair = 2
    topk = min(TOPK_MAX, seq // 4)
    q_w, kv_w = ATT_HEADS * ATT_HEAD_DIM, ATT_KV_HEADS * ATT_HEAD_DIM
    qi_w = IDX_HEADS * IDX_HEAD_DIM
    sub = wo.shape[2]
    cur =lambda b, i: (b * nq + jnp.minimum(i, nq - 1), 0)
    prev = lambda b, i: (b * nq + jnp.maximum(i - 1, 0), 0)
    once = dict(pipeline_mode=pl.Buffered(1))
    return pl.pallas_call(
        functools.partial(_attention_kernel, topk=topk),
        grid=(batch, nq + 1),
        in_specs=[pl.BlockSpec((tq, q_w), cur),
                  pl.BlockSpec((seq, kv_w), lambda b, i: (b, q_w // kv_w), **once),
                  pl.BlockSpec((seq, kv_w), lambda b, i: (b, 0), **once),
                  pl.BlockSpec((tq, qi_w), cur),
                  pl.BlockSpec((tq, LANES), cur),
                  pl.BlockSpec((seq, LANES), lambda b, i: (b, 0), **once),
                  pl.BlockSpec((tq, d), prev),
                  pl.BlockSpec(wo.shape, lambda b, i: (0, 0, 0), **once)],
        out_specs=pl.BlockSpec((tq, d), prev),
        out_shape=jax.ShapeDtypeStruct((n, d), jnp.float32),
        scratch_shapes=[pltpu.VMEM((2, IDX_HEADS * tq // 2, IDX_HEAD_DIM), jnp.bfloat16),
                        pltpu.VMEM((2 * IDX_HEADS, tq // 2, LANES), jnp.float32),
                        pltpu.VMEM((seq // ck, tq, ck), jnp.float32),
                        pltpu.VMEM((seq // ck, tq, ck), jnp.float32),
                        pltpu.VMEM((pair, rep * tq, ATT_HEAD_DIM), jnp.bfloat16),
                        pltpu.VMEM((pair, seq // ck, rep * tq, ck), jnp.float32),
                        pltpu.VMEM((pair, rep * tq, LANES), jnp.float32),
                        pltpu.VMEM((pair, rep * tq, ATT_HEAD_DIM + LANES), jnp.float32),
                        pltpu.VMEM((tq, q_w), jnp.bfloat16),
                        pltpu.VMEM((SEARCH_TRIPS, tq, sub), jnp.float32)],
        compiler_params=_params("arbitrary", "arbitrary"),
        name="dsa_attention",
    )(qk, qk, v, qi, kw, kw, x, wo)


def _mlstm_kernel(q_ref, k_ref, v_ref, og_ref, ig_ref, lf_ref, hg_ref, o_ref):
    L = ML_CHUNK
    nc = q_ref.shape[0] // L
    dk, dv = ML_QK_DIM, ML_V_DIM
    heads = q_ref.shape[1] // dk
    row = lax.broadcasted_iota(jnp.int32, (L, L), 0)
    col = lax.broadcasted_iota(jnp.int32, (L, L), 1)
    tri = col <= row
    eye = col == row
    gain = hg_ref[...]

    def to_col(x_row):
        return jnp.sum(jnp.where(eye, jnp.broadcast_to(x_row, (L, L)), 0.0), axis=1, keepdims=True)

    def chunk(c, carry):
        return tuple(head_chunk(hd, c, carry[hd]) for hd in range(heads))

    def head_chunk(hd, c, state):
        C, n_row, m = state
        off = pl.multiple_of(c * L, L)
        qc = q_ref[pl.ds(off, L), hd * dk:(hd + 1) * dk]
        kc = k_ref[pl.ds(off, L), hd * dk:(hd + 1) * dk]
        vc = v_ref[pl.ds(off, L), hd * dv:(hd + 1) * dv]
        ig_r = ig_ref[hd, c]
        lf_r = lf_ref[hd, c]

        lf_b = jnp.broadcast_to(lf_r, (L, L))
        b_col = jnp.sum(jnp.where(tri, lf_b, 0.0), axis=1, keepdims=True)
        lf_col = to_col(lf_r)
        b_row = jnp.sum(jnp.where(row <= col, jnp.broadcast_to(lf_col, (L, L)), 0.0), axis=0, keepdims=True)
        a = jnp.sum(lf_r, axis=1, keepdims=True)

        g_row = a - b_row + ig_r
        m_loc = jnp.max(g_row, axis=1, keepdims=True)
        w_row = jnp.exp(g_row - m_loc)
        w_col = to_col(w_row)
        c_loc = lax.dot_general(kc, (w_col * vc.astype(jnp.float32)).astype(jnp.bfloat16),
                                (((0,), (0,)), ((), ())), preferred_element_type=jnp.float32)
        n_loc = jnp.dot(jnp.broadcast_to(w_row, (SUBLANES, L)).astype(jnp.bfloat16), kc,
                        preferred_element_type=jnp.float32)[0:1]

        dm = jnp.where(tri, b_col - b_row + ig_r, -jnp.inf)
        inter = b_col + m
        m_t = jnp.maximum(inter, jnp.max(dm, axis=1, keepdims=True))
        s = lax.dot_general(qc, kc, (((1,), (1,)), ((), ())), preferred_element_type=jnp.float32)
        s = s * jnp.exp(dm - m_t)
        s_inter = jnp.exp(inter - m_t)
        qf = qc.astype(jnp.float32)
        num = (jnp.dot(s.astype(jnp.bfloat16), vc, preferred_element_type=jnp.float32)
               + s_inter * jnp.dot(qc, C.astype(jnp.bfloat16), preferred_element_type=jnp.float32))
        den = jnp.sum(s, axis=1, keepdims=True) + s_inter * jnp.sum(qf * n_row, axis=1, keepdims=True)
        h = num / jnp.maximum(jnp.abs(den), jnp.exp(-m_t))

        ms = jnp.mean(h * h, axis=-1, keepdims=True)
        hn = h * lax.rsqrt(ms + EPS) * gain
        og = og_ref[pl.ds(off, L), hd * dv:(hd + 1) * dv]
        o_ref[pl.ds(off, L), hd * dv:(hd + 1) * dv] = (jax.nn.sigmoid(og) * hn).astype(o_ref.dtype)

        m_new = jnp.maximum(a + m, m_loc)
        s_old = jnp.exp(a + m - m_new)
        s_new = jnp.exp(m_loc - m_new)
        return (s_old * C + s_new * c_loc, s_old * n_row + s_new * n_loc, m_new)

    init = (jnp.zeros((dk, dv), jnp.float32), jnp.zeros((1, dk), jnp.float32), jnp.zeros((1, 1), jnp.float32))
    lax.fori_loop(0, nc, chunk, (init,) * heads, unroll=ML_CHUNK_UNROLL)


def _mlstm(qk, v, og, gates_t, h_gain, batch, seq):
    n = qk.shape[0]
    L = ML_CHUNK
    nc = seq // L
    hb = ML_HEADS_PER_STEP
    nhb = ML_HEADS // hb
    g4 = gates_t.reshape(2 * ML_HEADS, batch * nc, 1, L)
    return pl.pallas_call(
        _mlstm_kernel,
        grid=(batch, nhb),
        in_specs=[pl.BlockSpec((seq, hb * ML_QK_DIM), lambda b, h: (b, h)),
                  pl.BlockSpec((seq, hb * ML_QK_DIM), lambda b, h: (b, nhb + h)),
                  pl.BlockSpec((seq, hb * ML_V_DIM), lambda b, h: (b, h)),
                  pl.BlockSpec((seq, hb * ML_V_DIM), lambda b, h: (b, h)),
                  pl.BlockSpec((hb, nc, 1, L), lambda b, h: (h, b, 0, 0)),
                  pl.BlockSpec((hb, nc, 1, L), lambda b, h: (nhb + h, b, 0, 0)),
                  pl.BlockSpec((1, ML_V_DIM), lambda b, h: (0, 0))],
        out_specs=pl.BlockSpec((seq, hb * ML_V_DIM), lambda b, h: (b, h)),
        out_shape=jax.ShapeDtypeStruct((n, ML_HEADS * ML_V_DIM), jnp.bfloat16),
        compiler_params=_params("parallel", "parallel"),
        name="mlstm",
    )(qk, qk, v, og, g4, g4, h_gain.reshape(1, ML_V_DIM))


def _outproj_kernel(a_ref, w_ref, x_ref, o_ref):
    o_ref[...] = x_ref[...] + jnp.dot(a_ref[...], w_ref[...], preferred_element_type=jnp.float32)


def _outproj(a, w, x, tm=512):
    n, d = x.shape
    k = a.shape[1]
    return pl.pallas_call(
        _outproj_kernel,
        grid=(n // tm,),
        in_specs=[pl.BlockSpec((tm, k), lambda i: (i, 0)),
                  pl.BlockSpec((k, d), lambda i: (0, 0)),
                  pl.BlockSpec((tm, d), lambda i: (i, 0))],
        out_specs=pl.BlockSpec((tm, d), lambda i: (i, 0)),
        out_shape=jax.ShapeDtypeStruct((n, d), jnp.float32),
        compiler_params=_params("parallel"),
        name="outproj",
    )(a, w, x)


def _ffn_step(first_chunk, x_ref, g_ref, wu, wd, o_ref, hn_ref):
    @pl.when(first_chunk)
    def _():
        x = x_ref[...]
        ms = jnp.mean(x * x, axis=-1, keepdims=True)
        hn_ref[...] = (x * lax.rsqrt(ms + EPS) * g_ref[...]).astype(hn_ref.dtype)
        o_ref[...] = x

    acts = []
    for s in range(wu.shape[1] // FFN_SUB):
        u = jnp.dot(hn_ref[...], wu[:, s * FFN_SUB:(s + 1) * FFN_SUB], preferred_element_type=jnp.float32)
        acts.append(jnp.square(jnp.maximum(u, 0.0)).astype(jnp.bfloat16))
    o_ref[...] += jnp.dot(jnp.concatenate(acts, axis=1), wd[...], preferred_element_type=jnp.float32)


def _ffn_first_tile_kernel(x_ref, g_ref, wu32_ref, wd32_ref, o_ref, wu16_ref, wd16_ref, hn_ref):
    wu16_ref[...] = wu32_ref[...].astype(wu16_ref.dtype)
    wd16_ref[...] = wd32_ref[...].astype(wd16_ref.dtype)
    _ffn_step(pl.program_id(0) == 0, x_ref, g_ref, wu16_ref, wd16_ref, o_ref, hn_ref)


def _ffn_kernel(x_ref, g_ref, wu_ref, wd_ref, *rest, n_cast, first_done):
    cast_src, o_ref, cast_dst, hn_ref = rest[:n_cast], rest[n_cast], rest[n_cast + 1:-1], rest[-1]
    i, j = pl.program_id(0), pl.program_id(1)
    step = i * pl.num_programs(1) + j
    for src, dst in zip(cast_src, cast_dst):
        dst[...] = src[...].astype(dst.dtype)

    if first_done:
        @pl.when(step == 0)
        def _():
            o_ref[...] = x_ref[...]

        @pl.when(i > 0)
        def _():
            _ffn_step(j == 0, x_ref, g_ref, wu_ref, wd_ref, o_ref, hn_ref)
    else:
        _ffn_step(j == 0, x_ref, g_ref, wu_ref, wd_ref, o_ref, hn_ref)


FFN_ROWS, FFN_CHUNK = 1024, 512


def _ffn_first_tile(x, g, w_up32, w_down32, layer):
    n, d = x.shape
    tm, tf = min(FFN_ROWS, n), FFN_CHUNK
    f = w_up32.shape[2]
    return pl.pallas_call(
        _ffn_first_tile_kernel,
        grid=(f // tf,),
        in_specs=[pl.BlockSpec((tm, d), lambda j: (0, 0), pipeline_mode=pl.Buffered(1)),
                  pl.BlockSpec((1, d), lambda j: (0, 0)),
                  pl.BlockSpec((None, d, tf), lambda j: (layer, 0, j)),
                  pl.BlockSpec((None, tf, d), lambda j: (layer, j, 0))],
        out_specs=[pl.BlockSpec((tm, d), lambda j: (0, 0)),
                   pl.BlockSpec((d, tf), lambda j: (0, j)),
                   pl.BlockSpec((tf, d), lambda j: (j, 0))],
        out_shape=[jax.ShapeDtypeStruct((tm, d), jnp.float32), jax.ShapeDtypeStruct((d, f), jnp.bfloat16),
                   jax.ShapeDtypeStruct((f, d), jnp.bfloat16)],
        scratch_shapes=[pltpu.VMEM((tm, d), jnp.bfloat16)],
        compiler_params=_params("arbitrary"),
        name="ffn_first_tile",
    )(x, g.reshape(1, d), w_up32, w_down32)


def _ffn(x, g, w_up, w_down, cast=(), first_done=False):
    n, d = x.shape
    tm, tf = min(FFN_ROWS, n), FFN_CHUNK
    f = w_up.shape[1]
    ni, nj = n // tm, f // tf
    chunk = (lambda i, j: jnp.where(i > 0, j, 0)) if first_done else (lambda i, j: j)
    in_specs, out_specs, out_shapes = [], [], []
    for a, layer in cast:
        rows, cols = a.shape[1:]
        slab = rows // (ni * nj) // BF16_ROWS * BF16_ROWS
        in_specs.append(pl.BlockSpec((None, slab, cols), lambda i, j, layer=layer: (layer, i * nj + j, 0)))
        out_specs.append(pl.BlockSpec((slab, cols), lambda i, j: (i * nj + j, 0)))
        out_shapes.append(jax.ShapeDtypeStruct((slab * ni * nj, cols), jnp.bfloat16))
    out = pl.pallas_call(
        functools.partial(_ffn_kernel, n_cast=len(cast), first_done=first_done),
        grid=(ni, nj),
        in_specs=[pl.BlockSpec((tm, d), lambda i, j: (i, 0)),
                  pl.BlockSpec((1, d), lambda i, j: (0, 0)),
                  pl.BlockSpec((d, tf), lambda i, j: (0, chunk(i, j))),
                  pl.BlockSpec((tf, d), lambda i, j: (chunk(i, j), 0))] + in_specs,
        out_specs=[pl.BlockSpec((tm, d), lambda i, j: (i, 0))] + out_specs,
        out_shape=[jax.ShapeDtypeStruct((n, d), jnp.float32)] + out_shapes,
        scratch_shapes=[pltpu.VMEM((tm, d), jnp.bfloat16)],
        compiler_params=_params("arbitrary", "arbitrary"),
        name="ffn",
    )(x, g.reshape(1, d), w_up, w_down, *[a for a, _ in cast])
    return out[0], out[1:]


def _attention_layer(h, batch, seq, g_mix, w_t, w_tail_t, q_gain, k_gain, w_out):
    bf = jnp.bfloat16
    n, d = h.shape
    n_qk = (ATT_HEADS + ATT_KV_HEADS) * ATT_HEAD_DIM
    n_v = ATT_KV_HEADS * ATT_HEAD_DIM
    n_qi = IDX_HEADS * IDX_HEAD_DIM
    tm = min(PROJ_ROWS, seq)

    c, s1, s2, half = _rope_tables(seq, ATT_HEAD_DIM, 1)
    ci, s1i, s2i, half_i = _rope_tables(seq, IDX_HEAD_DIM, LANES // IDX_HEAD_DIM)
    lane = jnp.arange(LANES)
    is_k = (lane < IDX_HEAD_DIM)[None, :]
    is_w = jnp.logical_and(lane >= IDX_HEAD_DIM, lane < IDX_HEAD_DIM + IDX_HEADS)[None, :]
    w_scale = IDX_HEADS ** -0.5 * IDX_HEAD_DIM ** -0.5
    tab = jnp.stack([c, s1, s2])
    tab_i = jnp.stack([ci, s1i, s2i])
    tab_t = jnp.stack([jnp.where(is_k, ci, jnp.where(is_w, w_scale, 0.0)),
                       jnp.where(is_k, s1i, 0.0), jnp.where(is_k, s2i, 0.0)])
    gains = jnp.stack([q_gain, k_gain]).reshape(2, 1, ATT_HEAD_DIM)
    n_main = n_qk + n_v + n_qi
    w_tail = jnp.pad(w_tail_t.astype(bf), ((0, LANES - w_tail_t.shape[0]), (0, 0)))

    w_out_stack, w_out_layer = w_out
    wo_rows = w_out_stack.shape[1] // (n // tm)
    wo_sub = d // SEARCH_TRIPS

    row = lambda width: pl.BlockSpec((tm, width), lambda i: (i, 0))
    whole = lambda a: pl.BlockSpec(a.shape, lambda i: (0,) * a.ndim, pipeline_mode=pl.Buffered(1))
    tbl = pl.BlockSpec((3, tm, LANES), lambda i: (0, i % (seq // tm), 0))
    qk, v, qi, kw, wo = pl.pallas_call(
        functools.partial(_att_proj_kernel, half=half, half_i=half_i),
        grid=(n // tm,),
        in_specs=[row(d), whole(g_mix.reshape(1, d)),
                  pl.BlockSpec((n_main, d), lambda i: (0, 0), pipeline_mode=pl.Buffered(1)), whole(w_tail),
                  whole(gains), tbl, tbl, tbl,
                  pl.BlockSpec((None, wo_rows, d), lambda i: (w_out_layer, i, 0))],
        out_specs=[row(n_qk), row(n_v), row(n_qi), row(LANES),
                   pl.BlockSpec((SEARCH_TRIPS, wo_rows, wo_sub), lambda i: (0, i, 0))],
        out_shape=[jax.ShapeDtypeStruct((n, n_qk), bf), jax.ShapeDtypeStruct((n, n_v), bf),
                   jax.ShapeDtypeStruct((n, n_qi), bf), jax.ShapeDtypeStruct((n, LANES), jnp.float32),
                   jax.ShapeDtypeStruct((SEARCH_TRIPS, w_out_stack.shape[1], wo_sub), bf)],
        scratch_shapes=[pltpu.VMEM((tm, d), bf)],
        compiler_params=_params("parallel"),
        name="att_proj",
    )(h, g_mix.reshape(1, d), w_t, w_tail, gains, tab, tab_i, tab_t, w_out_stack)

    return _attention(qk, v, qi, kw, h, wo, batch, seq)


def _mlstm_layer(h, batch, seq, g_mix, w_t, w_tail_t, b_gate, h_gain, w_out):
    bf = jnp.bfloat16
    n, d = h.shape
    n_qk, n_v = 2 * ML_HEADS * ML_QK_DIM, ML_HEADS * ML_V_DIM
    n_g = 2 * ML_HEADS
    tm = min(PROJ_ROWS, seq)
    g2 = g_mix.reshape(1, d)
    wg_t = w_tail_t.astype(bf)
    n_main = n_qk + n_v + d

    row = lambda width: pl.BlockSpec((tm, width), lambda i: (i, 0))
    whole = lambda a: pl.BlockSpec(a.shape, lambda i: (0,) * a.ndim, pipeline_mode=pl.Buffered(1))
    qk, v, og, gates_t = pl.pallas_call(
        _ml_proj_kernel,
        grid=(n // tm,),
        in_specs=[row(d), whole(g2), pl.BlockSpec((n_main, d), lambda i: (0, 0), pipeline_mode=pl.Buffered(1)),
                  whole(wg_t), pl.BlockSpec((n_g, 1), lambda i: (0, 0))],
        out_specs=[row(n_qk), row(n_v), row(d), pl.BlockSpec((n_g, tm), lambda i: (0, i))],
        out_shape=[jax.ShapeDtypeStruct((n, n_qk), bf), jax.ShapeDtypeStruct((n, n_v), bf),
                   jax.ShapeDtypeStruct((n, d), jnp.float32), jax.ShapeDtypeStruct((n_g, n), jnp.float32)],
        scratch_shapes=[pltpu.VMEM((tm, d), bf)],
        compiler_params=_params("parallel"),
        name="ml_proj",
    )(h, g2, w_t, wg_t, b_gate.reshape(n_g, 1))
    y = _mlstm(qk, v, og, gates_t, h_gain, batch, seq)
    return _outproj(y, w_out.astype(bf), h)


def kernel(x, norm_mix, norm_ffn, att_w_in, att_q_gain, att_k_gain, att_w_out, ml_w_in, ml_b_gate, ml_h_gain,
           ml_w_out, ffn_w_up, ffn_w_down):
    batch, seq, d = x.shape
    bf = jnp.bfloat16
    h = x.reshape(batch * seq, d)
    depth = norm_mix.shape[0]
    att_w_in_t, ml_w_in_t = jnp.swapaxes(att_w_in, 1, 2), jnp.swapaxes(ml_w_in, 1, 2)
    def tail_t(w_t_f32, idx):
        aligned = w_t_f32.shape[1] // LANES * LANES
        have = ready["w_in"]
        return have[aligned:] if have.shape[0] > aligned else w_t_f32[idx, aligned:, :]

    ready = {"w_in": att_w_in_t[0].astype(bf)}
    for i in range(depth):
        j = i // 2
        if i % 2 == 0:
            w_out = (ready["w_out"][None], 0) if "w_out" in ready else (att_w_out, j)
            h = _attention_layer(h, batch, seq, norm_mix[i], ready["w_in"], tail_t(att_w_in_t, j), att_q_gain[j],
                                 att_k_gain[j], w_out)
        else:
            h = _mlstm_layer(h, batch, seq, norm_mix[i], ready["w_in"], tail_t(ml_w_in_t, j), ml_b_gate[j],
                             ml_h_gain[j], ready["w_out"])
        nxt = i + 1
        cast = []
        if nxt < depth:
            w_in_t, w_out = (att_w_in_t, att_w_out) if nxt % 2 == 0 else (ml_w_in_t, ml_w_out)
            cast = [(ffn_w_up, nxt), (ffn_w_down, nxt), (w_in_t, nxt // 2), (w_out, nxt // 2)]
        if "w_up" in ready:
            h, done = _ffn(h, norm_ffn[i], ready["w_up"], ready["w_down"], cast)
        else:
            first, w_up, w_down = _ffn_first_tile(h, norm_ffn[i], ffn_w_up, ffn_w_down, i)
            h = lax.dynamic_update_slice(h, first, (0, 0))
            h, done = _ffn(h, norm_ffn[i], w_up, w_down, cast, first_done=True)
        if done:
            ready = dict(zip(("w_up", "w_down", "w_in", "w_out"), done))
    return h.reshape(batch, seq, d)
```

```python
import functools

import jax
import jax.numpy as jnp
from jax import lax
from jax.experimental import pallas as pl
from jax.experimental.pallas import tpu as pltpu

D_MODEL = 2048
ATT_HEADS = 16
ATT_KV_HEADS = 4
ATT_HEAD_DIM = 128
IDX_HEADS = 16
IDX_HEAD_DIM = 64
TOPK_MAX = 256
ML_HEADS = 8
ML_V_DIM = 256
ML_QK_DIM = 128
GATE_SOFTCAP = 15.0
D_FF = 4 * D_MODEL
ROPE_THETA = 500000.0
ROT_FRAC = 4
EPS = 1e-6

LANES = 128
SUBLANES = 8
BF16_ROWS = 16
VMEM_LIMIT = 60 * 1024 * 1024
INT_MIN = -(2 ** 31)
NEG_BIG = -1e30

LOG2_E = 1.4426950408889634

ATT_Q_BLOCK = 256
ATT_KEY_CHUNK = 256
SEARCH_UNROLL = 4
SEARCH_TRIPS = 32 // SEARCH_UNROLL
ML_CHUNK = 256
ML_HEADS_PER_STEP = 4
ML_CHUNK_UNROLL = 2
PROJ_ROWS = 512
PROJ_SUB = 512
FFN_SUB = 256


def _params(*sem):
    return pltpu.CompilerParams(dimension_semantics=sem, vmem_limit_bytes=VMEM_LIMIT)


def _store_rmsnorm(x_ref, g_ref, hn_ref):
    x = x_ref[...]
    ms = jnp.mean(x * x, axis=-1, keepdims=True)
    hn_ref[...] = (x * lax.rsqrt(ms + EPS) * g_ref[...]).astype(hn_ref.dtype)

def _rope_tables(seq, head_dim, heads_per_vreg):
    rot = head_dim // ROT_FRAC
    half = rot // 2
    inv_freq = ROPE_THETA ** (-2.0 * jnp.arange(half, dtype=jnp.float32) / rot)
    ang = jnp.arange(seq).astype(jnp.float32)[:, None] * inv_freq[None, :]
    cos, sin = jnp.cos(ang), jnp.sin(ang)
    ones = jnp.ones((seq, head_dim - rot), jnp.float32)
    zeros_h = jnp.zeros((seq, half), jnp.float32)
    zeros_r = jnp.zeros((seq, head_dim - rot), jnp.float32)
    c = jnp.concatenate([cos, cos, ones], axis=1)
    s1 = jnp.concatenate([-sin, zeros_h, zeros_r], axis=1)
    s2 = jnp.concatenate([zeros_h, sin, zeros_r], axis=1)
    rep = lambda t: jnp.tile(t, (1, heads_per_vreg))
    return rep(c), rep(s1), rep(s2), half


def _rope(y, c, s1, s2, half):
    return y * c + pltpu.roll(y, LANES - half, 1) * s1 + pltpu.roll(y, half, 1) * s2


def _att_proj_kernel(x_ref, g_ref, w_ref, wt_ref, gain_ref, tab_ref, tabi_ref, tabt_ref, wo_src_ref,
                     qk_ref, v_ref, qi_ref, kw_ref, wo_ref, hn_ref, *, half, half_i):
    _store_rmsnorm(x_ref, g_ref, hn_ref)
    n_qk, n_v, n_qi = qk_ref.shape[1], v_ref.shape[1], qi_ref.shape[1]
    n_q = ATT_HEADS * ATT_HEAD_DIM

    def cols(start, width):
        return lax.dot_general(hn_ref[...], w_ref[start:start + width, :], (((1,), (1,)), ((), ())),
                               preferred_element_type=jnp.float32)

    for s in range(n_qk // PROJ_SUB):
        acc = cols(s * PROJ_SUB, PROJ_SUB)
        gain = gain_ref[0 if s * PROJ_SUB < n_q else 1]
        for grp in range(PROJ_SUB // LANES):
            xg = acc[:, grp * LANES:(grp + 1) * LANES]
            ms = jnp.mean(xg * xg, axis=-1, keepdims=True)
            y = _rope(xg * lax.rsqrt(ms + EPS) * gain, tab_ref[0], tab_ref[1], tab_ref[2], half)
            qk_ref[:, s * PROJ_SUB + grp * LANES:s * PROJ_SUB + (grp + 1) * LANES] = y.astype(qk_ref.dtype)

    v_ref[...] = cols(n_qk, n_v).astype(v_ref.dtype)

    for s in range(n_qi // PROJ_SUB):
        acc = cols(n_qk + n_v + s * PROJ_SUB, PROJ_SUB)
        for grp in range(PROJ_SUB // LANES):
            y = _rope(acc[:, grp * LANES:(grp + 1) * LANES], tabi_ref[0], tabi_ref[1], tabi_ref[2], half_i)
            qi_ref[:, s * PROJ_SUB + grp * LANES:s * PROJ_SUB + (grp + 1) * LANES] = y.astype(qi_ref.dtype)

    tail = lax.dot_general(hn_ref[...], wt_ref[...], (((1,), (1,)), ((), ())), preferred_element_type=jnp.float32)
    kw_ref[...] = _rope(tail, tabt_ref[0], tabt_ref[1], tabt_ref[2], half_i)

    sub = wo_ref.shape[2]
    for t in range(wo_ref.shape[0]):
        wo_ref[t] = wo_src_ref[:, t * sub:(t + 1) * sub].astype(wo_ref.dtype)


def _ml_proj_kernel(x_ref, g_ref, w_ref, wg_ref, b_ref, qk_ref, v_ref, og_ref, gates_ref, hn_ref):
    _store_rmsnorm(x_ref, g_ref, hn_ref)
    n_qk, n_v = qk_ref.shape[1], v_ref.shape[1]
    for s in range((n_qk + n_v + og_ref.shape[1]) // PROJ_SUB):
        acc = lax.dot_general(hn_ref[...], w_ref[s * PROJ_SUB:(s + 1) * PROJ_SUB, :], (((1,), (1,)), ((), ())),
                              preferred_element_type=jnp.float32)
        start = s * PROJ_SUB
        if start < n_qk // 2:
            qk_ref[:, start:start + PROJ_SUB] = acc.astype(qk_ref.dtype)
        elif start < n_qk:
            qk_ref[:, start:start + PROJ_SUB] = (acc * ML_QK_DIM ** -0.5).astype(qk_ref.dtype)
        elif start < n_qk + n_v:
            v_ref[:, start - n_qk:start - n_qk + PROJ_SUB] = acc.astype(v_ref.dtype)
        else:
            og_ref[:, start - n_qk - n_v:start - n_qk - n_v + PROJ_SUB] = acc

    g = lax.dot_general(wg_ref[...], hn_ref[...], (((1,), (1,)), ((), ())),
                        preferred_element_type=jnp.float32)
    g = g + b_ref[...]
    g = GATE_SOFTCAP * jnp.tanh(g / GATE_SOFTCAP)
    logf = jnp.minimum(g, 0.0) - jnp.log1p(jnp.exp(-jnp.abs(g)))
    is_forget = lax.broadcasted_iota(jnp.int32, g.shape, 0) >= ML_HEADS
    gates_ref[...] = jnp.where(is_forget, logf, g)


def _key_to_f32(key):
    return pltpu.bitcast(jnp.where(key < 0, key ^ jnp.int32(0x7FFFFFFF), key), jnp.float32)


def _select_topk(score_ref, bias_ref, *, topk, n_chunks, side_work):
    _, tq, ck = score_ref.shape
    kf = jnp.float32(topk)

    def count(pred, thr):
        part = jnp.zeros((tq, LANES), jnp.float32)
        for c in range(n_chunks):
            for j in range(ck // LANES):
                part = part + jnp.where(pred(score_ref[c, :, j * LANES:(j + 1) * LANES], thr), 1.0, 0.0)
        return jnp.sum(part, axis=1, keepdims=True)

    key_ninf = INT_MIN + 0x7FFFFF
    key_pinf = 0x7F800000

    def search(it, tau):
        cand = tau + lax.shift_left(jnp.int32(1), jnp.int32(31) - it)
        in_range = jnp.logical_and(cand > tau, cand <= key_pinf)
        cnt = count(jnp.greater_equal, _key_to_f32(cand))
        return jnp.where(jnp.logical_and(in_range, cnt >= kf), cand, tau)

    def trip(t, tau):
        for u in range(SEARCH_UNROLL):
            tau = search(t * SEARCH_UNROLL + u, tau)
        side_work(t)
        return tau

    tau = lax.fori_loop(0, SEARCH_TRIPS, trip, jnp.full((tq, 1), key_ninf, jnp.int32))
    thr = _key_to_f32(tau)
    finite = lambda s, th: jnp.logical_and(s >= th, s > -jnp.inf)
    n_ge = count(finite, thr)

    @pl.when(jnp.max(n_ge) <= kf)
    def _():
        for c in range(n_chunks):
            bias_ref[c] = jnp.where(finite(score_ref[c], thr), 0.0, NEG_BIG)

    @pl.when(jnp.max(n_ge) > kf)
    def _():
        need = kf - count(jnp.greater, thr)
        r = lax.broadcasted_iota(jnp.int32, (ck, ck), 0)
        col = lax.broadcasted_iota(jnp.int32, (ck, ck), 1)
        before = jnp.where(r < col, 1.0, 0.0).astype(jnp.bfloat16)
        seen = jnp.zeros((tq, 1), jnp.float32)
        for c in range(n_chunks):
            s = score_ref[c]
            eq = jnp.logical_and(s == thr, s > -jnp.inf)
            e = jnp.where(eq, 1.0, 0.0)
            prefix = jnp.dot(e.astype(jnp.bfloat16), before, preferred_element_type=jnp.float32) + seen
            seen = seen + jnp.sum(e, axis=1, keepdims=True)
            keep = jnp.logical_or(s > thr, jnp.logical_and(eq, prefix < need))
            bias_ref[c] = jnp.where(keep, 0.0, NEG_BIG)


def _for_each_chunk(n, body):
    def pair(p, carry):
        body(2 * p)
        body(2 * p + 1)
        return carry

    lax.fori_loop(0, n // 2, pair, 0)
    pl.when(n % 2 == 1)(lambda: body(n - 1))


def _attention_kernel(q_ref, k_ref, v_ref, qi_ref, wq_ref, kk_ref, x_ref, wo_ref, h_ref, *scratch, topk):
    o_ref, proj_ref = scratch[-2:]
    i = pl.program_id(1)
    last = pl.num_programs(1) - 1
    sub = wo_ref.shape[2]
    assert wo_ref.shape[0] == SEARCH_TRIPS

    @pl.when(jnp.logical_and(pl.program_id(0) == 0, i == 0))
    def _():
        o_ref[...] = jnp.zeros(o_ref.shape, o_ref.dtype)

    def project(t):
        proj_ref[t] = jnp.dot(o_ref[...], wo_ref[t], preferred_element_type=jnp.float32)

    def finish_previous():
        for t in range(wo_ref.shape[0]):
            h_ref[:, t * sub:(t + 1) * sub] = x_ref[:, t * sub:(t + 1) * sub] + proj_ref[t]

    @pl.when(i < last)
    def _():
        _attention_block(q_ref, k_ref, v_ref, qi_ref, wq_ref, kk_ref, *scratch[:-1], topk=topk, q0=i * q_ref.shape[0],
                         project=project, finish_previous=finish_previous)

    @pl.when(i == last)
    def _():
        for t in range(wo_ref.shape[0]):
            project(t)
        finish_previous()


def _attention_block(q_ref, k_ref, v_ref, qi_ref, wq_ref, kk_ref,
                     qs_ref, wb_ref, score_ref, bias_ref, q4_ref, s_ref, m_ref, acc_ref, o_ref, *,
                     topk, q0, project, finish_previous):
    tq = q_ref.shape[0]
    ck = ATT_KEY_CHUNK
    half = tq // 2
    nt = (((1,), (1,)), ((), ()))
    n_chunks = (q0 + tq) // ck

    for r in range(2):
        rows = slice(r * half, (r + 1) * half)
        for h in range(IDX_HEADS):
            qs_ref[r, h * half:(h + 1) * half, :] = (
                qi_ref[rows, h * IDX_HEAD_DIM:(h + 1) * IDX_HEAD_DIM].astype(jnp.bfloat16))
            wb_ref[r * IDX_HEADS + h] = jnp.broadcast_to(
                wq_ref[rows, IDX_HEAD_DIM + h:IDX_HEAD_DIM + h + 1], (half, LANES))

    def score_chunk(c):
        off = pl.multiple_of(c * ck, ck)
        ki = kk_ref[pl.ds(off, ck), :].astype(jnp.bfloat16)[:, :IDX_HEAD_DIM]
        spos = off + lax.broadcasted_iota(jnp.int32, (half, LANES), 1)
        for r in range(2):
            d = lax.dot_general(qs_ref[r], ki, nt, preferred_element_type=jnp.float32)
            tpos = q0 + r * half + lax.broadcasted_iota(jnp.int32, (half, LANES), 0)
            for j in range(ck // LANES):
                acc = jnp.zeros((half, LANES), jnp.float32)
                for h in range(IDX_HEADS):
                    dh = d[h * half:(h + 1) * half, j * LANES:(j + 1) * LANES]
                    acc = acc + jnp.maximum(dh, 0.0) * wb_ref[r * IDX_HEADS + h]
                score_ref[c, r * half:(r + 1) * half, j * LANES:(j + 1) * LANES] = (
                    jnp.where(spos + j * LANES <= tpos, acc, -jnp.inf))

    _for_each_chunk(n_chunks, score_chunk)

    for n in range(1, score_ref.shape[0] + 1):
        pl.when(n_chunks == n)(functools.partial(_select_topk, score_ref, bias_ref, topk=topk, n_chunks=n,
                                                 side_work=project))
    finish_previous()

    exp2_scale = ATT_HEAD_DIM ** -0.5 * LOG2_E
    rep = ATT_HEADS // ATT_KV_HEADS
    pair = s_ref.shape[0]
    head_cols = lambda h: slice(h * ATT_HEAD_DIM, (h + 1) * ATT_HEAD_DIM)
    for g0 in range(0, ATT_KV_HEADS, pair):
        for u in range(pair):
            for r in range(rep):
                q4_ref[u, r * tq:(r + 1) * tq, :] = q_ref[:, head_cols((g0 + u) * rep + r)]
        m_ref[...] = jnp.full(m_ref.shape, NEG_BIG, jnp.float32)

        def logits_chunk(c):
            off = pl.multiple_of(c * ck, ck)
            for u in range(pair):
                s = lax.dot_general(q4_ref[u], k_ref[pl.ds(off, ck), head_cols(g0 + u)], nt,
                                    preferred_element_type=jnp.float32)
                s = ((s.reshape(rep, tq, ck) + bias_ref[c][None]) * exp2_scale).reshape(rep * tq, ck)
                s_ref[u, c] = s
                m = m_ref[u]
                for j in range(ck // LANES):
                    m = jnp.maximum(m, s[:, j * LANES:(j + 1) * LANES])
                m_ref[u] = m

        _for_each_chunk(n_chunks, logits_chunk)
        for u in range(pair):
            m_ref[u] = jnp.broadcast_to(jnp.max(m_ref[u], axis=1, keepdims=True), m_ref.shape[1:])
        acc_ref[...] = jnp.zeros(acc_ref.shape, jnp.float32)
        ones = jnp.ones((ck, LANES), jnp.bfloat16)

        def pv_chunk(c):
            off = pl.multiple_of(c * ck, ck)
            for u in range(pair):
                p = jnp.exp2(s_ref[u, c] - jnp.concatenate([m_ref[u]] * (ck // LANES), axis=1))
                v1 = jnp.concatenate([v_ref[pl.ds(off, ck), head_cols(g0 + u)], ones], axis=1)
                acc_ref[u] += jnp.dot(p.astype(jnp.bfloat16), v1, preferred_element_type=jnp.float32)

        _for_each_chunk(n_chunks, pv_chunk)
        for u in range(pair):
            o4 = acc_ref[u, :, :ATT_HEAD_DIM] / acc_ref[u, :, ATT_HEAD_DIM:]
            for r in range(rep):
                o_ref[:, head_cols((g0 + u) * rep + r)] = o4[r * tq:(r + 1) * tq].astype(o_ref.dtype)


def _attention(qk, v, qi, kw, x, wo, batch, seq):
    n, d = x.shape
    tq, ck = ATT_Q_BLOCK, ATT_KEY_CHUNK
    nq = seq // tq
    rep = ATT_HEADS // ATT_KV_HEADS
    pair = 2
    topk = min(TOPK_MAX, seq // 4)
    q_w, kv_w = ATT_HEADS * ATT_HEAD_DIM, ATT_KV_HEADS * ATT_HEAD_DIM
    qi_w = IDX_HEADS * IDX_HEAD_DIM
    sub = wo.shape[2]
    cur =lambda b, i: (b * nq + jnp.minimum(i, nq - 1), 0)
    prev = lambda b, i: (b * nq + jnp.maximum(i - 1, 0), 0)
    once = dict(pipeline_mode=pl.Buffered(1))
    return pl.pallas_call(
        functools.partial(_attention_kernel, topk=topk),
        grid=(batch, nq + 1),
        in_specs=[pl.BlockSpec((tq, q_w), cur),
                  pl.BlockSpec((seq, kv_w), lambda b, i: (b, q_w // kv_w), **once),
                  pl.BlockSpec((seq, kv_w), lambda b, i: (b, 0), **once),
                  pl.BlockSpec((tq, qi_w), cur),
                  pl.BlockSpec((tq, LANES), cur),
                  pl.BlockSpec((seq, LANES), lambda b, i: (b, 0), **once),
                  pl.BlockSpec((tq, d), prev),
                  pl.BlockSpec(wo.shape, lambda b, i: (0, 0, 0), **once)],
        out_specs=pl.BlockSpec((tq, d), prev),
        out_shape=jax.ShapeDtypeStruct((n, d), jnp.float32),
        scratch_shapes=[pltpu.VMEM((2, IDX_HEADS * tq // 2, IDX_HEAD_DIM), jnp.bfloat16),
                        pltpu.VMEM((2 * IDX_HEADS, tq // 2, LANES), jnp.float32),
                        pltpu.VMEM((seq // ck, tq, ck), jnp.float32),
                        pltpu.VMEM((seq // ck, tq, ck), jnp.float32),
                        pltpu.VMEM((pair, rep * tq, ATT_HEAD_DIM), jnp.bfloat16),
                        pltpu.VMEM((pair, seq // ck, rep * tq, ck), jnp.float32),
                        pltpu.VMEM((pair, rep * tq, LANES), jnp.float32),
                        pltpu.VMEM((pair, rep * tq, ATT_HEAD_DIM + LANES), jnp.float32),
                        pltpu.VMEM((tq, q_w), jnp.bfloat16),
                        pltpu.VMEM((SEARCH_TRIPS, tq, sub), jnp.float32)],
        compiler_params=_params("arbitrary", "arbitrary"),
        name="dsa_attention",
    )(qk, qk, v, qi, kw, kw, x, wo)


def _mlstm_kernel(q_ref, k_ref, v_ref, og_ref, ig_ref, lf_ref, hg_ref, o_ref):
    L = ML_CHUNK
    nc = q_ref.shape[0] // L
    dk, dv = ML_QK_DIM, ML_V_DIM
    heads = q_ref.shape[1] // dk
    row = lax.broadcasted_iota(jnp.int32, (L, L), 0)
    col = lax.broadcasted_iota(jnp.int32, (L, L), 1)
    tri = col <= row
    eye = col == row
    gain = hg_ref[...]

    def to_col(x_row):
        return jnp.sum(jnp.where(eye, jnp.broadcast_to(x_row, (L, L)), 0.0), axis=1, keepdims=True)

    def chunk(c, carry):
        return tuple(head_chunk(hd, c, carry[hd]) for hd in range(heads))

    def head_chunk(hd, c, state):
        C, n_row, m = state
        off = pl.multiple_of(c * L, L)
        qc = q_ref[pl.ds(off, L), hd * dk:(hd + 1) * dk]
        kc = k_ref[pl.ds(off, L), hd * dk:(hd + 1) * dk]
        vc = v_ref[pl.ds(off, L), hd * dv:(hd + 1) * dv]
        ig_r = ig_ref[hd, c]
        lf_r = lf_ref[hd, c]

        lf_b = jnp.broadcast_to(lf_r, (L, L))
        b_col = jnp.sum(jnp.where(tri, lf_b, 0.0), axis=1, keepdims=True)
        lf_col = to_col(lf_r)
        b_row = jnp.sum(jnp.where(row <= col, jnp.broadcast_to(lf_col, (L, L)), 0.0), axis=0, keepdims=True)
        a = jnp.sum(lf_r, axis=1, keepdims=True)

        g_row = a - b_row + ig_r
        m_loc = jnp.max(g_row, axis=1, keepdims=True)
        w_row = jnp.exp(g_row - m_loc)
        w_col = to_col(w_row)
        c_loc = lax.dot_general(kc, (w_col * vc.astype(jnp.float32)).astype(jnp.bfloat16),
                                (((0,), (0,)), ((), ())), preferred_element_type=jnp.float32)
        n_loc = jnp.dot(jnp.broadcast_to(w_row, (SUBLANES, L)).astype(jnp.bfloat16), kc,
                        preferred_element_type=jnp.float32)[0:1]

        dm = jnp.where(tri, b_col - b_row + ig_r, -jnp.inf)
        inter = b_col + m
        m_t = jnp.maximum(inter, jnp.max(dm, axis=1, keepdims=True))
        s = lax.dot_general(qc, kc, (((1,), (1,)), ((), ())), preferred_element_type=jnp.float32)
        s = s * jnp.exp(dm - m_t)
        s_inter = jnp.exp(inter - m_t)
        qf = qc.astype(jnp.float32)
        num = (jnp.dot(s.astype(jnp.bfloat16), vc, preferred_element_type=jnp.float32)
               + s_inter * jnp.dot(qc, C.astype(jnp.bfloat16), preferred_element_type=jnp.float32))
        den = jnp.sum(s, axis=1, keepdims=True) + s_inter * jnp.sum(qf * n_row, axis=1, keepdims=True)
        h = num / jnp.maximum(jnp.abs(den), jnp.exp(-m_t))

        ms = jnp.mean(h * h, axis=-1, keepdims=True)
        hn = h * lax.rsqrt(ms + EPS) * gain
        og = og_ref[pl.ds(off, L), hd * dv:(hd + 1) * dv]
        o_ref[pl.ds(off, L), hd * dv:(hd + 1) * dv] = (jax.nn.sigmoid(og) * hn).astype(o_ref.dtype)

        m_new = jnp.maximum(a + m, m_loc)
        s_old = jnp.exp(a + m - m_new)
        s_new = jnp.exp(m_loc - m_new)
        return (s_old * C + s_new * c_loc, s_old * n_row + s_new * n_loc, m_new)

    init = (jnp.zeros((dk, dv), jnp.float32), jnp.zeros((1, dk), jnp.float32), jnp.zeros((1, 1), jnp.float32))
    lax.fori_loop(0, nc, chunk, (init,) * heads, unroll=ML_CHUNK_UNROLL)


def _mlstm(qk, v, og, gates_t, h_gain, batch, seq):
    n = qk.shape[0]
    L = ML_CHUNK
    nc = seq // L
    hb = ML_HEADS_PER_STEP
    nhb = ML_HEADS // hb
    g4 = gates_t.reshape(2 * ML_HEADS, batch * nc, 1, L)
    return pl.pallas_call(
        _mlstm_kernel,
        grid=(batch, nhb),
        in_specs=[pl.BlockSpec((seq, hb * ML_QK_DIM), lambda b, h: (b, h)),
                  pl.BlockSpec((seq, hb * ML_QK_DIM), lambda b, h: (b, nhb + h)),
                  pl.BlockSpec((seq, hb * ML_V_DIM), lambda b, h: (b, h)),
                  pl.BlockSpec((seq, hb * ML_V_DIM), lambda b, h: (b, h)),
                  pl.BlockSpec((hb, nc, 1, L), lambda b, h: (h, b, 0, 0)),
                  pl.BlockSpec((hb, nc, 1, L), lambda b, h: (nhb + h, b, 0, 0)),
                  pl.BlockSpec((1, ML_V_DIM), lambda b, h: (0, 0))],
        out_specs=pl.BlockSpec((seq, hb * ML_V_DIM), lambda b, h: (b, h)),
        out_shape=jax.ShapeDtypeStruct((n, ML_HEADS * ML_V_DIM), jnp.bfloat16),
        compiler_params=_params("parallel", "parallel"),
        name="mlstm",
    )(qk, qk, v, og, g4, g4, h_gain.reshape(1, ML_V_DIM))


def _outproj_kernel(a_ref, w_ref, x_ref, o_ref):
    o_ref[...] = x_ref[...] + jnp.dot(a_ref[...], w_ref[...], preferred_element_type=jnp.float32)


def _outproj(a, w, x, tm=512):
    n, d = x.shape
    k = a.shape[1]
    return pl.pallas_call(
        _outproj_kernel,
        grid=(n // tm,),
        in_specs=[pl.BlockSpec((tm, k), lambda i: (i, 0)),
                  pl.BlockSpec((k, d), lambda i: (0, 0)),
                  pl.BlockSpec((tm, d), lambda i: (i, 0))],
        out_specs=pl.BlockSpec((tm, d), lambda i: (i, 0)),
        out_shape=jax.ShapeDtypeStruct((n, d), jnp.float32),
        compiler_params=_params("parallel"),
        name="outproj",
    )(a, w, x)


def _ffn_step(first_chunk, x_ref, g_ref, wu, wd, o_ref, hn_ref):
    @pl.when(first_chunk)
    def _():
        x = x_ref[...]
        ms = jnp.mean(x * x, axis=-1, keepdims=True)
        hn_ref[...] = (x * lax.rsqrt(ms + EPS) * g_ref[...]).astype(hn_ref.dtype)
        o_ref[...] = x

    acts = []
    for s in range(wu.shape[1] // FFN_SUB):
        u = jnp.dot(hn_ref[...], wu[:, s * FFN_SUB:(s + 1) * FFN_SUB], preferred_element_type=jnp.float32)
        acts.append(jnp.square(jnp.maximum(u, 0.0)).astype(jnp.bfloat16))
    o_ref[...] += jnp.dot(jnp.concatenate(acts, axis=1), wd[...], preferred_element_type=jnp.float32)


def _ffn_first_tile_kernel(x_ref, g_ref, wu32_ref, wd32_ref, o_ref, wu16_ref, wd16_ref, hn_ref):
    wu16_ref[...] = wu32_ref[...].astype(wu16_ref.dtype)
    wd16_ref[...] = wd32_ref[...].astype(wd16_ref.dtype)
    _ffn_step(pl.program_id(0) == 0, x_ref, g_ref, wu16_ref, wd16_ref, o_ref, hn_ref)


def _ffn_kernel(x_ref, g_ref, wu_ref, wd_ref, *rest, n_cast):
    cast_src, o_ref, cast_dst, hn_ref = rest[:n_cast], rest[n_cast], rest[n_cast + 1:-1], rest[-1]
    i, j = pl.program_id(0), pl.program_id(1)
    for src, dst in zip(cast_src, cast_dst):
        dst[...] = src[...].astype(dst.dtype)

    @pl.when(jnp.logical_and(i == 0, j == 0))
    def _():
        o_ref[...] = x_ref[...]

    @pl.when(i > 0)
    def _():
        _ffn_step(j == 0, x_ref, g_ref, wu_ref, wd_ref, o_ref, hn_ref)


FFN_ROWS, FFN_CHUNK = 1024, 512


def _ffn_first_tile(x, g, w_up32, w_down32, layer):
    n, d = x.shape
    tm, tf = min(FFN_ROWS, n), FFN_CHUNK
    f = w_up32.shape[2]
    return pl.pallas_call(
        _ffn_first_tile_kernel,
        grid=(f // tf,),
        in_specs=[pl.BlockSpec((tm, d), lambda j: (0, 0), pipeline_mode=pl.Buffered(1)),
                  pl.BlockSpec((1, d), lambda j: (0, 0)),
                  pl.BlockSpec((None, d, tf), lambda j: (layer, 0, j)),
                  pl.BlockSpec((None, tf, d), lambda j: (layer, j, 0))],
        out_specs=[pl.BlockSpec((tm, d), lambda j: (0, 0)),
                   pl.BlockSpec((d, tf), lambda j: (0, j)),
                   pl.BlockSpec((tf, d), lambda j: (j, 0))],
        out_shape=[jax.ShapeDtypeStruct((tm, d), jnp.float32), jax.ShapeDtypeStruct((d, f), jnp.bfloat16),
                   jax.ShapeDtypeStruct((f, d), jnp.bfloat16)],
        scratch_shapes=[pltpu.VMEM((tm, d), jnp.bfloat16)],
        compiler_params=_params("arbitrary"),
        name="ffn_first_tile",
    )(x, g.reshape(1, d), w_up32, w_down32)


def _ffn(x, g, w_up, w_down, cast=()):
    n, d = x.shape
    tm, tf = min(FFN_ROWS, n), FFN_CHUNK
    f = w_up.shape[1]
    ni, nj = n // tm, f // tf
    chunk = lambda i, j: jnp.where(i > 0, j, 0)
    in_specs, out_specs, out_shapes = [], [], []
    for a, layer in cast:
        rows, cols = a.shape[1:]
        slab = rows // (ni * nj) // BF16_ROWS * BF16_ROWS
        in_specs.append(pl.BlockSpec((None, slab, cols), lambda i, j, layer=layer: (layer, i * nj + j, 0)))
        out_specs.append(pl.BlockSpec((slab, cols), lambda i, j: (i * nj + j, 0)))
        out_shapes.append(jax.ShapeDtypeStruct((slab * ni * nj, cols), jnp.bfloat16))
    out = pl.pallas_call(
        functools.partial(_ffn_kernel, n_cast=len(cast)),
        grid=(ni, nj),
        in_specs=[pl.BlockSpec((tm, d), lambda i, j: (i, 0)),
                  pl.BlockSpec((1, d), lambda i, j: (0, 0)),
                  pl.BlockSpec((d, tf), lambda i, j: (0, chunk(i, j))),
                  pl.BlockSpec((tf, d), lambda i, j: (chunk(i, j), 0))] + in_specs,
        out_specs=[pl.BlockSpec((tm, d), lambda i, j: (i, 0))] + out_specs,
        out_shape=[jax.ShapeDtypeStruct((n, d), jnp.float32)] + out_shapes,
        scratch_shapes=[pltpu.VMEM((tm, d), jnp.bfloat16)],
        compiler_params=_params("arbitrary", "arbitrary"),
        name="ffn",
    )(x, g.reshape(1, d), w_up, w_down, *[a for a, _ in cast])
    return out[0], out[1:]


def _attention_layer(h, batch, seq, g_mix, w_t, w_tail_t, q_gain, k_gain, w_out):
    bf = jnp.bfloat16
    n, d = h.shape
    n_qk = (ATT_HEADS + ATT_KV_HEADS) * ATT_HEAD_DIM
    n_v = ATT_KV_HEADS * ATT_HEAD_DIM
    n_qi = IDX_HEADS * IDX_HEAD_DIM
    tm = min(PROJ_ROWS, seq)

    c, s1, s2, half = _rope_tables(seq, ATT_HEAD_DIM, 1)
    ci, s1i, s2i, half_i = _rope_tables(seq, IDX_HEAD_DIM, LANES // IDX_HEAD_DIM)
    lane = jnp.arange(LANES)
    is_k = (lane < IDX_HEAD_DIM)[None, :]
    is_w = jnp.logical_and(lane >= IDX_HEAD_DIM, lane < IDX_HEAD_DIM + IDX_HEADS)[None, :]
    w_scale = IDX_HEADS ** -0.5 * IDX_HEAD_DIM ** -0.5
    tab = jnp.stack([c, s1, s2])
    tab_i = jnp.stack([ci, s1i, s2i])
    tab_t = jnp.stack([jnp.where(is_k, ci, jnp.where(is_w, w_scale, 0.0)),
                       jnp.where(is_k, s1i, 0.0), jnp.where(is_k, s2i, 0.0)])
    gains = jnp.stack([q_gain, k_gain]).reshape(2, 1, ATT_HEAD_DIM)
    n_main = n_qk + n_v + n_qi
    w_tail = jnp.pad(w_tail_t.astype(bf), ((0, LANES - w_tail_t.shape[0]), (0, 0)))

    w_out_stack, w_out_layer = w_out
    wo_rows = w_out_stack.shape[1] // (n // tm)
    wo_sub = d // SEARCH_TRIPS

    row = lambda width: pl.BlockSpec((tm, width), lambda i: (i, 0))
    whole = lambda a: pl.BlockSpec(a.shape, lambda i: (0,) * a.ndim, pipeline_mode=pl.Buffered(1))
    tbl = pl.BlockSpec((3, tm, LANES), lambda i: (0, i % (seq // tm), 0))
    qk, v, qi, kw, wo = pl.pallas_call(
        functools.partial(_att_proj_kernel, half=half, half_i=half_i),
        grid=(n // tm,),
        in_specs=[row(d), whole(g_mix.reshape(1, d)),
                  pl.BlockSpec((n_main, d), lambda i: (0, 0), pipeline_mode=pl.Buffered(1)), whole(w_tail),
                  whole(gains), tbl, tbl, tbl,
                  pl.BlockSpec((None, wo_rows, d), lambda i: (w_out_layer, i, 0))],
        out_specs=[row(n_qk), row(n_v), row(n_qi), row(LANES),
                   pl.BlockSpec((SEARCH_TRIPS, wo_rows, wo_sub), lambda i: (0, i, 0))],
        out_shape=[jax.ShapeDtypeStruct((n, n_qk), bf), jax.ShapeDtypeStruct((n, n_v), bf),
                   jax.ShapeDtypeStruct((n, n_qi), bf), jax.ShapeDtypeStruct((n, LANES), jnp.float32),
                   jax.ShapeDtypeStruct((SEARCH_TRIPS, w_out_stack.shape[1], wo_sub), bf)],
        scratch_shapes=[pltpu.VMEM((tm, d), bf)],
        compiler_params=_params("parallel"),
        name="att_proj",
    )(h, g_mix.reshape(1, d), w_t, w_tail, gains, tab, tab_i, tab_t, w_out_stack)

    return _attention(qk, v, qi, kw, h, wo, batch, seq)


def _mlstm_layer(h, batch, seq, g_mix, w_t, w_tail_t, b_gate, h_gain, w_out):
    bf = jnp.bfloat16
    n, d = h.shape
    n_qk, n_v = 2 * ML_HEADS * ML_QK_DIM, ML_HEADS * ML_V_DIM
    n_g = 2 * ML_HEADS
    tm = min(PROJ_ROWS, seq)
    g2 = g_mix.reshape(1, d)
    wg_t = w_tail_t.astype(bf)
    n_main = n_qk + n_v + d

    row = lambda width: pl.BlockSpec((tm, width), lambda i: (i, 0))
    whole = lambda a: pl.BlockSpec(a.shape, lambda i: (0,) * a.ndim, pipeline_mode=pl.Buffered(1))
    qk, v, og, gates_t = pl.pallas_call(
        _ml_proj_kernel,
        grid=(n // tm,),
        in_specs=[row(d), whole(g2), pl.BlockSpec((n_main, d), lambda i: (0, 0), pipeline_mode=pl.Buffered(1)),
                  whole(wg_t), pl.BlockSpec((n_g, 1), lambda i: (0, 0))],
        out_specs=[row(n_qk), row(n_v), row(d), pl.BlockSpec((n_g, tm), lambda i: (0, i))],
        out_shape=[jax.ShapeDtypeStruct((n, n_qk), bf), jax.ShapeDtypeStruct((n, n_v), bf),
                   jax.ShapeDtypeStruct((n, d), jnp.float32), jax.ShapeDtypeStruct((n_g, n), jnp.float32)],
        scratch_shapes=[pltpu.VMEM((tm, d), bf)],
        compiler_params=_params("parallel"),
        name="ml_proj",
    )(h, g2, w_t, wg_t, b_gate.reshape(n_g, 1))
    y = _mlstm(qk, v, og, gates_t, h_gain, batch, seq)
    return _outproj(y, w_out.astype(bf), h)


def kernel(x, norm_mix, norm_ffn, att_w_in, att_q_gain, att_k_gain, att_w_out, ml_w_in, ml_b_gate, ml_h_gain,
           ml_w_out, ffn_w_up, ffn_w_down):
    batch, seq, d = x.shape
    bf = jnp.bfloat16
    h = x.reshape(batch * seq, d)
    depth = norm_mix.shape[0]
    att_w_in_t, ml_w_in_t = jnp.swapaxes(att_w_in, 1, 2), jnp.swapaxes(ml_w_in, 1, 2)
    def tail_t(w_t_f32, idx):
        aligned = w_t_f32.shape[1] // LANES * LANES
        have = ready["w_in"]
        return have[aligned:] if have.shape[0] > aligned else w_t_f32[idx, aligned:, :]

    ready = {"w_in": att_w_in_t[0].astype(bf)}
    for i in range(depth):
        j = i // 2
        if i % 2 == 0:
            w_out = (ready["w_out"][None], 0) if "w_out" in ready else (att_w_out, j)
            h = _attention_layer(h, batch, seq, norm_mix[i], ready["w_in"], tail_t(att_w_in_t, j), att_q_gain[j],
                                 att_k_gain[j], w_out)
        else:
            h = _mlstm_layer(h, batch, seq, norm_mix[i], ready["w_in"], tail_t(ml_w_in_t, j), ml_b_gate[j],
                             ml_h_gain[j], ready["w_out"])
        nxt = i + 1
        cast = []
        if nxt < depth:
            w_in_t, w_out = (att_w_in_t, att_w_out) if nxt % 2 == 0 else (ml_w_in_t, ml_w_out)
            cast = [(w_in_t, nxt // 2), (w_out, nxt // 2)]
        first, w_up, w_down = _ffn_first_tile(h, norm_ffn[i], ffn_w_up, ffn_w_down, i)
        h = lax.dynamic_update_slice(h, first, (0, 0))
        h, done = _ffn(h, norm_ffn[i], w_up, w_down, cast)
        if done:
            ready = dict(zip(("w_in", "w_out"), done))
    return h.reshape(batch, seq, d)
```

```python
import functools

import jax
import jax.numpy as jnp
from jax import lax
from jax.experimental import pallas as pl
from jax.experimental.pallas import tpu as pltpu

D_MODEL = 2048
ATT_HEADS = 16
ATT_KV_HEADS = 4
ATT_HEAD_DIM = 128
IDX_HEADS = 16
IDX_HEAD_DIM = 64
TOPK_MAX = 256
ML_HEADS = 8
ML_V_DIM = 256
ML_QK_DIM = 128
GATE_SOFTCAP = 15.0
ROPE_THETA = 500000.0
ROT_FRAC = 4
EPS = 1e-6

LANES = 128
SUBLANES = 8
BF16_ROWS = 16
VMEM_LIMIT = 60 * 1024 * 1024
INT_MIN = -(2 ** 31)
NEG_BIG = -1e30

LOG2_E = 1.4426950408889634

ATT_Q_BLOCK = 256
ATT_KEY_CHUNK = 256
SEARCH_UNROLL = 4
SEARCH_TRIPS = 32 // SEARCH_UNROLL
ML_CHUNK = 256
ML_HEADS_PER_STEP = 4
ML_CHUNK_UNROLL = 2
PROJ_ROWS = 512
PROJ_SUB = 512
FFN_SUB = 256


def _params(*sem):
    return pltpu.CompilerParams(dimension_semantics=sem, vmem_limit_bytes=VMEM_LIMIT)


def _store_rmsnorm(x_ref, g_ref, hn_ref):
    x = x_ref[...]
    ms = jnp.mean(x * x, axis=-1, keepdims=True)
    hn_ref[...] = (x * lax.rsqrt(ms + EPS) * g_ref[...]).astype(hn_ref.dtype)

def _rope_tables(seq, head_dim, heads_per_vreg):
    rot = head_dim // ROT_FRAC
    half = rot // 2
    inv_freq = ROPE_THETA ** (-2.0 * jnp.arange(half, dtype=jnp.float32) / rot)
    ang = jnp.arange(seq).astype(jnp.float32)[:, None] * inv_freq[None, :]
    cos, sin = jnp.cos(ang), jnp.sin(ang)
    ones = jnp.ones((seq, head_dim - rot), jnp.float32)
    zeros_h = jnp.zeros((seq, half), jnp.float32)
    zeros_r = jnp.zeros((seq, head_dim - rot), jnp.float32)
    c = jnp.concatenate([cos, cos, ones], axis=1)
    s1 = jnp.concatenate([-sin, zeros_h, zeros_r], axis=1)
    s2 = jnp.concatenate([zeros_h, sin, zeros_r], axis=1)
    rep = lambda t: jnp.tile(t, (1, heads_per_vreg))
    return rep(c), rep(s1), rep(s2), half


def _rope(y, c, s1, s2, half):
    return y * c + pltpu.roll(y, LANES - half, 1) * s1 + pltpu.roll(y, half, 1) * s2


def _att_proj_kernel(x_ref, g_ref, w_ref, wt_ref, gain_ref, tab_ref, tabi_ref, tabt_ref, wo_src_ref,
                     qk_ref, v_ref, qi_ref, kw_ref, wo_ref, hn_ref, *, half, half_i):
    _store_rmsnorm(x_ref, g_ref, hn_ref)
    n_qk, n_v, n_qi = qk_ref.shape[1], v_ref.shape[1], qi_ref.shape[1]
    n_q = ATT_HEADS * ATT_HEAD_DIM

    def cols(start, width):
        return lax.dot_general(hn_ref[...], w_ref[start:start + width, :], (((1,), (1,)), ((), ())),
                               preferred_element_type=jnp.float32)

    for s in range(n_qk // PROJ_SUB):
        acc = cols(s * PROJ_SUB, PROJ_SUB)
        gain = gain_ref[0 if s * PROJ_SUB < n_q else 1]
        for grp in range(PROJ_SUB // LANES):
            xg = acc[:, grp * LANES:(grp + 1) * LANES]
            ms = jnp.mean(xg * xg, axis=-1, keepdims=True)
            y = _rope(xg * lax.rsqrt(ms + EPS) * gain, tab_ref[0], tab_ref[1], tab_ref[2], half)
            qk_ref[:, s * PROJ_SUB + grp * LANES:s * PROJ_SUB + (grp + 1) * LANES] = y.astype(qk_ref.dtype)

    v_ref[...] = cols(n_qk, n_v).astype(v_ref.dtype)

    for s in range(n_qi // PROJ_SUB):
        acc = cols(n_qk + n_v + s * PROJ_SUB, PROJ_SUB)
        for grp in range(PROJ_SUB // LANES):
            y = _rope(acc[:, grp * LANES:(grp + 1) * LANES], tabi_ref[0], tabi_ref[1], tabi_ref[2], half_i)
            qi_ref[:, s * PROJ_SUB + grp * LANES:s * PROJ_SUB + (grp + 1) * LANES] = y.astype(qi_ref.dtype)

    tail = lax.dot_general(hn_ref[...], wt_ref[...], (((1,), (1,)), ((), ())), preferred_element_type=jnp.float32)
    kw_ref[...] = _rope(tail, tabt_ref[0], tabt_ref[1], tabt_ref[2], half_i)

    sub = wo_ref.shape[2]
    for t in range(wo_ref.shape[0]):
        wo_ref[t] = wo_src_ref[:, t * sub:(t + 1) * sub].astype(wo_ref.dtype)


def _ml_proj_kernel(x_ref, g_ref, w_ref, wg_ref, b_ref, qk_ref, v_ref, og_ref, gates_ref, hn_ref):
    _store_rmsnorm(x_ref, g_ref, hn_ref)
    n_qk, n_v = qk_ref.shape[1], v_ref.shape[1]
    for s in range((n_qk + n_v + og_ref.shape[1]) // PROJ_SUB):
        acc = lax.dot_general(hn_ref[...], w_ref[s * PROJ_SUB:(s + 1) * PROJ_SUB, :], (((1,), (1,)), ((), ())),
                              preferred_element_type=jnp.float32)
        start = s * PROJ_SUB
        if start < n_qk // 2:
            qk_ref[:, start:start + PROJ_SUB] = acc.astype(qk_ref.dtype)
        elif start < n_qk:
            qk_ref[:, start:start + PROJ_SUB] = (acc * ML_QK_DIM ** -0.5).astype(qk_ref.dtype)
        elif start < n_qk + n_v:
            v_ref[:, start - n_qk:start - n_qk + PROJ_SUB] = acc.astype(v_ref.dtype)
        else:
            og_ref[:, start - n_qk - n_v:start - n_qk - n_v + PROJ_SUB] = acc

    g = lax.dot_general(wg_ref[...], hn_ref[...], (((1,), (1,)), ((), ())),
                        preferred_element_type=jnp.float32)
    g = g + b_ref[...]
    g = GATE_SOFTCAP * jnp.tanh(g / GATE_SOFTCAP)
    logf = jnp.minimum(g, 0.0) - jnp.log1p(jnp.exp(-jnp.abs(g)))
    is_forget = lax.broadcasted_iota(jnp.int32, g.shape, 0) >= ML_HEADS
    gates_ref[...] = jnp.where(is_forget, logf, g)


def _key_to_f32(key):
    return pltpu.bitcast(jnp.where(key < 0, key ^ jnp.int32(0x7FFFFFFF), key), jnp.float32)


def _select_topk(score_ref, bias_ref, *, topk, n_chunks, side_work):
    _, tq, ck = score_ref.shape
    kf = jnp.float32(topk)

    def count(pred, thr):
        part = jnp.zeros((tq, LANES), jnp.float32)
        for c in range(n_chunks):
            for j in range(ck // LANES):
                part = part + jnp.where(pred(score_ref[c, :, j * LANES:(j + 1) * LANES], thr), 1.0, 0.0)
        return jnp.sum(part, axis=1, keepdims=True)

    key_ninf = INT_MIN + 0x7FFFFF
    key_pinf = 0x7F800000

    def search(it, tau):
        cand = tau + lax.shift_left(jnp.int32(1), jnp.int32(31) - it)
        in_range = jnp.logical_and(cand > tau, cand <= key_pinf)
        cnt = count(jnp.greater_equal, _key_to_f32(cand))
        return jnp.where(jnp.logical_and(in_range, cnt >= kf), cand, tau)

    def trip(t, tau):
        for u in range(SEARCH_UNROLL):
            tau = search(t * SEARCH_UNROLL + u, tau)
        side_work(t)
        return tau

    tau = lax.fori_loop(0, SEARCH_TRIPS, trip, jnp.full((tq, 1), key_ninf, jnp.int32))
    thr = _key_to_f32(tau)
    finite = lambda s, th: jnp.logical_and(s >= th, s > -jnp.inf)
    n_ge = count(finite, thr)

    @pl.when(jnp.max(n_ge) <= kf)
    def _():
        for c in range(n_chunks):
            bias_ref[c] = jnp.where(finite(score_ref[c], thr), 0.0, NEG_BIG)

    @pl.when(jnp.max(n_ge) > kf)
    def _():
        need = kf - count(jnp.greater, thr)
        r = lax.broadcasted_iota(jnp.int32, (ck, ck), 0)
        col = lax.broadcasted_iota(jnp.int32, (ck, ck), 1)
        before = jnp.where(r < col, 1.0, 0.0).astype(jnp.bfloat16)
        seen = jnp.zeros((tq, 1), jnp.float32)
        for c in range(n_chunks):
            s = score_ref[c]
            eq = jnp.logical_and(s == thr, s > -jnp.inf)
            e = jnp.where(eq, 1.0, 0.0)
            prefix = jnp.dot(e.astype(jnp.bfloat16), before, preferred_element_type=jnp.float32) + seen
            seen = seen + jnp.sum(e, axis=1, keepdims=True)
            keep = jnp.logical_or(s > thr, jnp.logical_and(eq, prefix < need))
            bias_ref[c] = jnp.where(keep, 0.0, NEG_BIG)


def _for_each_chunk(n, body):
    def pair(p, carry):
        body(2 * p)
        body(2 * p + 1)
        return carry

    lax.fori_loop(0, n // 2, pair, 0)
    pl.when(n % 2 == 1)(lambda: body(n - 1))


def _attention_kernel(q_ref, k_ref, v_ref, qi_ref, wq_ref, kk_ref, x_ref, wo_ref, h_ref, *scratch, topk):
    o_ref, proj_ref = scratch[-2:]
    i = pl.program_id(1)
    last = pl.num_programs(1) - 1
    sub = wo_ref.shape[2]
    assert wo_ref.shape[0] == SEARCH_TRIPS

    @pl.when(jnp.logical_and(pl.program_id(0) == 0, i == 0))
    def _():
        o_ref[...] = jnp.zeros(o_ref.shape, o_ref.dtype)

    def project(t):
        proj_ref[t] = jnp.dot(o_ref[...], wo_ref[t], preferred_element_type=jnp.float32)

    def finish_previous():
        for t in range(wo_ref.shape[0]):
            h_ref[:, t * sub:(t + 1) * sub] = x_ref[:, t * sub:(t + 1) * sub] + proj_ref[t]

    @pl.when(i < last)
    def _():
        _attention_block(q_ref, k_ref, v_ref, qi_ref, wq_ref, kk_ref, *scratch[:-1], topk=topk, q0=i * q_ref.shape[0],
                         project=project, finish_previous=finish_previous)

    @pl.when(i == last)
    def _():
        for t in range(wo_ref.shape[0]):
            project(t)
        finish_previous()


def _attention_block(q_ref, k_ref, v_ref, qi_ref, wq_ref, kk_ref,
                     qs_ref, wb_ref, score_ref, bias_ref, q4_ref, s_ref, m_ref, acc_ref, o_ref, *,
                     topk, q0, project, finish_previous):
    tq = q_ref.shape[0]
    ck = ATT_KEY_CHUNK
    half = tq // 2
    nt = (((1,), (1,)), ((), ()))
    n_chunks = (q0 + tq) // ck

    for r in range(2):
        rows = slice(r * half, (r + 1) * half)
        for h in range(IDX_HEADS):
            qs_ref[r, h * half:(h + 1) * half, :] = (
                qi_ref[rows, h * IDX_HEAD_DIM:(h + 1) * IDX_HEAD_DIM].astype(jnp.bfloat16))
            wb_ref[r * IDX_HEADS + h] = jnp.broadcast_to(
                wq_ref[rows, IDX_HEAD_DIM + h:IDX_HEAD_DIM + h + 1], (half, LANES))

    def score_chunk(c):
        off = pl.multiple_of(c * ck, ck)
        ki = kk_ref[pl.ds(off, ck), :].astype(jnp.bfloat16)[:, :IDX_HEAD_DIM]
        spos = off + lax.broadcasted_iota(jnp.int32, (half, LANES), 1)
        for r in range(2):
            d = lax.dot_general(qs_ref[r], ki, nt, preferred_element_type=jnp.float32)
            tpos = q0 + r * half + lax.broadcasted_iota(jnp.int32, (half, LANES), 0)
            for j in range(ck // LANES):
                acc = jnp.zeros((half, LANES), jnp.float32)
                for h in range(IDX_HEADS):
                    dh = d[h * half:(h + 1) * half, j * LANES:(j + 1) * LANES]
                    acc = acc + jnp.maximum(dh, 0.0) * wb_ref[r * IDX_HEADS + h]
                score_ref[c, r * half:(r + 1) * half, j * LANES:(j + 1) * LANES] = (
                    jnp.where(spos + j * LANES <= tpos, acc, -jnp.inf))

    _for_each_chunk(n_chunks, score_chunk)

    for n in range(1, score_ref.shape[0] + 1):
        pl.when(n_chunks == n)(functools.partial(_select_topk, score_ref, bias_ref, topk=topk, n_chunks=n,
                                                 side_work=project))
    finish_previous()

    exp2_scale = ATT_HEAD_DIM ** -0.5 * LOG2_E
    rep = ATT_HEADS // ATT_KV_HEADS
    pair = s_ref.shape[0]
    head_cols = lambda h: slice(h * ATT_HEAD_DIM, (h + 1) * ATT_HEAD_DIM)
    for g0 in range(0, ATT_KV_HEADS, pair):
        for u in range(pair):
            for r in range(rep):
                q4_ref[u, r * tq:(r + 1) * tq, :] = q_ref[:, head_cols((g0 + u) * rep + r)]
        m_ref[...] = jnp.full(m_ref.shape, NEG_BIG, jnp.float32)

        def logits_chunk(c):
            off = pl.multiple_of(c * ck, ck)
            for u in range(pair):
                s = lax.dot_general(q4_ref[u], k_ref[pl.ds(off, ck), head_cols(g0 + u)], nt,
                                    preferred_element_type=jnp.float32)
                s = ((s.reshape(rep, tq, ck) + bias_ref[c][None]) * exp2_scale).reshape(rep * tq, ck)
                s_ref[u, c] = s
                m = m_ref[u]
                for j in range(ck // LANES):
                    m = jnp.maximum(m, s[:, j * LANES:(j + 1) * LANES])
                m_ref[u] = m

        _for_each_chunk(n_chunks, logits_chunk)
        for u in range(pair):
            m_ref[u] = jnp.broadcast_to(jnp.max(m_ref[u], axis=1, keepdims=True), m_ref.shape[1:])
        acc_ref[...] = jnp.zeros(acc_ref.shape, jnp.float32)
        ones = jnp.ones((ck, LANES), jnp.bfloat16)

        def pv_chunk(c):
            off = pl.multiple_of(c * ck, ck)
            for u in range(pair):
                p = jnp.exp2(s_ref[u, c] - jnp.concatenate([m_ref[u]] * (ck // LANES), axis=1))
                v1 = jnp.concatenate([v_ref[pl.ds(off, ck), head_cols(g0 + u)], ones], axis=1)
                acc_ref[u] += jnp.dot(p.astype(jnp.bfloat16), v1, preferred_element_type=jnp.float32)

        _for_each_chunk(n_chunks, pv_chunk)
        for u in range(pair):
            o4 = acc_ref[u, :, :ATT_HEAD_DIM] / acc_ref[u, :, ATT_HEAD_DIM:]
            for r in range(rep):
                o_ref[:, head_cols((g0 + u) * rep + r)] = o4[r * tq:(r + 1) * tq].astype(o_ref.dtype)


def _attention(qk, v, qi, kw, x, wo, batch, seq):
    n, d = x.shape
    tq, ck = ATT_Q_BLOCK, ATT_KEY_CHUNK
    nq = seq // tq
    rep = ATT_HEADS // ATT_KV_HEADS
    pair = 2
    topk = min(TOPK_MAX, seq // 4)
    q_w, kv_w = ATT_HEADS * ATT_HEAD_DIM, ATT_KV_HEADS * ATT_HEAD_DIM
    qi_w = IDX_HEADS * IDX_HEAD_DIM
    sub = wo.shape[2]
    cur =lambda b, i: (b * nq + jnp.minimum(i, nq - 1), 0)
    prev = lambda b, i: (b * nq + jnp.maximum(i - 1, 0), 0)
    once = dict(pipeline_mode=pl.Buffered(1))
    return pl.pallas_call(
        functools.partial(_attention_kernel, topk=topk),
        grid=(batch, nq + 1),
        in_specs=[pl.BlockSpec((tq, q_w), cur),
                  pl.BlockSpec((seq, kv_w), lambda b, i: (b, q_w // kv_w)),
                  pl.BlockSpec((seq, kv_w), lambda b, i: (b, 0)),
                  pl.BlockSpec((tq, qi_w), cur),
                  pl.BlockSpec((tq, LANES), cur),
                  pl.BlockSpec((seq, LANES), lambda b, i: (b, 0), **once),
                  pl.BlockSpec((tq, d), prev),
                  pl.BlockSpec(wo.shape, lambda b, i: (0, 0, 0), **once)],
        out_specs=pl.BlockSpec((tq, d), prev),
        out_shape=jax.ShapeDtypeStruct((n, d), jnp.float32),
        scratch_shapes=[pltpu.VMEM((2, IDX_HEADS * tq // 2, IDX_HEAD_DIM), jnp.bfloat16),
                        pltpu.VMEM((2 * IDX_HEADS, tq // 2, LANES), jnp.float32),
                        pltpu.VMEM((seq // ck, tq, ck), jnp.float32),
                        pltpu.VMEM((seq // ck, tq, ck), jnp.float32),
                        pltpu.VMEM((pair, rep * tq, ATT_HEAD_DIM), jnp.bfloat16),
                        pltpu.VMEM((pair, seq // ck, rep * tq, ck), jnp.float32),
                        pltpu.VMEM((pair, rep * tq, LANES), jnp.float32),
                        pltpu.VMEM((pair, rep * tq, ATT_HEAD_DIM + LANES), jnp.float32),
                        pltpu.VMEM((tq, q_w), jnp.bfloat16),
                        pltpu.VMEM((SEARCH_TRIPS, tq, sub), jnp.float32)],
        compiler_params=_params("arbitrary", "arbitrary"),
        name="dsa_attention",
    )(qk, qk, v, qi, kw, kw, x, wo)


def _mlstm_kernel(q_ref, k_ref, v_ref, og_ref, ig_ref, lf_ref, hg_ref, o_ref):
    L = ML_CHUNK
    nc = q_ref.shape[0] // L
    dk, dv = ML_QK_DIM, ML_V_DIM
    heads = q_ref.shape[1] // dk
    row = lax.broadcasted_iota(jnp.int32, (L, L), 0)
    col = lax.broadcasted_iota(jnp.int32, (L, L), 1)
    tri = col <= row
    eye = col == row
    gain = hg_ref[...]

    def to_col(x_row):
        return jnp.sum(jnp.where(eye, jnp.broadcast_to(x_row, (L, L)), 0.0), axis=1, keepdims=True)

    def chunk(c, carry):
        return tuple(head_chunk(hd, c, carry[hd]) for hd in range(heads))

    def head_chunk(hd, c, state):
        C, n_row, m = state
        off = pl.multiple_of(c * L, L)
        qc = q_ref[pl.ds(off, L), hd * dk:(hd + 1) * dk]
        kc = k_ref[pl.ds(off, L), hd * dk:(hd + 1) * dk]
        vc = v_ref[pl.ds(off, L), hd * dv:(hd + 1) * dv]
        ig_r = ig_ref[hd, c]
        lf_r = lf_ref[hd, c]

        lf_b = jnp.broadcast_to(lf_r, (L, L))
        b_col = jnp.sum(jnp.where(tri, lf_b, 0.0), axis=1, keepdims=True)
        lf_col = to_col(lf_r)
        b_row = jnp.sum(jnp.where(row <= col, jnp.broadcast_to(lf_col, (L, L)), 0.0), axis=0, keepdims=True)
        a = jnp.sum(lf_r, axis=1, keepdims=True)

        g_row = a - b_row + ig_r
        m_loc = jnp.max(g_row, axis=1, keepdims=True)
        w_row = jnp.exp(g_row - m_loc)
        w_col = to_col(w_row)
        c_loc = lax.dot_general(kc, (w_col * vc.astype(jnp.float32)).astype(jnp.bfloat16),
                                (((0,), (0,)), ((), ())), preferred_element_type=jnp.float32)
        n_loc = jnp.dot(jnp.broadcast_to(w_row, (SUBLANES, L)).astype(jnp.bfloat16), kc,
                        preferred_element_type=jnp.float32)[0:1]

        dm = jnp.where(tri, b_col - b_row + ig_r, -jnp.inf)
        inter = b_col + m
        m_t = jnp.maximum(inter, jnp.max(dm, axis=1, keepdims=True))
        s = lax.dot_general(qc, kc, (((1,), (1,)), ((), ())), preferred_element_type=jnp.float32)
        s = s * jnp.exp(dm - m_t)
        s_inter = jnp.exp(inter - m_t)
        qf = qc.astype(jnp.float32)
        num = (jnp.dot(s.astype(jnp.bfloat16), vc, preferred_element_type=jnp.float32)
               + s_inter * jnp.dot(qc, C.astype(jnp.bfloat16), preferred_element_type=jnp.float32))
        den = jnp.sum(s, axis=1, keepdims=True) + s_inter * jnp.sum(qf * n_row, axis=1, keepdims=True)
        h = num / jnp.maximum(jnp.abs(den), jnp.exp(-m_t))

        ms = jnp.mean(h * h, axis=-1, keepdims=True)
        hn = h * lax.rsqrt(ms + EPS) * gain
        og = og_ref[pl.ds(off, L), hd * dv:(hd + 1) * dv]
        o_ref[pl.ds(off, L), hd * dv:(hd + 1) * dv] = (jax.nn.sigmoid(og) * hn).astype(o_ref.dtype)

        m_new = jnp.maximum(a + m, m_loc)
        s_old = jnp.exp(a + m - m_new)
        s_new = jnp.exp(m_loc - m_new)
        return (s_old * C + s_new * c_loc, s_old * n_row + s_new * n_loc, m_new)

    init = (jnp.zeros((dk, dv), jnp.float32), jnp.zeros((1, dk), jnp.float32), jnp.zeros((1, 1), jnp.float32))
    lax.fori_loop(0, nc, chunk, (init,) * heads, unroll=ML_CHUNK_UNROLL)


def _mlstm(qk, v, og, gates_t, h_gain, batch, seq):
    n = qk.shape[0]
    L = ML_CHUNK
    nc = seq // L
    hb = ML_HEADS_PER_STEP
    nhb = ML_HEADS // hb
    g4 = gates_t.reshape(2 * ML_HEADS, batch * nc, 1, L)
    return pl.pallas_call(
        _mlstm_kernel,
        grid=(batch, nhb),
        in_specs=[pl.BlockSpec((seq, hb * ML_QK_DIM), lambda b, h: (b, h)),
                  pl.BlockSpec((seq, hb * ML_QK_DIM), lambda b, h: (b, nhb + h)),
                  pl.BlockSpec((seq, hb * ML_V_DIM), lambda b, h: (b, h)),
                  pl.BlockSpec((seq, hb * ML_V_DIM), lambda b, h: (b, h)),
                  pl.BlockSpec((hb, nc, 1, L), lambda b, h: (h, b, 0, 0)),
                  pl.BlockSpec((hb, nc, 1, L), lambda b, h: (nhb + h, b, 0, 0)),
                  pl.BlockSpec((1, ML_V_DIM), lambda b, h: (0, 0))],
        out_specs=pl.BlockSpec((seq, hb * ML_V_DIM), lambda b, h: (b, h)),
        out_shape=jax.ShapeDtypeStruct((n, ML_HEADS * ML_V_DIM), jnp.bfloat16),
        compiler_params=_params("parallel", "parallel"),
        name="mlstm",
    )(qk, qk, v, og, g4, g4, h_gain.reshape(1, ML_V_DIM))


def _outproj_kernel(a_ref, w_ref, x_ref, o_ref):
    o_ref[...] = x_ref[...] + jnp.dot(a_ref[...], w_ref[...], preferred_element_type=jnp.float32)


def _outproj(a, w, x, tm=512):
    n, d = x.shape
    k = a.shape[1]
    return pl.pallas_call(
        _outproj_kernel,
        grid=(n // tm,),
        in_specs=[pl.BlockSpec((tm, k), lambda i: (i, 0)),
                  pl.BlockSpec((k, d), lambda i: (0, 0)),
                  pl.BlockSpec((tm, d), lambda i: (i, 0))],
        out_specs=pl.BlockSpec((tm, d), lambda i: (i, 0)),
        out_shape=jax.ShapeDtypeStruct((n, d), jnp.float32),
        compiler_params=_params("parallel"),
        name="outproj",
    )(a, w, x)


def _ffn_step(first_chunk, x_ref, g_ref, wu, wd, o_ref, hn_ref):
    @pl.when(first_chunk)
    def _():
        x = x_ref[...]
        ms = jnp.mean(x * x, axis=-1, keepdims=True)
        hn_ref[...] = (x * lax.rsqrt(ms + EPS) * g_ref[...]).astype(hn_ref.dtype)
        o_ref[...] = x

    acts = []
    for s in range(wu.shape[1] // FFN_SUB):
        u = jnp.dot(hn_ref[...], wu[:, s * FFN_SUB:(s + 1) * FFN_SUB], preferred_element_type=jnp.float32)
        acts.append(jnp.square(jnp.maximum(u, 0.0)).astype(jnp.bfloat16))
    o_ref[...] += jnp.dot(jnp.concatenate(acts, axis=1), wd[...], preferred_element_type=jnp.float32)


def _ffn_first_tile_kernel(x_ref, g_ref, wu32_ref, wd32_ref, o_ref, wu16_ref, wd16_ref, hn_ref):
    wu16_ref[...] = wu32_ref[...].astype(wu16_ref.dtype)
    wd16_ref[...] = wd32_ref[...].astype(wd16_ref.dtype)
    _ffn_step(pl.program_id(0) == 0, x_ref, g_ref, wu16_ref, wd16_ref, o_ref, hn_ref)


def _ffn_kernel(x_ref, g_ref, wu_ref, wd_ref, *rest, n_cast, first_done):
    cast_src, o_ref, cast_dst, hn_ref = rest[:n_cast], rest[n_cast], rest[n_cast + 1:-1], rest[-1]
    i, j = pl.program_id(0), pl.program_id(1)
    step = i * pl.num_programs(1) + j
    for src, dst in zip(cast_src, cast_dst):
        dst[...] = src[...].astype(dst.dtype)

    if first_done:
        @pl.when(step == 0)
        def _():
            o_ref[...] = x_ref[...]

        @pl.when(i > 0)
        def _():
            _ffn_step(j == 0, x_ref, g_ref, wu_ref, wd_ref, o_ref, hn_ref)
    else:
        _ffn_step(j == 0, x_ref, g_ref, wu_ref, wd_ref, o_ref, hn_ref)


FFN_ROWS, FFN_CHUNK = 1024, 512


def _ffn_first_tile(x, g, w_up32, w_down32, layer):
    n, d = x.shape
    tm, tf = min(FFN_ROWS, n), FFN_CHUNK
    f = w_up32.shape[2]
    return pl.pallas_call(
        _ffn_first_tile_kernel,
        grid=(f // tf,),
        in_specs=[pl.BlockSpec((tm, d), lambda j: (0, 0), pipeline_mode=pl.Buffered(1)),
                  pl.BlockSpec((1, d), lambda j: (0, 0)),
                  pl.BlockSpec((None, d, tf), lambda j: (layer, 0, j)),
                  pl.BlockSpec((None, tf, d), lambda j: (layer, j, 0))],
        out_specs=[pl.BlockSpec((tm, d), lambda j: (0, 0)),
                   pl.BlockSpec((d, tf), lambda j: (0, j)),
                   pl.BlockSpec((tf, d), lambda j: (j, 0))],
        out_shape=[jax.ShapeDtypeStruct((tm, d), jnp.float32), jax.ShapeDtypeStruct((d, f), jnp.bfloat16),
                   jax.ShapeDtypeStruct((f, d), jnp.bfloat16)],
        scratch_shapes=[pltpu.VMEM((tm, d), jnp.bfloat16)],
        compiler_params=_params("arbitrary"),
        name="ffn_first_tile",
    )(x, g.reshape(1, d), w_up32, w_down32)


def _ffn(x, g, w_up, w_down, cast=(), first_done=False):
    n, d = x.shape
    tm, tf = min(FFN_ROWS, n), FFN_CHUNK
    f = w_up.shape[1]
    ni, nj = n // tm, f // tf
    chunk = (lambda i, j: jnp.where(i > 0, j, 0)) if first_done else (lambda i, j: j)
    in_specs, out_specs, out_shapes = [], [], []
    for a, layer in cast:
        rows, cols = a.shape[1:]
        slab = rows // (ni * nj) // BF16_ROWS * BF16_ROWS
        in_specs.append(pl.BlockSpec((None, slab, cols), lambda i, j, layer=layer: (layer, i * nj + j, 0)))
        out_specs.append(pl.BlockSpec((slab, cols), lambda i, j: (i * nj + j, 0)))
        out_shapes.append(jax.ShapeDtypeStruct((slab * ni * nj, cols), jnp.bfloat16))
    out = pl.pallas_call(
        functools.partial(_ffn_kernel, n_cast=len(cast), first_done=first_done),
        grid=(ni, nj),
        in_specs=[pl.BlockSpec((tm, d), lambda i, j: (i, 0)),
                  pl.BlockSpec((1, d), lambda i, j: (0, 0)),
                  pl.BlockSpec((d, tf), lambda i, j: (0, chunk(i, j))),
                  pl.BlockSpec((tf, d), lambda i, j: (chunk(i, j), 0))] + in_specs,
        out_specs=[pl.BlockSpec((tm, d), lambda i, j: (i, 0))] + out_specs,
        out_shape=[jax.ShapeDtypeStruct((n, d), jnp.float32)] + out_shapes,
        scratch_shapes=[pltpu.VMEM((tm, d), jnp.bfloat16)],
        compiler_params=_params("arbitrary", "arbitrary"),
        name="ffn",
    )(x, g.reshape(1, d), w_up, w_down, *[a for a, _ in cast])
    return out[0], out[1:]


def _attention_layer(h, batch, seq, g_mix, w_t, w_tail_t, q_gain, k_gain, w_out):
    bf = jnp.bfloat16
    n, d = h.shape
    n_qk = (ATT_HEADS + ATT_KV_HEADS) * ATT_HEAD_DIM
    n_v = ATT_KV_HEADS * ATT_HEAD_DIM
    n_qi = IDX_HEADS * IDX_HEAD_DIM
    tm = min(PROJ_ROWS, seq)

    c, s1, s2, half = _rope_tables(seq, ATT_HEAD_DIM, 1)
    ci, s1i, s2i, half_i = _rope_tables(seq, IDX_HEAD_DIM, LANES // IDX_HEAD_DIM)
    lane = jnp.arange(LANES)
    is_k = (lane < IDX_HEAD_DIM)[None, :]
    is_w = jnp.logical_and(lane >= IDX_HEAD_DIM, lane < IDX_HEAD_DIM + IDX_HEADS)[None, :]
    w_scale = IDX_HEADS ** -0.5 * IDX_HEAD_DIM ** -0.5
    tab = jnp.stack([c, s1, s2])
    tab_i = jnp.stack([ci, s1i, s2i])
    tab_t = jnp.stack([jnp.where(is_k, ci, jnp.where(is_w, w_scale, 0.0)),
                       jnp.where(is_k, s1i, 0.0), jnp.where(is_k, s2i, 0.0)])
    gains = jnp.stack([q_gain, k_gain]).reshape(2, 1, ATT_HEAD_DIM)
    n_main = n_qk + n_v + n_qi
    w_tail = jnp.pad(w_tail_t.astype(bf), ((0, LANES - w_tail_t.shape[0]), (0, 0)))

    w_out_stack, w_out_layer = w_out
    wo_rows = w_out_stack.shape[1] // (n // tm)
    wo_sub = d // SEARCH_TRIPS

    row = lambda width: pl.BlockSpec((tm, width), lambda i: (i, 0))
    whole = lambda a: pl.BlockSpec(a.shape, lambda i: (0,) * a.ndim, pipeline_mode=pl.Buffered(1))
    tbl = pl.BlockSpec((3, tm, LANES), lambda i: (0, i % (seq // tm), 0))
    qk, v, qi, kw, wo = pl.pallas_call(
        functools.partial(_att_proj_kernel, half=half, half_i=half_i),
        grid=(n // tm,),
        in_specs=[row(d), whole(g_mix.reshape(1, d)),
                  pl.BlockSpec((n_main, d), lambda i: (0, 0), pipeline_mode=pl.Buffered(1)), whole(w_tail),
                  whole(gains), tbl, tbl, tbl,
                  pl.BlockSpec((None, wo_rows, d), lambda i: (w_out_layer, i, 0))],
        out_specs=[row(n_qk), row(n_v), row(n_qi), row(LANES),
                   pl.BlockSpec((SEARCH_TRIPS, wo_rows, wo_sub), lambda i: (0, i, 0))],
        out_shape=[jax.ShapeDtypeStruct((n, n_qk), bf), jax.ShapeDtypeStruct((n, n_v), bf),
                   jax.ShapeDtypeStruct((n, n_qi), bf), jax.ShapeDtypeStruct((n, LANES), jnp.float32),
                   jax.ShapeDtypeStruct((SEARCH_TRIPS, w_out_stack.shape[1], wo_sub), bf)],
        scratch_shapes=[pltpu.VMEM((tm, d), bf)],
        compiler_params=_params("parallel"),
        name="att_proj",
    )(h, g_mix.reshape(1, d), w_t, w_tail, gains, tab, tab_i, tab_t, w_out_stack)

    return _attention(qk, v, qi, kw, h, wo, batch, seq)


def _mlstm_layer(h, batch, seq, g_mix, w_t, w_tail_t, b_gate, h_gain, w_out):
    bf = jnp.bfloat16
    n, d = h.shape
    n_qk, n_v = 2 * ML_HEADS * ML_QK_DIM, ML_HEADS * ML_V_DIM
    n_g = 2 * ML_HEADS
    tm = min(PROJ_ROWS, seq)
    g2 = g_mix.reshape(1, d)
    wg_t = w_tail_t.astype(bf)
    n_main = n_qk + n_v + d

    row = lambda width: pl.BlockSpec((tm, width), lambda i: (i, 0))
    whole = lambda a: pl.BlockSpec(a.shape, lambda i: (0,) * a.ndim, pipeline_mode=pl.Buffered(1))
    qk, v, og, gates_t = pl.pallas_call(
        _ml_proj_kernel,
        grid=(n // tm,),
        in_specs=[row(d), whole(g2), pl.BlockSpec((n_main, d), lambda i: (0, 0), pipeline_mode=pl.Buffered(1)),
                  whole(wg_t), pl.BlockSpec((n_g, 1), lambda i: (0, 0))],
        out_specs=[row(n_qk), row(n_v), row(d), pl.BlockSpec((n_g, tm), lambda i: (0, i))],
        out_shape=[jax.ShapeDtypeStruct((n, n_qk), bf), jax.ShapeDtypeStruct((n, n_v), bf),
                   jax.ShapeDtypeStruct((n, d), jnp.float32), jax.ShapeDtypeStruct((n_g, n), jnp.float32)],
        scratch_shapes=[pltpu.VMEM((tm, d), bf)],
        compiler_params=_params("parallel"),
        name="ml_proj",
    )(h, g2, w_t, wg_t, b_gate.reshape(n_g, 1))
    y = _mlstm(qk, v, og, gates_t, h_gain, batch, seq)
    return _outproj(y, w_out.astype(bf), h)


def kernel(x, norm_mix, norm_ffn, att_w_in, att_q_gain, att_k_gain, att_w_out, ml_w_in, ml_b_gate, ml_h_gain,
           ml_w_out, ffn_w_up, ffn_w_down):
    batch, seq, d = x.shape
    bf = jnp.bfloat16
    h = x.reshape(batch * seq, d)
    depth = norm_mix.shape[0]
    att_w_in_t, ml_w_in_t = jnp.swapaxes(att_w_in, 1, 2), jnp.swapaxes(ml_w_in, 1, 2)
    def tail_t(w_t_f32, idx):
        aligned = w_t_f32.shape[1] // LANES * LANES
        have = ready["w_in"]
        return have[aligned:] if have.shape[0] > aligned else w_t_f32[idx, aligned:, :]

    ready = {"w_in": att_w_in_t[0].astype(bf)}
    for i in range(depth):
        j = i // 2
        if i % 2 == 0:
            w_out = (ready["w_out"][None], 0) if "w_out" in ready else (att_w_out, j)
            h = _attention_layer(h, batch, seq, norm_mix[i], ready["w_in"], tail_t(att_w_in_t, j), att_q_gain[j],
                                 att_k_gain[j], w_out)
        else:
            h = _mlstm_layer(h, batch, seq, norm_mix[i], ready["w_in"], tail_t(ml_w_in_t, j), ml_b_gate[j],
                             ml_h_gain[j], ready["w_out"])
        nxt = i + 1
        cast = []
        if nxt < depth:
            w_in_t, w_out = (att_w_in_t, att_w_out) if nxt % 2 == 0 else (ml_w_in_t, ml_w_out)
            cast = [(ffn_w_up, nxt), (ffn_w_down, nxt), (w_in_t, nxt // 2), (w_out, nxt // 2)]
        if "w_up" in ready:
            h, done = _ffn(h, norm_ffn[i], ready["w_up"], ready["w_down"], cast)
        else:
            first, w_up, w_down = _ffn_first_tile(h, norm_ffn[i], ffn_w_up, ffn_w_down, i)
            h = lax.dynamic_update_slice(h, first, (0, 0))
            h, done = _ffn(h, norm_ffn[i], w_up, w_down, cast, first_done=True)
        if done:
            ready = dict(zip(("w_up", "w_down", "w_in", "w_out"), done))
    return h.reshape(batch, seq, d)
```

```python
import functools

import jax
import jax.numpy as jnp
from jax import lax
from jax.experimental import pallas as pl
from jax.experimental.pallas import tpu as pltpu

D_MODEL = 2048
ATT_HEADS = 16
ATT_KV_HEADS = 4
ATT_HEAD_DIM = 128
IDX_HEADS = 16
IDX_HEAD_DIM = 64
TOPK_MAX = 256
ML_HEADS = 8
ML_V_DIM = 256
ML_QK_DIM = 128
GATE_SOFTCAP = 15.0
ROPE_THETA = 500000.0
ROT_FRAC = 4
EPS = 1e-6

LANES = 128
SUBLANES = 8
BF16_ROWS = 16
VMEM_LIMIT = 60 * 1024 * 1024
INT_MIN = -(2 ** 31)
NEG_BIG = -1e30

LOG2_E = 1.4426950408889634

ATT_Q_BLOCK = 256
ATT_KEY_CHUNK = 256
SEARCH_UNROLL = 8
SEARCH_TRIPS = 32 // SEARCH_UNROLL
ML_CHUNK = 256
ML_HEADS_PER_STEP = 4
ML_CHUNK_UNROLL = 2
PROJ_ROWS = 512
PROJ_SUB = 512
FFN_SUB = 256


def _params(*sem):
    return pltpu.CompilerParams(dimension_semantics=sem, vmem_limit_bytes=VMEM_LIMIT)


def _store_rmsnorm(x_ref, g_ref, hn_ref):
    x = x_ref[...]
    ms = jnp.mean(x * x, axis=-1, keepdims=True)
    hn_ref[...] = (x * lax.rsqrt(ms + EPS) * g_ref[...]).astype(hn_ref.dtype)

def _rope_tables(seq, head_dim, heads_per_vreg):
    rot = head_dim // ROT_FRAC
    half = rot // 2
    inv_freq = ROPE_THETA ** (-2.0 * jnp.arange(half, dtype=jnp.float32) / rot)
    ang = jnp.arange(seq).astype(jnp.float32)[:, None] * inv_freq[None, :]
    cos, sin = jnp.cos(ang), jnp.sin(ang)
    ones = jnp.ones((seq, head_dim - rot), jnp.float32)
    zeros_h = jnp.zeros((seq, half), jnp.float32)
    zeros_r = jnp.zeros((seq, head_dim - rot), jnp.float32)
    c = jnp.concatenate([cos, cos, ones], axis=1)
    s1 = jnp.concatenate([-sin, zeros_h, zeros_r], axis=1)
    s2 = jnp.concatenate([zeros_h, sin, zeros_r], axis=1)
    rep = lambda t: jnp.tile(t, (1, heads_per_vreg))
    return rep(c), rep(s1), rep(s2), half


def _rope(y, c, s1, s2, half):
    return y * c + pltpu.roll(y, LANES - half, 1) * s1 + pltpu.roll(y, half, 1) * s2


def _att_proj_kernel(x_ref, g_ref, w_ref, wt_ref, gain_ref, tab_ref, tabi_ref, tabt_ref, wo_src_ref,
                     qk_ref, v_ref, qi_ref, kw_ref, wo_ref, hn_ref, *, half, half_i):
    _store_rmsnorm(x_ref, g_ref, hn_ref)
    n_qk, n_v, n_qi = qk_ref.shape[1], v_ref.shape[1], qi_ref.shape[1]
    n_q = ATT_HEADS * ATT_HEAD_DIM

    def cols(start, width):
        return lax.dot_general(hn_ref[...], w_ref[start:start + width, :], (((1,), (1,)), ((), ())),
                               preferred_element_type=jnp.float32)

    for s in range(n_qk // PROJ_SUB):
        acc = cols(s * PROJ_SUB, PROJ_SUB)
        gain = gain_ref[0 if s * PROJ_SUB < n_q else 1]
        for grp in range(PROJ_SUB // LANES):
            xg = acc[:, grp * LANES:(grp + 1) * LANES]
            ms = jnp.mean(xg * xg, axis=-1, keepdims=True)
            y = _rope(xg * lax.rsqrt(ms + EPS) * gain, tab_ref[0], tab_ref[1], tab_ref[2], half)
            qk_ref[:, s * PROJ_SUB + grp * LANES:s * PROJ_SUB + (grp + 1) * LANES] = y.astype(qk_ref.dtype)

    v_ref[...] = cols(n_qk, n_v).astype(v_ref.dtype)

    for s in range(n_qi // PROJ_SUB):
        acc = cols(n_qk + n_v + s * PROJ_SUB, PROJ_SUB)
        for grp in range(PROJ_SUB // LANES):
            y = _rope(acc[:, grp * LANES:(grp + 1) * LANES], tabi_ref[0], tabi_ref[1], tabi_ref[2], half_i)
            qi_ref[:, s * PROJ_SUB + grp * LANES:s * PROJ_SUB + (grp + 1) * LANES] = y.astype(qi_ref.dtype)

    tail = lax.dot_general(hn_ref[...], wt_ref[...], (((1,), (1,)), ((), ())), preferred_element_type=jnp.float32)
    kw_ref[...] = _rope(tail, tabt_ref[0], tabt_ref[1], tabt_ref[2], half_i)

    sub = wo_ref.shape[2]
    for t in range(wo_ref.shape[0]):
        wo_ref[t] = wo_src_ref[:, t * sub:(t + 1) * sub].astype(wo_ref.dtype)


def _ml_proj_kernel(x_ref, g_ref, w_ref, wg_ref, b_ref, qk_ref, v_ref, og_ref, gates_ref, hn_ref):
    _store_rmsnorm(x_ref, g_ref, hn_ref)
    n_qk, n_v = qk_ref.shape[1], v_ref.shape[1]
    for s in range((n_qk + n_v + og_ref.shape[1]) // PROJ_SUB):
        acc = lax.dot_general(hn_ref[...], w_ref[s * PROJ_SUB:(s + 1) * PROJ_SUB, :], (((1,), (1,)), ((), ())),
                              preferred_element_type=jnp.float32)
        start = s * PROJ_SUB
        if start < n_qk // 2:
            qk_ref[:, start:start + PROJ_SUB] = acc.astype(qk_ref.dtype)
        elif start < n_qk:
            qk_ref[:, start:start + PROJ_SUB] = (acc * ML_QK_DIM ** -0.5).astype(qk_ref.dtype)
        elif start < n_qk + n_v:
            v_ref[:, start - n_qk:start - n_qk + PROJ_SUB] = acc.astype(v_ref.dtype)
        else:
            og_ref[:, start - n_qk - n_v:start - n_qk - n_v + PROJ_SUB] = acc

    g = lax.dot_general(wg_ref[...], hn_ref[...], (((1,), (1,)), ((), ())),
                        preferred_element_type=jnp.float32)
    g = g + b_ref[...]
    g = GATE_SOFTCAP * jnp.tanh(g / GATE_SOFTCAP)
    logf = jnp.minimum(g, 0.0) - jnp.log1p(jnp.exp(-jnp.abs(g)))
    is_forget = lax.broadcasted_iota(jnp.int32, g.shape, 0) >= ML_HEADS
    gates_ref[...] = jnp.where(is_forget, logf, g)


def _key_to_f32(key):
    return pltpu.bitcast(jnp.where(key < 0, key ^ jnp.int32(0x7FFFFFFF), key), jnp.float32)


def _select_topk(score_ref, bias_ref, *, topk, n_chunks, side_work):
    _, tq, ck = score_ref.shape
    kf = jnp.float32(topk)

    def count(pred, thr):
        part = jnp.zeros((tq, LANES), jnp.float32)
        for c in range(n_chunks):
            for j in range(ck // LANES):
                part = part + jnp.where(pred(score_ref[c, :, j * LANES:(j + 1) * LANES], thr), 1.0, 0.0)
        return jnp.sum(part, axis=1, keepdims=True)

    key_ninf = INT_MIN + 0x7FFFFF
    key_pinf = 0x7F800000

    def search(it, tau):
        cand = tau + lax.shift_left(jnp.int32(1), jnp.int32(31) - it)
        in_range = jnp.logical_and(cand > tau, cand <= key_pinf)
        cnt = count(jnp.greater_equal, _key_to_f32(cand))
        return jnp.where(jnp.logical_and(in_range, cnt >= kf), cand, tau)

    def trip(t, tau):
        for u in range(SEARCH_UNROLL):
            tau = search(t * SEARCH_UNROLL + u, tau)
        side_work(t)
        return tau

    tau = lax.fori_loop(0, SEARCH_TRIPS, trip, jnp.full((tq, 1), key_ninf, jnp.int32))
    thr = _key_to_f32(tau)
    finite = lambda s, th: jnp.logical_and(s >= th, s > -jnp.inf)
    n_ge = count(finite, thr)

    @pl.when(jnp.max(n_ge) <= kf)
    def _():
        for c in range(n_chunks):
            bias_ref[c] = jnp.where(finite(score_ref[c], thr), 0.0, NEG_BIG)

    @pl.when(jnp.max(n_ge) > kf)
    def _():
        need = kf - count(jnp.greater, thr)
        r = lax.broadcasted_iota(jnp.int32, (ck, ck), 0)
        col = lax.broadcasted_iota(jnp.int32, (ck, ck), 1)
        before = jnp.where(r < col, 1.0, 0.0).astype(jnp.bfloat16)
        seen = jnp.zeros((tq, 1), jnp.float32)
        for c in range(n_chunks):
            s = score_ref[c]
            eq = jnp.logical_and(s == thr, s > -jnp.inf)
            e = jnp.where(eq, 1.0, 0.0)
            prefix = jnp.dot(e.astype(jnp.bfloat16), before, preferred_element_type=jnp.float32) + seen
            seen = seen + jnp.sum(e, axis=1, keepdims=True)
            keep = jnp.logical_or(s > thr, jnp.logical_and(eq, prefix < need))
            bias_ref[c] = jnp.where(keep, 0.0, NEG_BIG)


def _for_each_chunk(n, body):
    def pair(p, carry):
        body(2 * p)
        body(2 * p + 1)
        return carry

    lax.fori_loop(0, n // 2, pair, 0)
    pl.when(n % 2 == 1)(lambda: body(n - 1))


def _attention_kernel(q_ref, k_ref, v_ref, qi_ref, wq_ref, kk_ref, x_ref, wo_ref, h_ref, *scratch, topk):
    o_ref, proj_ref = scratch[-2:]
    i = pl.program_id(1)
    last = pl.num_programs(1) - 1
    sub = wo_ref.shape[2]
    assert wo_ref.shape[0] == SEARCH_TRIPS

    @pl.when(jnp.logical_and(pl.program_id(0) == 0, i == 0))
    def _():
        o_ref[...] = jnp.zeros(o_ref.shape, o_ref.dtype)

    def project(t):
        proj_ref[t] = jnp.dot(o_ref[...], wo_ref[t], preferred_element_type=jnp.float32)

    def finish_previous():
        for t in range(wo_ref.shape[0]):
            h_ref[:, t * sub:(t + 1) * sub] = x_ref[:, t * sub:(t + 1) * sub] + proj_ref[t]

    @pl.when(i < last)
    def _():
        _attention_block(q_ref, k_ref, v_ref, qi_ref, wq_ref, kk_ref, *scratch[:-1], topk=topk, q0=i * q_ref.shape[0],
                         project=project, finish_previous=finish_previous)

    @pl.when(i == last)
    def _():
        for t in range(wo_ref.shape[0]):
            project(t)
        finish_previous()


def _attention_block(q_ref, k_ref, v_ref, qi_ref, wq_ref, kk_ref,
                     qs_ref, wb_ref, score_ref, bias_ref, q4_ref, s_ref, m_ref, acc_ref, o_ref, *,
                     topk, q0, project, finish_previous):
    tq = q_ref.shape[0]
    ck = ATT_KEY_CHUNK
    half = tq // 2
    nt = (((1,), (1,)), ((), ()))
    n_chunks = (q0 + tq) // ck

    for r in range(2):
        rows = slice(r * half, (r + 1) * half)
        for h in range(IDX_HEADS):
            qs_ref[r, h * half:(h + 1) * half, :] = (
                qi_ref[rows, h * IDX_HEAD_DIM:(h + 1) * IDX_HEAD_DIM].astype(jnp.bfloat16))
            wb_ref[r * IDX_HEADS + h] = jnp.broadcast_to(
                wq_ref[rows, IDX_HEAD_DIM + h:IDX_HEAD_DIM + h + 1], (half, LANES))

    def score_chunk(c):
        off = pl.multiple_of(c * ck, ck)
        ki = kk_ref[pl.ds(off, ck), :].astype(jnp.bfloat16)[:, :IDX_HEAD_DIM]
        spos = off + lax.broadcasted_iota(jnp.int32, (half, LANES), 1)
        for r in range(2):
            d = lax.dot_general(qs_ref[r], ki, nt, preferred_element_type=jnp.float32)
            tpos = q0 + r * half + lax.broadcasted_iota(jnp.int32, (half, LANES), 0)
            for j in range(ck // LANES):
                acc = jnp.zeros((half, LANES), jnp.float32)
                for h in range(IDX_HEADS):
                    dh = d[h * half:(h + 1) * half, j * LANES:(j + 1) * LANES]
                    acc = acc + jnp.maximum(dh, 0.0) * wb_ref[r * IDX_HEADS + h]
                score_ref[c, r * half:(r + 1) * half, j * LANES:(j + 1) * LANES] = (
                    jnp.where(spos + j * LANES <= tpos, acc, -jnp.inf))

    _for_each_chunk(n_chunks, score_chunk)

    for n in range(1, score_ref.shape[0] + 1):
        pl.when(n_chunks == n)(functools.partial(_select_topk, score_ref, bias_ref, topk=topk, n_chunks=n,
                                                 side_work=project))
    finish_previous()

    exp2_scale = ATT_HEAD_DIM ** -0.5 * LOG2_E
    rep = ATT_HEADS // ATT_KV_HEADS
    pair = s_ref.shape[0]
    head_cols = lambda h: slice(h * ATT_HEAD_DIM, (h + 1) * ATT_HEAD_DIM)
    for g0 in range(0, ATT_KV_HEADS, pair):
        for u in range(pair):
            for r in range(rep):
                q4_ref[u, r * tq:(r + 1) * tq, :] = q_ref[:, head_cols((g0 + u) * rep + r)]
        m_ref[...] = jnp.full(m_ref.shape, NEG_BIG, jnp.float32)

        def logits_chunk(c):
            off = pl.multiple_of(c * ck, ck)
            for u in range(pair):
                s = lax.dot_general(q4_ref[u], k_ref[pl.ds(off, ck), head_cols(g0 + u)], nt,
                                    preferred_element_type=jnp.float32)
                s = ((s.reshape(rep, tq, ck) + bias_ref[c][None]) * exp2_scale).reshape(rep * tq, ck)
                s_ref[u, c] = s
                m = m_ref[u]
                for j in range(ck // LANES):
                    m = jnp.maximum(m, s[:, j * LANES:(j + 1) * LANES])
                m_ref[u] = m

        _for_each_chunk(n_chunks, logits_chunk)
        for u in range(pair):
            m_ref[u] = jnp.broadcast_to(jnp.max(m_ref[u], axis=1, keepdims=True), m_ref.shape[1:])
        acc_ref[...] = jnp.zeros(acc_ref.shape, jnp.float32)
        ones = jnp.ones((ck, LANES), jnp.bfloat16)

        def pv_chunk(c):
            off = pl.multiple_of(c * ck, ck)
            for u in range(pair):
                p = jnp.exp2(s_ref[u, c] - jnp.concatenate([m_ref[u]] * (ck // LANES), axis=1))
                v1 = jnp.concatenate([v_ref[pl.ds(off, ck), head_cols(g0 + u)], ones], axis=1)
                acc_ref[u] += jnp.dot(p.astype(jnp.bfloat16), v1, preferred_element_type=jnp.float32)

        _for_each_chunk(n_chunks, pv_chunk)
        for u in range(pair):
            o4 = acc_ref[u, :, :ATT_HEAD_DIM] / acc_ref[u, :, ATT_HEAD_DIM:]
            for r in range(rep):
                o_ref[:, head_cols((g0 + u) * rep + r)] = o4[r * tq:(r + 1) * tq].astype(o_ref.dtype)


def _attention(qk, v, qi, kw, x, wo, batch, seq):
    n, d = x.shape
    tq, ck = ATT_Q_BLOCK, ATT_KEY_CHUNK
    nq = seq // tq
    rep = ATT_HEADS // ATT_KV_HEADS
    pair = 2
    topk = min(TOPK_MAX, seq // 4)
    q_w, kv_w = ATT_HEADS * ATT_HEAD_DIM, ATT_KV_HEADS * ATT_HEAD_DIM
    qi_w = IDX_HEADS * IDX_HEAD_DIM
    sub = wo.shape[2]
    cur =lambda b, i: (b * nq + jnp.minimum(i, nq - 1), 0)
    prev = lambda b, i: (b * nq + jnp.maximum(i - 1, 0), 0)
    once = dict(pipeline_mode=pl.Buffered(1))
    return pl.pallas_call(
        functools.partial(_attention_kernel, topk=topk),
        grid=(batch, nq + 1),
        in_specs=[pl.BlockSpec((tq, q_w), cur),
                  pl.BlockSpec((seq, kv_w), lambda b, i: (b, q_w // kv_w)),
                  pl.BlockSpec((seq, kv_w), lambda b, i: (b, 0)),
                  pl.BlockSpec((tq, qi_w), cur),
                  pl.BlockSpec((tq, LANES), cur),
                  pl.BlockSpec((seq, LANES), lambda b, i: (b, 0), **once),
                  pl.BlockSpec((tq, d), prev),
                  pl.BlockSpec(wo.shape, lambda b, i: (0, 0, 0), **once)],
        out_specs=pl.BlockSpec((tq, d), prev),
        out_shape=jax.ShapeDtypeStruct((n, d), jnp.float32),
        scratch_shapes=[pltpu.VMEM((2, IDX_HEADS * tq // 2, IDX_HEAD_DIM), jnp.bfloat16),
                        pltpu.VMEM((2 * IDX_HEADS, tq // 2, LANES), jnp.float32),
                        pltpu.VMEM((seq // ck, tq, ck), jnp.float32),
                        pltpu.VMEM((seq // ck, tq, ck), jnp.float32),
                        pltpu.VMEM((pair, rep * tq, ATT_HEAD_DIM), jnp.bfloat16),
                        pltpu.VMEM((pair, seq // ck, rep * tq, ck), jnp.float32),
                        pltpu.VMEM((pair, rep * tq, LANES), jnp.float32),
                        pltpu.VMEM((pair, rep * tq, ATT_HEAD_DIM + LANES), jnp.float32),
                        pltpu.VMEM((tq, q_w), jnp.bfloat16),
                        pltpu.VMEM((SEARCH_TRIPS, tq, sub), jnp.float32)],
        compiler_params=_params("arbitrary", "arbitrary"),
        name="dsa_attention",
    )(qk, qk, v, qi, kw, kw, x, wo)


def _mlstm_kernel(q_ref, k_ref, v_ref, og_ref, ig_ref, lf_ref, hg_ref, o_ref):
    L = ML_CHUNK
    nc = q_ref.shape[0] // L
    dk, dv = ML_QK_DIM, ML_V_DIM
    heads = q_ref.shape[1] // dk
    row = lax.broadcasted_iota(jnp.int32, (L, L), 0)
    col = lax.broadcasted_iota(jnp.int32, (L, L), 1)
    tri = col <= row
    eye = col == row
    gain = hg_ref[...]

    def to_col(x_row):
        return jnp.sum(jnp.where(eye, jnp.broadcast_to(x_row, (L, L)), 0.0), axis=1, keepdims=True)

    def chunk(c, carry):
        return tuple(head_chunk(hd, c, carry[hd]) for hd in range(heads))

    def head_chunk(hd, c, state):
        C, n_row, m = state
        off = pl.multiple_of(c * L, L)
        qc = q_ref[pl.ds(off, L), hd * dk:(hd + 1) * dk]
        kc = k_ref[pl.ds(off, L), hd * dk:(hd + 1) * dk]
        vc = v_ref[pl.ds(off, L), hd * dv:(hd + 1) * dv]
        ig_r = ig_ref[hd, c]
        lf_r = lf_ref[hd, c]

        lf_b = jnp.broadcast_to(lf_r, (L, L))
        b_col = jnp.sum(jnp.where(tri, lf_b, 0.0), axis=1, keepdims=True)
        lf_col = to_col(lf_r)
        b_row = jnp.sum(jnp.where(row <= col, jnp.broadcast_to(lf_col, (L, L)), 0.0), axis=0, keepdims=True)
        a = jnp.sum(lf_r, axis=1, keepdims=True)

        g_row = a - b_row + ig_r
        m_loc = jnp.max(g_row, axis=1, keepdims=True)
        w_row = jnp.exp(g_row - m_loc)
        w_col = to_col(w_row)
        c_loc = lax.dot_general(kc, (w_col * vc.astype(jnp.float32)).astype(jnp.bfloat16),
                                (((0,), (0,)), ((), ())), preferred_element_type=jnp.float32)
        n_loc = jnp.dot(jnp.broadcast_to(w_row, (SUBLANES, L)).astype(jnp.bfloat16), kc,
                        preferred_element_type=jnp.float32)[0:1]

        dm = jnp.where(tri, b_col - b_row + ig_r, -jnp.inf)
        inter = b_col + m
        m_t = jnp.maximum(inter, jnp.max(dm, axis=1, keepdims=True))
        s = lax.dot_general(qc, kc, (((1,), (1,)), ((), ())), preferred_element_type=jnp.float32)
        s = s * jnp.exp(dm - m_t)
        s_inter = jnp.exp(inter - m_t)
        qf = qc.astype(jnp.float32)
        num = (jnp.dot(s.astype(jnp.bfloat16), vc, preferred_element_type=jnp.float32)
               + s_inter * jnp.dot(qc, C.astype(jnp.bfloat16), preferred_element_type=jnp.float32))
        den = jnp.sum(s, axis=1, keepdims=True) + s_inter * jnp.sum(qf * n_row, axis=1, keepdims=True)
        h = num / jnp.maximum(jnp.abs(den), jnp.exp(-m_t))

        ms = jnp.mean(h * h, axis=-1, keepdims=True)
        hn = h * lax.rsqrt(ms + EPS) * gain
        og = og_ref[pl.ds(off, L), hd * dv:(hd + 1) * dv]
        o_ref[pl.ds(off, L), hd * dv:(hd + 1) * dv] = (jax.nn.sigmoid(og) * hn).astype(o_ref.dtype)

        m_new = jnp.maximum(a + m, m_loc)
        s_old = jnp.exp(a + m - m_new)
        s_new = jnp.exp(m_loc - m_new)
        return (s_old * C + s_new * c_loc, s_old * n_row + s_new * n_loc, m_new)

    init = (jnp.zeros((dk, dv), jnp.float32), jnp.zeros((1, dk), jnp.float32), jnp.zeros((1, 1), jnp.float32))
    lax.fori_loop(0, nc, chunk, (init,) * heads, unroll=ML_CHUNK_UNROLL)


def _mlstm(qk, v, og, gates_t, h_gain, batch, seq):
    n = qk.shape[0]
    L = ML_CHUNK
    nc = seq // L
    hb = ML_HEADS_PER_STEP
    nhb = ML_HEADS // hb
    g4 = gates_t.reshape(2 * ML_HEADS, batch * nc, 1, L)
    return pl.pallas_call(
        _mlstm_kernel,
        grid=(batch, nhb),
        in_specs=[pl.BlockSpec((seq, hb * ML_QK_DIM), lambda b, h: (b, h)),
                  pl.BlockSpec((seq, hb * ML_QK_DIM), lambda b, h: (b, nhb + h)),
                  pl.BlockSpec((seq, hb * ML_V_DIM), lambda b, h: (b, h)),
                  pl.BlockSpec((seq, hb * ML_V_DIM), lambda b, h: (b, h)),
                  pl.BlockSpec((hb, nc, 1, L), lambda b, h: (h, b, 0, 0)),
                  pl.BlockSpec((hb, nc, 1, L), lambda b, h: (nhb + h, b, 0, 0)),
                  pl.BlockSpec((1, ML_V_DIM), lambda b, h: (0, 0))],
        out_specs=pl.BlockSpec((seq, hb * ML_V_DIM), lambda b, h: (b, h)),
        out_shape=jax.ShapeDtypeStruct((n, ML_HEADS * ML_V_DIM), jnp.bfloat16),
        compiler_params=_params("parallel", "parallel"),
        name="mlstm",
    )(qk, qk, v, og, g4, g4, h_gain.reshape(1, ML_V_DIM))


def _outproj_kernel(a_ref, w_ref, x_ref, o_ref):
    o_ref[...] = x_ref[...] + jnp.dot(a_ref[...], w_ref[...], preferred_element_type=jnp.float32)


def _outproj(a, w, x, tm=512):
    n, d = x.shape
    k = a.shape[1]
    return pl.pallas_call(
        _outproj_kernel,
        grid=(n // tm,),
        in_specs=[pl.BlockSpec((tm, k), lambda i: (i, 0)),
                  pl.BlockSpec((k, d), lambda i: (0, 0)),
                  pl.BlockSpec((tm, d), lambda i: (i, 0))],
        out_specs=pl.BlockSpec((tm, d), lambda i: (i, 0)),
        out_shape=jax.ShapeDtypeStruct((n, d), jnp.float32),
        compiler_params=_params("parallel"),
        name="outproj",
    )(a, w, x)


def _ffn_step(first_chunk, x_ref, g_ref, wu, wd, o_ref, hn_ref):
    @pl.when(first_chunk)
    def _():
        x = x_ref[...]
        ms = jnp.mean(x * x, axis=-1, keepdims=True)
        hn_ref[...] = (x * lax.rsqrt(ms + EPS) * g_ref[...]).astype(hn_ref.dtype)
        o_ref[...] = x

    acts = []
    for s in range(wu.shape[1] // FFN_SUB):
        u = jnp.dot(hn_ref[...], wu[:, s * FFN_SUB:(s + 1) * FFN_SUB], preferred_element_type=jnp.float32)
        acts.append(jnp.square(jnp.maximum(u, 0.0)).astype(jnp.bfloat16))
    o_ref[...] += jnp.dot(jnp.concatenate(acts, axis=1), wd[...], preferred_element_type=jnp.float32)


def _ffn_first_tile_kernel(x_ref, g_ref, wu32_ref, wd32_ref, o_ref, wu16_ref, wd16_ref, hn_ref):
    wu16_ref[...] = wu32_ref[...].astype(wu16_ref.dtype)
    wd16_ref[...] = wd32_ref[...].astype(wd16_ref.dtype)
    _ffn_step(pl.program_id(0) == 0, x_ref, g_ref, wu16_ref, wd16_ref, o_ref, hn_ref)


def _ffn_kernel(x_ref, g_ref, wu_ref, wd_ref, *rest, n_cast, first_done):
    cast_src, o_ref, cast_dst, hn_ref = rest[:n_cast], rest[n_cast], rest[n_cast + 1:-1], rest[-1]
    i, j = pl.program_id(0), pl.program_id(1)
    step = i * pl.num_programs(1) + j
    for src, dst in zip(cast_src, cast_dst):
        dst[...] = src[...].astype(dst.dtype)

    if first_done:
        @pl.when(step == 0)
        def _():
            o_ref[...] = x_ref[...]

        @pl.when(i > 0)
        def _():
            _ffn_step(j == 0, x_ref, g_ref, wu_ref, wd_ref, o_ref, hn_ref)
    else:
        _ffn_step(j == 0, x_ref, g_ref, wu_ref, wd_ref, o_ref, hn_ref)


FFN_ROWS, FFN_CHUNK = 1024, 512


def _ffn_first_tile(x, g, w_up32, w_down32, layer):
    n, d = x.shape
    tm, tf = min(FFN_ROWS, n), FFN_CHUNK
    f = w_up32.shape[2]
    return pl.pallas_call(
        _ffn_first_tile_kernel,
        grid=(f // tf,),
        in_specs=[pl.BlockSpec((tm, d), lambda j: (0, 0), pipeline_mode=pl.Buffered(1)),
                  pl.BlockSpec((1, d), lambda j: (0, 0)),
                  pl.BlockSpec((None, d, tf), lambda j: (layer, 0, j)),
                  pl.BlockSpec((None, tf, d), lambda j: (layer, j, 0))],
        out_specs=[pl.BlockSpec((tm, d), lambda j: (0, 0)),
                   pl.BlockSpec((d, tf), lambda j: (0, j)),
                   pl.BlockSpec((tf, d), lambda j: (j, 0))],
        out_shape=[jax.ShapeDtypeStruct((tm, d), jnp.float32), jax.ShapeDtypeStruct((d, f), jnp.bfloat16),
                   jax.ShapeDtypeStruct((f, d), jnp.bfloat16)],
        scratch_shapes=[pltpu.VMEM((tm, d), jnp.bfloat16)],
        compiler_params=_params("arbitrary"),
        name="ffn_first_tile",
    )(x, g.reshape(1, d), w_up32, w_down32)


def _ffn(x, g, w_up, w_down, cast=(), first_done=False):
    n, d = x.shape
    tm, tf = min(FFN_ROWS, n), FFN_CHUNK
    f = w_up.shape[1]
    ni, nj = n // tm, f // tf
    chunk = (lambda i, j: jnp.where(i > 0, j, 0)) if first_done else (lambda i, j: j)
    in_specs, out_specs, out_shapes = [], [], []
    for a, layer in cast:
        rows, cols = a.shape[1:]
        slab = rows // (ni * nj) // BF16_ROWS * BF16_ROWS
        in_specs.append(pl.BlockSpec((None, slab, cols), lambda i, j, layer=layer: (layer, i * nj + j, 0)))
        out_specs.append(pl.BlockSpec((slab, cols), lambda i, j: (i * nj + j, 0)))
        out_shapes.append(jax.ShapeDtypeStruct((slab * ni * nj, cols), jnp.bfloat16))
    out = pl.pallas_call(
        functools.partial(_ffn_kernel, n_cast=len(cast), first_done=first_done),
        grid=(ni, nj),
        in_specs=[pl.BlockSpec((tm, d), lambda i, j: (i, 0)),
                  pl.BlockSpec((1, d), lambda i, j: (0, 0)),
                  pl.BlockSpec((d, tf), lambda i, j: (0, chunk(i, j))),
                  pl.BlockSpec((tf, d), lambda i, j: (chunk(i, j), 0))] + in_specs,
        out_specs=[pl.BlockSpec((tm, d), lambda i, j: (i, 0))] + out_specs,
        out_shape=[jax.ShapeDtypeStruct((n, d), jnp.float32)] + out_shapes,
        scratch_shapes=[pltpu.VMEM((tm, d), jnp.bfloat16)],
        compiler_params=_params("arbitrary", "arbitrary"),
        name="ffn",
    )(x, g.reshape(1, d), w_up, w_down, *[a for a, _ in cast])
    return out[0], out[1:]


def _attention_layer(h, batch, seq, g_mix, w_t, w_tail_t, q_gain, k_gain, w_out):
    bf = jnp.bfloat16
    n, d = h.shape
    n_qk = (ATT_HEADS + ATT_KV_HEADS) * ATT_HEAD_DIM
    n_v = ATT_KV_HEADS * ATT_HEAD_DIM
    n_qi = IDX_HEADS * IDX_HEAD_DIM
    tm = min(PROJ_ROWS, seq)

    c, s1, s2, half = _rope_tables(seq, ATT_HEAD_DIM, 1)
    ci, s1i, s2i, half_i = _rope_tables(seq, IDX_HEAD_DIM, LANES // IDX_HEAD_DIM)
    lane = jnp.arange(LANES)
    is_k = (lane < IDX_HEAD_DIM)[None, :]
    is_w = jnp.logical_and(lane >= IDX_HEAD_DIM, lane < IDX_HEAD_DIM + IDX_HEADS)[None, :]
    w_scale = IDX_HEADS ** -0.5 * IDX_HEAD_DIM ** -0.5
    tab = jnp.stack([c, s1, s2])
    tab_i = jnp.stack([ci, s1i, s2i])
    tab_t = jnp.stack([jnp.where(is_k, ci, jnp.where(is_w, w_scale, 0.0)),
                       jnp.where(is_k, s1i, 0.0), jnp.where(is_k, s2i, 0.0)])
    gains = jnp.stack([q_gain, k_gain]).reshape(2, 1, ATT_HEAD_DIM)
    n_main = n_qk + n_v + n_qi
    w_tail = jnp.pad(w_tail_t.astype(bf), ((0, LANES - w_tail_t.shape[0]), (0, 0)))

    w_out_stack, w_out_layer = w_out
    wo_rows = w_out_stack.shape[1] // (n // tm)
    wo_sub = d // SEARCH_TRIPS

    row = lambda width: pl.BlockSpec((tm, width), lambda i: (i, 0))
    whole = lambda a: pl.BlockSpec(a.shape, lambda i: (0,) * a.ndim, pipeline_mode=pl.Buffered(1))
    tbl = pl.BlockSpec((3, tm, LANES), lambda i: (0, i % (seq // tm), 0))
    qk, v, qi, kw, wo = pl.pallas_call(
        functools.partial(_att_proj_kernel, half=half, half_i=half_i),
        grid=(n // tm,),
        in_specs=[row(d), whole(g_mix.reshape(1, d)),
                  pl.BlockSpec((n_main, d), lambda i: (0, 0), pipeline_mode=pl.Buffered(1)), whole(w_tail),
                  whole(gains), tbl, tbl, tbl,
                  pl.BlockSpec((None, wo_rows, d), lambda i: (w_out_layer, i, 0))],
        out_specs=[row(n_qk), row(n_v), row(n_qi), row(LANES),
                   pl.BlockSpec((SEARCH_TRIPS, wo_rows, wo_sub), lambda i: (0, i, 0))],
        out_shape=[jax.ShapeDtypeStruct((n, n_qk), bf), jax.ShapeDtypeStruct((n, n_v), bf),
                   jax.ShapeDtypeStruct((n, n_qi), bf), jax.ShapeDtypeStruct((n, LANES), jnp.float32),
                   jax.ShapeDtypeStruct((SEARCH_TRIPS, w_out_stack.shape[1], wo_sub), bf)],
        scratch_shapes=[pltpu.VMEM((tm, d), bf)],
        compiler_params=_params("parallel"),
        name="att_proj",
    )(h, g_mix.reshape(1, d), w_t, w_tail, gains, tab, tab_i, tab_t, w_out_stack)

    return _attention(qk, v, qi, kw, h, wo, batch, seq)


def _mlstm_layer(h, batch, seq, g_mix, w_t, w_tail_t, b_gate, h_gain, w_out):
    bf = jnp.bfloat16
    n, d = h.shape
    n_qk, n_v = 2 * ML_HEADS * ML_QK_DIM, ML_HEADS * ML_V_DIM
    n_g = 2 * ML_HEADS
    tm = min(PROJ_ROWS, seq)
    g2 = g_mix.reshape(1, d)
    wg_t = w_tail_t.astype(bf)
    n_main = n_qk + n_v + d

    row = lambda width: pl.BlockSpec((tm, width), lambda i: (i, 0))
    whole = lambda a: pl.BlockSpec(a.shape, lambda i: (0,) * a.ndim, pipeline_mode=pl.Buffered(1))
    qk, v, og, gates_t = pl.pallas_call(
        _ml_proj_kernel,
        grid=(n // tm,),
        in_specs=[row(d), whole(g2), pl.BlockSpec((n_main, d), lambda i: (0, 0), pipeline_mode=pl.Buffered(1)),
                  whole(wg_t), pl.BlockSpec((n_g, 1), lambda i: (0, 0))],
        out_specs=[row(n_qk), row(n_v), row(d), pl.BlockSpec((n_g, tm), lambda i: (0, i))],
        out_shape=[jax.ShapeDtypeStruct((n, n_qk), bf), jax.ShapeDtypeStruct((n, n_v), bf),
                   jax.ShapeDtypeStruct((n, d), jnp.float32), jax.ShapeDtypeStruct((n_g, n), jnp.float32)],
        scratch_shapes=[pltpu.VMEM((tm, d), bf)],
        compiler_params=_params("parallel"),
        name="ml_proj",
    )(h, g2, w_t, wg_t, b_gate.reshape(n_g, 1))
    y = _mlstm(qk, v, og, gates_t, h_gain, batch, seq)
    return _outproj(y, w_out.astype(bf), h)


def kernel(x, norm_mix, norm_ffn, att_w_in, att_q_gain, att_k_gain, att_w_out, ml_w_in, ml_b_gate, ml_h_gain,
           ml_w_out, ffn_w_up, ffn_w_down):
    batch, seq, d = x.shape
    bf = jnp.bfloat16
    h = x.reshape(batch * seq, d)
    depth = norm_mix.shape[0]
    att_w_in_t, ml_w_in_t = jnp.swapaxes(att_w_in, 1, 2), jnp.swapaxes(ml_w_in, 1, 2)
    def tail_t(w_t_f32, idx):
        aligned = w_t_f32.shape[1] // LANES * LANES
        have = ready["w_in"]
        return have[aligned:] if have.shape[0] > aligned else w_t_f32[idx, aligned:, :]

    ready = {"w_in": att_w_in_t[0].astype(bf)}
    for i in range(depth):
        j = i // 2
        if i % 2 == 0:
            w_out = (ready["w_out"][None], 0) if "w_out" in ready else (att_w_out, j)
            h = _attention_layer(h, batch, seq, norm_mix[i], ready["w_in"], tail_t(att_w_in_t, j), att_q_gain[j],
                                 att_k_gain[j], w_out)
        else:
            h = _mlstm_layer(h, batch, seq, norm_mix[i], ready["w_in"], tail_t(ml_w_in_t, j), ml_b_gate[j],
                             ml_h_gain[j], ready["w_out"])
        nxt = i + 1
        cast = []
        if nxt < depth:
            w_in_t, w_out = (att_w_in_t, att_w_out) if nxt % 2 == 0 else (ml_w_in_t, ml_w_out)
            cast = [(ffn_w_up, nxt), (ffn_w_down, nxt), (w_in_t, nxt // 2), (w_out, nxt // 2)]
        if "w_up" in ready:
            h, done = _ffn(h, norm_ffn[i], ready["w_up"], ready["w_down"], cast)
        else:
            first, w_up, w_down = _ffn_first_tile(h, norm_ffn[i], ffn_w_up, ffn_w_down, i)
            h = lax.dynamic_update_slice(h, first, (0, 0))
            h, done = _ffn(h, norm_ffn[i], w_up, w_down, cast, first_done=True)
        if done:
            ready = dict(zip(("w_up", "w_down", "w_in", "w_out"), done))
    return h.reshape(batch, seq, d)
```

```python
import functools

import jax
import jax.numpy as jnp
from jax import lax
from jax.experimental import pallas as pl
from jax.experimental.pallas import tpu as pltpu

D_MODEL = 2048
ATT_HEADS = 16
ATT_KV_HEADS = 4
ATT_HEAD_DIM = 128
IDX_HEADS = 16
IDX_HEAD_DIM = 64
TOPK_MAX = 256
ML_HEADS = 8
ML_V_DIM = 256
ML_QK_DIM = 128
GATE_SOFTCAP = 15.0
ROPE_THETA = 500000.0
ROT_FRAC = 4
EPS = 1e-6

LANES = 128
SUBLANES = 8
BF16_ROWS = 16
VMEM_LIMIT = 60 * 1024 * 1024
INT_MIN = -(2 ** 31)
NEG_BIG = -1e30

LOG2_E = 1.4426950408889634

ATT_Q_BLOCK = 256
ATT_KEY_CHUNK = 256
SEARCH_UNROLL = 4
SEARCH_TRIPS = 32 // SEARCH_UNROLL
ML_CHUNK = 256
ML_HEADS_PER_STEP = 4
ML_CHUNK_UNROLL = 2
PROJ_ROWS = 512
PROJ_SUB = 512
FFN_SUB = 256


def _params(*sem, fuse_inputs=None):
    return pltpu.CompilerParams(dimension_semantics=sem, vmem_limit_bytes=VMEM_LIMIT, allow_input_fusion=fuse_inputs)


def _store_rmsnorm(x_ref, g_ref, hn_ref):
    x = x_ref[...]
    ms = jnp.mean(x * x, axis=-1, keepdims=True)
    hn_ref[...] = (x * lax.rsqrt(ms + EPS) * g_ref[...]).astype(hn_ref.dtype)

def _rope_tables(seq, head_dim, heads_per_vreg):
    rot = head_dim // ROT_FRAC
    half = rot // 2
    inv_freq = ROPE_THETA ** (-2.0 * jnp.arange(half, dtype=jnp.float32) / rot)
    ang = jnp.arange(seq).astype(jnp.float32)[:, None] * inv_freq[None, :]
    cos, sin = jnp.cos(ang), jnp.sin(ang)
    ones = jnp.ones((seq, head_dim - rot), jnp.float32)
    zeros_h = jnp.zeros((seq, half), jnp.float32)
    zeros_r = jnp.zeros((seq, head_dim - rot), jnp.float32)
    c = jnp.concatenate([cos, cos, ones], axis=1)
    s1 = jnp.concatenate([-sin, zeros_h, zeros_r], axis=1)
    s2 = jnp.concatenate([zeros_h, sin, zeros_r], axis=1)
    rep = lambda t: jnp.tile(t, (1, heads_per_vreg))
    return rep(c), rep(s1), rep(s2), half


def _rope(y, c, s1, s2, half):
    return y * c + pltpu.roll(y, LANES - half, 1) * s1 + pltpu.roll(y, half, 1) * s2


def _att_proj_kernel(x_ref, g_ref, w_ref, wt_ref, gain_ref, tab_ref, tabi_ref, tabt_ref, wo_src_ref,
                     qk_ref, v_ref, qi_ref, kw_ref, wo_ref, hn_ref, *, half, half_i):
    _store_rmsnorm(x_ref, g_ref, hn_ref)
    n_qk, n_v, n_qi = qk_ref.shape[1], v_ref.shape[1], qi_ref.shape[1]
    n_q = ATT_HEADS * ATT_HEAD_DIM

    def cols(start, width):
        return lax.dot_general(hn_ref[...], w_ref[start:start + width, :], (((1,), (1,)), ((), ())),
                               preferred_element_type=jnp.float32)

    for s in range(n_qk // PROJ_SUB):
        acc = cols(s * PROJ_SUB, PROJ_SUB)
        gain = gain_ref[0 if s * PROJ_SUB < n_q else 1]
        for grp in range(PROJ_SUB // LANES):
            xg = acc[:, grp * LANES:(grp + 1) * LANES]
            ms = jnp.mean(xg * xg, axis=-1, keepdims=True)
            y = _rope(xg * lax.rsqrt(ms + EPS) * gain, tab_ref[0], tab_ref[1], tab_ref[2], half)
            qk_ref[:, s * PROJ_SUB + grp * LANES:s * PROJ_SUB + (grp + 1) * LANES] = y.astype(qk_ref.dtype)

    v_ref[...] = cols(n_qk, n_v).astype(v_ref.dtype)

    for s in range(n_qi // PROJ_SUB):
        acc = cols(n_qk + n_v + s * PROJ_SUB, PROJ_SUB)
        for grp in range(PROJ_SUB // LANES):
            y = _rope(acc[:, grp * LANES:(grp + 1) * LANES], tabi_ref[0], tabi_ref[1], tabi_ref[2], half_i)
            qi_ref[:, s * PROJ_SUB + grp * LANES:s * PROJ_SUB + (grp + 1) * LANES] = y.astype(qi_ref.dtype)

    tail = lax.dot_general(hn_ref[...], wt_ref[...], (((1,), (1,)), ((), ())), preferred_element_type=jnp.float32)
    kw_ref[...] = _rope(tail, tabt_ref[0], tabt_ref[1], tabt_ref[2], half_i)

    sub = wo_ref.shape[2]
    for t in range(wo_ref.shape[0]):
        wo_ref[t] = wo_src_ref[:, t * sub:(t + 1) * sub].astype(wo_ref.dtype)


def _ml_proj_kernel(x_ref, g_ref, w_ref, wg_ref, b_ref, qk_ref, v_ref, og_ref, gates_ref, hn_ref):
    _store_rmsnorm(x_ref, g_ref, hn_ref)
    n_qk, n_v = qk_ref.shape[1], v_ref.shape[1]
    for s in range((n_qk + n_v + og_ref.shape[1]) // PROJ_SUB):
        acc = lax.dot_general(hn_ref[...], w_ref[s * PROJ_SUB:(s + 1) * PROJ_SUB, :], (((1,), (1,)), ((), ())),
                              preferred_element_type=jnp.float32)
        start = s * PROJ_SUB
        if start < n_qk // 2:
            qk_ref[:, start:start + PROJ_SUB] = acc.astype(qk_ref.dtype)
        elif start < n_qk:
            qk_ref[:, start:start + PROJ_SUB] = (acc * ML_QK_DIM ** -0.5).astype(qk_ref.dtype)
        elif start < n_qk + n_v:
            v_ref[:, start - n_qk:start - n_qk + PROJ_SUB] = acc.astype(v_ref.dtype)
        else:
            og_ref[:, start - n_qk - n_v:start - n_qk - n_v + PROJ_SUB] = acc

    g = lax.dot_general(wg_ref[...], hn_ref[...], (((1,), (1,)), ((), ())),
                        preferred_element_type=jnp.float32)
    g = g + b_ref[...]
    g = GATE_SOFTCAP * jnp.tanh(g / GATE_SOFTCAP)
    logf = jnp.minimum(g, 0.0) - jnp.log1p(jnp.exp(-jnp.abs(g)))
    is_forget = lax.broadcasted_iota(jnp.int32, g.shape, 0) >= ML_HEADS
    gates_ref[...] = jnp.where(is_forget, logf, g)


def _key_to_f32(key):
    return pltpu.bitcast(jnp.where(key < 0, key ^ jnp.int32(0x7FFFFFFF), key), jnp.float32)


def _select_topk(score_ref, bias_ref, *, topk, n_chunks, side_work):
    _, tq, ck = score_ref.shape
    kf = jnp.float32(topk)

    def count(pred, thr):
        part = jnp.zeros((tq, LANES), jnp.float32)
        for c in range(n_chunks):
            for j in range(ck // LANES):
                part = part + jnp.where(pred(score_ref[c, :, j * LANES:(j + 1) * LANES], thr), 1.0, 0.0)
        return jnp.sum(part, axis=1, keepdims=True)

    key_ninf = INT_MIN + 0x7FFFFF
    key_pinf = 0x7F800000

    def search(it, tau):
        cand = tau + lax.shift_left(jnp.int32(1), jnp.int32(31) - it)
        in_range = jnp.logical_and(cand > tau, cand <= key_pinf)
        cnt = count(jnp.greater_equal, _key_to_f32(cand))
        return jnp.where(jnp.logical_and(in_range, cnt >= kf), cand, tau)

    def trip(t, tau):
        for u in range(SEARCH_UNROLL):
            tau = search(t * SEARCH_UNROLL + u, tau)
        side_work(t)
        return tau

    tau = lax.fori_loop(0, SEARCH_TRIPS, trip, jnp.full((tq, 1), key_ninf, jnp.int32))
    thr = _key_to_f32(tau)
    finite = lambda s, th: jnp.logical_and(s >= th, s > -jnp.inf)
    n_ge = count(finite, thr)

    @pl.when(jnp.max(n_ge) <= kf)
    def _():
        for c in range(n_chunks):
            bias_ref[c] = jnp.where(finite(score_ref[c], thr), 0.0, NEG_BIG)

    @pl.when(jnp.max(n_ge) > kf)
    def _():
        need = kf - count(jnp.greater, thr)
        r = lax.broadcasted_iota(jnp.int32, (ck, ck), 0)
        col = lax.broadcasted_iota(jnp.int32, (ck, ck), 1)
        before = jnp.where(r < col, 1.0, 0.0).astype(jnp.bfloat16)
        seen = jnp.zeros((tq, 1), jnp.float32)
        for c in range(n_chunks):
            s = score_ref[c]
            eq = jnp.logical_and(s == thr, s > -jnp.inf)
            e = jnp.where(eq, 1.0, 0.0)
            prefix = jnp.dot(e.astype(jnp.bfloat16), before, preferred_element_type=jnp.float32) + seen
            seen = seen + jnp.sum(e, axis=1, keepdims=True)
            keep = jnp.logical_or(s > thr, jnp.logical_and(eq, prefix < need))
            bias_ref[c] = jnp.where(keep, 0.0, NEG_BIG)


def _for_each_chunk(n, body):
    def pair(p, carry):
        body(2 * p)
        body(2 * p + 1)
        return carry

    lax.fori_loop(0, n // 2, pair, 0)
    pl.when(n % 2 == 1)(lambda: body(n - 1))


def _attention_kernel(q_ref, k_ref, v_ref, qi_ref, wq_ref, kk_ref, x_ref, wo_ref, h_ref, *scratch, topk):
    o_ref, proj_ref = scratch[-2:]
    i = pl.program_id(1)
    last = pl.num_programs(1) - 1
    sub = wo_ref.shape[2]
    assert wo_ref.shape[0] == SEARCH_TRIPS

    @pl.when(jnp.logical_and(pl.program_id(0) == 0, i == 0))
    def _():
        o_ref[...] = jnp.zeros(o_ref.shape, o_ref.dtype)

    def project(t):
        proj_ref[t] = jnp.dot(o_ref[...], wo_ref[t], preferred_element_type=jnp.float32)

    def finish_previous():
        for t in range(wo_ref.shape[0]):
            h_ref[:, t * sub:(t + 1) * sub] = x_ref[:, t * sub:(t + 1) * sub] + proj_ref[t]

    @pl.when(i < last)
    def _():
        _attention_block(q_ref, k_ref, v_ref, qi_ref, wq_ref, kk_ref, *scratch[:-1], topk=topk, q0=i * q_ref.shape[0],
                         project=project, finish_previous=finish_previous)

    @pl.when(i == last)
    def _():
        for t in range(wo_ref.shape[0]):
            project(t)
        finish_previous()


def _attention_block(q_ref, k_ref, v_ref, qi_ref, wq_ref, kk_ref,
                     qs_ref, wb_ref, score_ref, bias_ref, q4_ref, s_ref, m_ref, acc_ref, o_ref, *,
                     topk, q0, project, finish_previous):
    tq = q_ref.shape[0]
    ck = ATT_KEY_CHUNK
    half = tq // 2
    nt = (((1,), (1,)), ((), ()))
    n_chunks = (q0 + tq) // ck

    for r in range(2):
        rows = slice(r * half, (r + 1) * half)
        for h in range(IDX_HEADS):
            qs_ref[r, h * half:(h + 1) * half, :] = (
                qi_ref[rows, h * IDX_HEAD_DIM:(h + 1) * IDX_HEAD_DIM].astype(jnp.bfloat16))
            wb_ref[r * IDX_HEADS + h] = jnp.broadcast_to(
                wq_ref[rows, IDX_HEAD_DIM + h:IDX_HEAD_DIM + h + 1], (half, LANES))

    def score_chunk(c):
        off = pl.multiple_of(c * ck, ck)
        ki = kk_ref[pl.ds(off, ck), :].astype(jnp.bfloat16)[:, :IDX_HEAD_DIM]
        spos = off + lax.broadcasted_iota(jnp.int32, (half, LANES), 1)
        for r in range(2):
            d = lax.dot_general(qs_ref[r], ki, nt, preferred_element_type=jnp.float32)
            tpos = q0 + r * half + lax.broadcasted_iota(jnp.int32, (half, LANES), 0)
            for j in range(ck // LANES):
                acc = jnp.zeros((half, LANES), jnp.float32)
                for h in range(IDX_HEADS):
                    dh = d[h * half:(h + 1) * half, j * LANES:(j + 1) * LANES]
                    acc = acc + jnp.maximum(dh, 0.0) * wb_ref[r * IDX_HEADS + h]
                score_ref[c, r * half:(r + 1) * half, j * LANES:(j + 1) * LANES] = (
                    jnp.where(spos + j * LANES <= tpos, acc, -jnp.inf))

    _for_each_chunk(n_chunks, score_chunk)

    for n in range(1, score_ref.shape[0] + 1):
        pl.when(n_chunks == n)(functools.partial(_select_topk, score_ref, bias_ref, topk=topk, n_chunks=n,
                                                 side_work=project))
    finish_previous()

    exp2_scale = ATT_HEAD_DIM ** -0.5 * LOG2_E
    rep = ATT_HEADS // ATT_KV_HEADS
    pair = s_ref.shape[0]
    head_cols = lambda h: slice(h * ATT_HEAD_DIM, (h + 1) * ATT_HEAD_DIM)
    for g0 in range(0, ATT_KV_HEADS, pair):
        for u in range(pair):
            for r in range(rep):
                q4_ref[u, r * tq:(r + 1) * tq, :] = q_ref[:, head_cols((g0 + u) * rep + r)]
        m_ref[...] = jnp.full(m_ref.shape, NEG_BIG, jnp.float32)

        def logits_chunk(c):
            off = pl.multiple_of(c * ck, ck)
            for u in range(pair):
                s = lax.dot_general(q4_ref[u], k_ref[pl.ds(off, ck), head_cols(g0 + u)], nt,
                                    preferred_element_type=jnp.float32)
                s = ((s.reshape(rep, tq, ck) + bias_ref[c][None]) * exp2_scale).reshape(rep * tq, ck)
                s_ref[u, c] = s
                m = m_ref[u]
                for j in range(ck // LANES):
                    m = jnp.maximum(m, s[:, j * LANES:(j + 1) * LANES])
                m_ref[u] = m

        _for_each_chunk(n_chunks, logits_chunk)
        for u in range(pair):
            m_ref[u] = jnp.broadcast_to(jnp.max(m_ref[u], axis=1, keepdims=True), m_ref.shape[1:])
        acc_ref[...] = jnp.zeros(acc_ref.shape, jnp.float32)
        ones = jnp.ones((ck, LANES), jnp.bfloat16)

        def pv_chunk(c):
            off = pl.multiple_of(c * ck, ck)
            for u in range(pair):
                p = jnp.exp2(s_ref[u, c] - jnp.concatenate([m_ref[u]] * (ck // LANES), axis=1))
                v1 = jnp.concatenate([v_ref[pl.ds(off, ck), head_cols(g0 + u)], ones], axis=1)
                acc_ref[u] += jnp.dot(p.astype(jnp.bfloat16), v1, preferred_element_type=jnp.float32)

        _for_each_chunk(n_chunks, pv_chunk)
        for u in range(pair):
            o4 = acc_ref[u, :, :ATT_HEAD_DIM] / acc_ref[u, :, ATT_HEAD_DIM:]
            for r in range(rep):
                o_ref[:, head_cols((g0 + u) * rep + r)] = o4[r * tq:(r + 1) * tq].astype(o_ref.dtype)


def _attention(qk, v, qi, kw, x, wo, batch, seq):
    n, d = x.shape
    tq, ck = ATT_Q_BLOCK, ATT_KEY_CHUNK
    nq = seq // tq
    rep = ATT_HEADS // ATT_KV_HEADS
    pair = 2
    topk = min(TOPK_MAX, seq // 4)
    q_w, kv_w = ATT_HEADS * ATT_HEAD_DIM, ATT_KV_HEADS * ATT_HEAD_DIM
    qi_w = IDX_HEADS * IDX_HEAD_DIM
    sub = wo.shape[2]
    cur =lambda b, i: (b * nq + jnp.minimum(i, nq - 1), 0)
    prev = lambda b, i: (b * nq + jnp.maximum(i - 1, 0), 0)
    once = dict(pipeline_mode=pl.Buffered(1))
    return pl.pallas_call(
        functools.partial(_attention_kernel, topk=topk),
        grid=(batch, nq + 1),
        in_specs=[pl.BlockSpec((tq, q_w), cur),
                  pl.BlockSpec((seq, kv_w), lambda b, i: (b, q_w // kv_w)),
                  pl.BlockSpec((seq, kv_w), lambda b, i: (b, 0)),
                  pl.BlockSpec((tq, qi_w), cur),
                  pl.BlockSpec((tq, LANES), cur),
                  pl.BlockSpec((seq, LANES), lambda b, i: (b, 0), **once),
                  pl.BlockSpec((tq, d), prev),
                  pl.BlockSpec(wo.shape, lambda b, i: (0, 0, 0), **once)],
        out_specs=pl.BlockSpec((tq, d), prev),
        out_shape=jax.ShapeDtypeStruct((n, d), jnp.float32),
        scratch_shapes=[pltpu.VMEM((2, IDX_HEADS * tq // 2, IDX_HEAD_DIM), jnp.bfloat16),
                        pltpu.VMEM((2 * IDX_HEADS, tq // 2, LANES), jnp.float32),
                        pltpu.VMEM((seq // ck, tq, ck), jnp.float32),
                        pltpu.VMEM((seq // ck, tq, ck), jnp.float32),
                        pltpu.VMEM((pair, rep * tq, ATT_HEAD_DIM), jnp.bfloat16),
                        pltpu.VMEM((pair, seq // ck, rep * tq, ck), jnp.float32),
                        pltpu.VMEM((pair, rep * tq, LANES), jnp.float32),
                        pltpu.VMEM((pair, rep * tq, ATT_HEAD_DIM + LANES), jnp.float32),
                        pltpu.VMEM((tq, q_w), jnp.bfloat16),
                        pltpu.VMEM((SEARCH_TRIPS, tq, sub), jnp.float32)],
        compiler_params=_params("arbitrary", "arbitrary"),
        name="dsa_attention",
    )(qk, qk, v, qi, kw, kw, x, wo)


def _mlstm_kernel(q_ref, k_ref, v_ref, og_ref, ig_ref, lf_ref, hg_ref, o_ref):
    L = ML_CHUNK
    nc = q_ref.shape[0] // L
    dk, dv = ML_QK_DIM, ML_V_DIM
    heads = q_ref.shape[1] // dk
    row = lax.broadcasted_iota(jnp.int32, (L, L), 0)
    col = lax.broadcasted_iota(jnp.int32, (L, L), 1)
    tri = col <= row
    eye = col == row
    gain = hg_ref[...]

    def to_col(x_row):
        return jnp.sum(jnp.where(eye, jnp.broadcast_to(x_row, (L, L)), 0.0), axis=1, keepdims=True)

    def chunk(c, carry):
        return tuple(head_chunk(hd, c, carry[hd]) for hd in range(heads))

    def head_chunk(hd, c, state):
        C, n_row, m = state
        off = pl.multiple_of(c * L, L)
        qc = q_ref[pl.ds(off, L), hd * dk:(hd + 1) * dk]
        kc = k_ref[pl.ds(off, L), hd * dk:(hd + 1) * dk]
        vc = v_ref[pl.ds(off, L), hd * dv:(hd + 1) * dv]
        ig_r = ig_ref[hd, c]
        lf_r = lf_ref[hd, c]

        lf_b = jnp.broadcast_to(lf_r, (L, L))
        b_col = jnp.sum(jnp.where(tri, lf_b, 0.0), axis=1, keepdims=True)
        lf_col = to_col(lf_r)
        b_row = jnp.sum(jnp.where(row <= col, jnp.broadcast_to(lf_col, (L, L)), 0.0), axis=0, keepdims=True)
        a = jnp.sum(lf_r, axis=1, keepdims=True)

        g_row = a - b_row + ig_r
        m_loc = jnp.max(g_row, axis=1, keepdims=True)
        w_row = jnp.exp(g_row - m_loc)
        w_col = to_col(w_row)
        c_loc = lax.dot_general(kc, (w_col * vc.astype(jnp.float32)).astype(jnp.bfloat16),
                                (((0,), (0,)), ((), ())), preferred_element_type=jnp.float32)
        n_loc = jnp.dot(jnp.broadcast_to(w_row, (SUBLANES, L)).astype(jnp.bfloat16), kc,
                        preferred_element_type=jnp.float32)[0:1]

        dm = jnp.where(tri, b_col - b_row + ig_r, -jnp.inf)
        inter = b_col + m
        m_t = jnp.maximum(inter, jnp.max(dm, axis=1, keepdims=True))
        s = lax.dot_general(qc, kc, (((1,), (1,)), ((), ())), preferred_element_type=jnp.float32)
        s = s * jnp.exp(dm - m_t)
        s_inter = jnp.exp(inter - m_t)
        qf = qc.astype(jnp.float32)
        num = (jnp.dot(s.astype(jnp.bfloat16), vc, preferred_element_type=jnp.float32)
               + s_inter * jnp.dot(qc, C.astype(jnp.bfloat16), preferred_element_type=jnp.float32))
        den = jnp.sum(s, axis=1, keepdims=True) + s_inter * jnp.sum(qf * n_row, axis=1, keepdims=True)
        h = num / jnp.maximum(jnp.abs(den), jnp.exp(-m_t))

        ms = jnp.mean(h * h, axis=-1, keepdims=True)
        hn = h * lax.rsqrt(ms + EPS) * gain
        og = og_ref[pl.ds(off, L), hd * dv:(hd + 1) * dv]
        o_ref[pl.ds(off, L), hd * dv:(hd + 1) * dv] = (jax.nn.sigmoid(og) * hn).astype(o_ref.dtype)

        m_new = jnp.maximum(a + m, m_loc)
        s_old = jnp.exp(a + m - m_new)
        s_new = jnp.exp(m_loc - m_new)
        return (s_old * C + s_new * c_loc, s_old * n_row + s_new * n_loc, m_new)

    init = (jnp.zeros((dk, dv), jnp.float32), jnp.zeros((1, dk), jnp.float32), jnp.zeros((1, 1), jnp.float32))
    lax.fori_loop(0, nc, chunk, (init,) * heads, unroll=ML_CHUNK_UNROLL)


def _mlstm(qk, v, og, gates_t, h_gain, batch, seq):
    n = qk.shape[0]
    L = ML_CHUNK
    nc = seq // L
    hb = ML_HEADS_PER_STEP
    nhb = ML_HEADS // hb
    g4 = gates_t.reshape(2 * ML_HEADS, batch * nc, 1, L)
    return pl.pallas_call(
        _mlstm_kernel,
        grid=(batch, nhb),
        in_specs=[pl.BlockSpec((seq, hb * ML_QK_DIM), lambda b, h: (b, h)),
                  pl.BlockSpec((seq, hb * ML_QK_DIM), lambda b, h: (b, nhb + h)),
                  pl.BlockSpec((seq, hb * ML_V_DIM), lambda b, h: (b, h)),
                  pl.BlockSpec((seq, hb * ML_V_DIM), lambda b, h: (b, h)),
                  pl.BlockSpec((hb, nc, 1, L), lambda b, h: (h, b, 0, 0)),
                  pl.BlockSpec((hb, nc, 1, L), lambda b, h: (nhb + h, b, 0, 0)),
                  pl.BlockSpec((1, ML_V_DIM), lambda b, h: (0, 0))],
        out_specs=pl.BlockSpec((seq, hb * ML_V_DIM), lambda b, h: (b, h)),
        out_shape=jax.ShapeDtypeStruct((n, ML_HEADS * ML_V_DIM), jnp.bfloat16),
        compiler_params=_params("parallel", "parallel"),
        name="mlstm",
    )(qk, qk, v, og, g4, g4, h_gain.reshape(1, ML_V_DIM))


def _outproj_kernel(a_ref, w_ref, x_ref, o_ref):
    o_ref[...] = x_ref[...] + jnp.dot(a_ref[...], w_ref[...], preferred_element_type=jnp.float32)


def _outproj(a, w, x, tm=512):
    n, d = x.shape
    k = a.shape[1]
    return pl.pallas_call(
        _outproj_kernel,
        grid=(n // tm,),
        in_specs=[pl.BlockSpec((tm, k), lambda i: (i, 0)),
                  pl.BlockSpec((k, d), lambda i: (0, 0)),
                  pl.BlockSpec((tm, d), lambda i: (i, 0))],
        out_specs=pl.BlockSpec((tm, d), lambda i: (i, 0)),
        out_shape=jax.ShapeDtypeStruct((n, d), jnp.float32),
        compiler_params=_params("parallel"),
        name="outproj",
    )(a, w, x)


def _ffn_step(first_chunk, x_ref, g_ref, wu, wd, o_ref, hn_ref):
    @pl.when(first_chunk)
    def _():
        x = x_ref[...]
        ms = jnp.mean(x * x, axis=-1, keepdims=True)
        hn_ref[...] = (x * lax.rsqrt(ms + EPS) * g_ref[...]).astype(hn_ref.dtype)
        o_ref[...] = x

    acts = []
    for s in range(wu.shape[1] // FFN_SUB):
        u = jnp.dot(hn_ref[...], wu[:, s * FFN_SUB:(s + 1) * FFN_SUB], preferred_element_type=jnp.float32)
        acts.append(jnp.square(jnp.maximum(u, 0.0)).astype(jnp.bfloat16))
    o_ref[...] += jnp.dot(jnp.concatenate(acts, axis=1), wd[...], preferred_element_type=jnp.float32)


def _ffn_first_tile_kernel(x_ref, g_ref, wu32_ref, wd32_ref, o_ref, wu16_ref, wd16_ref, hn_ref):
    wu16_ref[...] = wu32_ref[...].astype(wu16_ref.dtype)
    wd16_ref[...] = wd32_ref[...].astype(wd16_ref.dtype)
    _ffn_step(pl.program_id(0) == 0, x_ref, g_ref, wu16_ref, wd16_ref, o_ref, hn_ref)


def _ffn_kernel(x_ref, g_ref, wu_ref, wd_ref, *rest, n_cast, first_done):
    cast_src, o_ref, cast_dst, hn_ref = rest[:n_cast], rest[n_cast], rest[n_cast + 1:-1], rest[-1]
    i, j = pl.program_id(0), pl.program_id(1)
    step = i * pl.num_programs(1) + j
    for src, dst in zip(cast_src, cast_dst):
        dst[...] = src[...].astype(dst.dtype)

    if first_done:
        @pl.when(step == 0)
        def _():
            o_ref[...] = x_ref[...]

        @pl.when(i > 0)
        def _():
            _ffn_step(j == 0, x_ref, g_ref, wu_ref, wd_ref, o_ref, hn_ref)
    else:
        _ffn_step(j == 0, x_ref, g_ref, wu_ref, wd_ref, o_ref, hn_ref)


FFN_ROWS, FFN_CHUNK = 1024, 512


def _ffn_first_tile(x, g, w_up32, w_down32, layer):
    n, d = x.shape
    tm, tf = min(FFN_ROWS, n), FFN_CHUNK
    f = w_up32.shape[2]
    return pl.pallas_call(
        _ffn_first_tile_kernel,
        grid=(f // tf,),
        in_specs=[pl.BlockSpec((tm, d), lambda j: (0, 0), pipeline_mode=pl.Buffered(1)),
                  pl.BlockSpec((1, d), lambda j: (0, 0)),
                  pl.BlockSpec((None, d, tf), lambda j: (layer, 0, j)),
                  pl.BlockSpec((None, tf, d), lambda j: (layer, j, 0))],
        out_specs=[pl.BlockSpec((tm, d), lambda j: (0, 0)),
                   pl.BlockSpec((d, tf), lambda j: (0, j)),
                   pl.BlockSpec((tf, d), lambda j: (j, 0))],
        out_shape=[jax.ShapeDtypeStruct((tm, d), jnp.float32), jax.ShapeDtypeStruct((d, f), jnp.bfloat16),
                   jax.ShapeDtypeStruct((f, d), jnp.bfloat16)],
        scratch_shapes=[pltpu.VMEM((tm, d), jnp.bfloat16)],
        compiler_params=_params("arbitrary"),
        name="ffn_first_tile",
    )(x, g.reshape(1, d), w_up32, w_down32)


def _ffn(x, g, w_up, w_down, cast=(), first_done=False):
    n, d = x.shape
    tm, tf = min(FFN_ROWS, n), FFN_CHUNK
    f = w_up.shape[1]
    ni, nj = n // tm, f // tf
    chunk = (lambda i, j: jnp.where(i > 0, j, 0)) if first_done else (lambda i, j: j)
    in_specs, out_specs, out_shapes = [], [], []
    for a, layer in cast:
        rows, cols = a.shape[1:]
        slab = rows // (ni * nj) // BF16_ROWS * BF16_ROWS
        in_specs.append(pl.BlockSpec((None, slab, cols), lambda i, j, layer=layer: (layer, i * nj + j, 0)))
        out_specs.append(pl.BlockSpec((slab, cols), lambda i, j: (i * nj + j, 0)))
        out_shapes.append(jax.ShapeDtypeStruct((slab * ni * nj, cols), jnp.bfloat16))
    out = pl.pallas_call(
        functools.partial(_ffn_kernel, n_cast=len(cast), first_done=first_done),
        grid=(ni, nj),
        in_specs=[pl.BlockSpec((tm, d), lambda i, j: (i, 0)),
                  pl.BlockSpec((1, d), lambda i, j: (0, 0)),
                  pl.BlockSpec((d, tf), lambda i, j: (0, chunk(i, j))),
                  pl.BlockSpec((tf, d), lambda i, j: (chunk(i, j), 0))] + in_specs,
        out_specs=[pl.BlockSpec((tm, d), lambda i, j: (i, 0))] + out_specs,
        out_shape=[jax.ShapeDtypeStruct((n, d), jnp.float32)] + out_shapes,
        scratch_shapes=[pltpu.VMEM((tm, d), jnp.bfloat16)],
        compiler_params=_params("arbitrary", "arbitrary"),
        name="ffn",
    )(x, g.reshape(1, d), w_up, w_down, *[a for a, _ in cast])
    return out[0], out[1:]


def _attention_layer(h, batch, seq, g_mix, w_t, w_tail_t, q_gain, k_gain, w_out):
    bf = jnp.bfloat16
    n, d = h.shape
    n_qk = (ATT_HEADS + ATT_KV_HEADS) * ATT_HEAD_DIM
    n_v = ATT_KV_HEADS * ATT_HEAD_DIM
    n_qi = IDX_HEADS * IDX_HEAD_DIM
    tm = min(PROJ_ROWS, seq)

    c, s1, s2, half = _rope_tables(seq, ATT_HEAD_DIM, 1)
    ci, s1i, s2i, half_i = _rope_tables(seq, IDX_HEAD_DIM, LANES // IDX_HEAD_DIM)
    lane = jnp.arange(LANES)
    is_k = (lane < IDX_HEAD_DIM)[None, :]
    is_w = jnp.logical_and(lane >= IDX_HEAD_DIM, lane < IDX_HEAD_DIM + IDX_HEADS)[None, :]
    w_scale = IDX_HEADS ** -0.5 * IDX_HEAD_DIM ** -0.5
    tab = jnp.stack([c, s1, s2])
    tab_i = jnp.stack([ci, s1i, s2i])
    tab_t = jnp.stack([jnp.where(is_k, ci, jnp.where(is_w, w_scale, 0.0)),
                       jnp.where(is_k, s1i, 0.0), jnp.where(is_k, s2i, 0.0)])
    gains = jnp.stack([q_gain, k_gain]).reshape(2, 1, ATT_HEAD_DIM)
    n_main = n_qk + n_v + n_qi
    w_tail = jnp.pad(w_tail_t.astype(bf), ((0, LANES - w_tail_t.shape[0]), (0, 0)))

    w_out_stack, w_out_layer = w_out
    wo_rows = w_out_stack.shape[1] // (n // tm)
    wo_sub = d // SEARCH_TRIPS

    row = lambda width: pl.BlockSpec((tm, width), lambda i: (i, 0))
    whole = lambda a: pl.BlockSpec(a.shape, lambda i: (0,) * a.ndim, pipeline_mode=pl.Buffered(1))
    tbl = pl.BlockSpec((3, tm, LANES), lambda i: (0, i % (seq // tm), 0))
    qk, v, qi, kw, wo = pl.pallas_call(
        functools.partial(_att_proj_kernel, half=half, half_i=half_i),
        grid=(n // tm,),
        in_specs=[row(d), whole(g_mix.reshape(1, d)),
                  pl.BlockSpec((n_main, d), lambda i: (0, 0), pipeline_mode=pl.Buffered(1)), whole(w_tail),
                  whole(gains), tbl, tbl, tbl,
                  pl.BlockSpec((None, wo_rows, d), lambda i: (w_out_layer, i, 0))],
        out_specs=[row(n_qk), row(n_v), row(n_qi), row(LANES),
                   pl.BlockSpec((SEARCH_TRIPS, wo_rows, wo_sub), lambda i: (0, i, 0))],
        out_shape=[jax.ShapeDtypeStruct((n, n_qk), bf), jax.ShapeDtypeStruct((n, n_v), bf),
                   jax.ShapeDtypeStruct((n, n_qi), bf), jax.ShapeDtypeStruct((n, LANES), jnp.float32),
                   jax.ShapeDtypeStruct((SEARCH_TRIPS, w_out_stack.shape[1], wo_sub), bf)],
        scratch_shapes=[pltpu.VMEM((tm, d), bf)],
        compiler_params=_params("parallel", fuse_inputs=[k == 2 for k in range(9)]),
        name="att_proj",
    )(h, g_mix.reshape(1, d), w_t, w_tail, gains, tab, tab_i, tab_t, w_out_stack)

    return _attention(qk, v, qi, kw, h, wo, batch, seq)


def _mlstm_layer(h, batch, seq, g_mix, w_t, w_tail_t, b_gate, h_gain, w_out):
    bf = jnp.bfloat16
    n, d = h.shape
    n_qk, n_v = 2 * ML_HEADS * ML_QK_DIM, ML_HEADS * ML_V_DIM
    n_g = 2 * ML_HEADS
    tm = min(PROJ_ROWS, seq)
    g2 = g_mix.reshape(1, d)
    wg_t = w_tail_t.astype(bf)
    n_main = n_qk + n_v + d

    row = lambda width: pl.BlockSpec((tm, width), lambda i: (i, 0))
    whole = lambda a: pl.BlockSpec(a.shape, lambda i: (0,) * a.ndim, pipeline_mode=pl.Buffered(1))
    qk, v, og, gates_t = pl.pallas_call(
        _ml_proj_kernel,
        grid=(n // tm,),
        in_specs=[row(d), whole(g2), pl.BlockSpec((n_main, d), lambda i: (0, 0), pipeline_mode=pl.Buffered(1)),
                  whole(wg_t), pl.BlockSpec((n_g, 1), lambda i: (0, 0))],
        out_specs=[row(n_qk), row(n_v), row(d), pl.BlockSpec((n_g, tm), lambda i: (0, i))],
        out_shape=[jax.ShapeDtypeStruct((n, n_qk), bf), jax.ShapeDtypeStruct((n, n_v), bf),
                   jax.ShapeDtypeStruct((n, d), jnp.float32), jax.ShapeDtypeStruct((n_g, n), jnp.float32)],
        scratch_shapes=[pltpu.VMEM((tm, d), bf)],
        compiler_params=_params("parallel"),
        name="ml_proj",
    )(h, g2, w_t, wg_t, b_gate.reshape(n_g, 1))
    y = _mlstm(qk, v, og, gates_t, h_gain, batch, seq)
    return _outproj(y, w_out.astype(bf), h)


def kernel(x, norm_mix, norm_ffn, att_w_in, att_q_gain, att_k_gain, att_w_out, ml_w_in, ml_b_gate, ml_h_gain,
           ml_w_out, ffn_w_up, ffn_w_down):
    batch, seq, d = x.shape
    bf = jnp.bfloat16
    h = x.reshape(batch * seq, d)
    depth = norm_mix.shape[0]
    att_w_in_t, ml_w_in_t = jnp.swapaxes(att_w_in, 1, 2), jnp.swapaxes(ml_w_in, 1, 2)
    def tail_t(w_t_f32, idx):
        aligned = w_t_f32.shape[1] // LANES * LANES
        have = ready["w_in"]
        return w_t_f32[idx, aligned:, :]

    ready = {"w_in": att_w_in_t[0].astype(bf)}
    for i in range(depth):
        j = i // 2
        if i % 2 == 0:
            w_out = (ready["w_out"][None], 0) if "w_out" in ready else (att_w_out, j)
            h = _attention_layer(h, batch, seq, norm_mix[i], ready["w_in"], tail_t(att_w_in_t, j), att_q_gain[j],
                                 att_k_gain[j], w_out)
        else:
            h = _mlstm_layer(h, batch, seq, norm_mix[i], ready["w_in"], tail_t(ml_w_in_t, j), ml_b_gate[j],
                             ml_h_gain[j], ready["w_out"])
        nxt = i + 1
        cast = []
        if nxt < depth:
            w_in_t, w_out = (att_w_in_t, att_w_out) if nxt % 2 == 0 else (ml_w_in_t, ml_w_out)
            cast = [(ffn_w_up, nxt), (ffn_w_down, nxt), (w_in_t, nxt // 2), (w_out, nxt // 2)]
        if "w_up" in ready:
            h, done = _ffn(h, norm_ffn[i], ready["w_up"], ready["w_down"], cast)
        else:
            first, w_up, w_down = _ffn_first_tile(h, norm_ffn[i], ffn_w_up, ffn_w_down, i)
            h = lax.dynamic_update_slice(h, first, (0, 0))
            h, done = _ffn(h, norm_ffn[i], w_up, w_down, cast, first_done=True)
        if done:
            ready = dict(zip(("w_up", "w_down", "w_in", "w_out"), done))
    return h.reshape(batch, seq, d)
```
